```python
import jax, jax.numpy as jnp
from jax import lax
import numpy as np

D_MODEL = 1024
BATCH = 8
SEQ = 8192
DEPTH = 2
DEC_BATCH = 16
DEC_SEQ = 4096
PAST_LEN = 128

HEAD_DIM = 64
N_HEADS = D_MODEL // HEAD_DIM
A_Q_HEADS = N_HEADS // 2
A_KV_HEADS = 2
B_HEADS = N_HEADS - A_Q_HEADS
C_HEADS = N_HEADS
A_Q = A_Q_HEADS * HEAD_DIM
A_KV = A_KV_HEADS * HEAD_DIM
B_W = B_HEADS * HEAD_DIM
W_IN_EVEN = A_Q + 2 * A_KV + 3 * B_W
W_MIX_EVEN = A_Q + B_W
C_W = C_HEADS * HEAD_DIM
W_IN_ODD = 3 * C_W
GRID_W = 64
NA_ROWS = 8
NA_COLS = 16
C_GROUPS = ((128, 1), (512, 4), (2048, 16))
Q_BLOCK = 128
ROPE_THETA = 10000.0
D_FF = -(-8 * D_MODEL // (3 * 256)) * 256
N_EVEN = (DEPTH + 1) // 2
N_ODD = DEPTH // 2
EPS = 1e-6
NEG = -1e30

kernel_name = "hybrid_bidir_axial_gqa_natten_dilated_encoder"


def _rmsnorm(x, g):
    xf = x.astype(jnp.float32)
    y = xf * lax.rsqrt(jnp.mean(xf * xf, axis=-1, keepdims=True) + EPS)
    return (y * g.astype(jnp.float32)).astype(x.dtype)


def _rope_1d(x, pos):
    n = x.shape[-1] // 2
    freqs = jnp.power(ROPE_THETA, -jnp.arange(n, dtype=jnp.float32) / n)
    ang = pos.astype(jnp.float32)[:, None] * freqs[None, :]
    cos = jnp.cos(ang)[:, None, :]
    sin = jnp.sin(ang)[:, None, :]
    xf = x.astype(jnp.float32)
    x1, x2 = xf[..., :n], xf[..., n:]
    return jnp.concatenate([x1 * cos - x2 * sin, x1 * sin + x2 * cos], axis=-1).astype(x.dtype)


def _axial_rope(x, seq):
    t = jnp.arange(seq)
    half = HEAD_DIM // 2
    return jnp.concatenate([_rope_1d(x[..., :half], t // GRID_W),
                            _rope_1d(x[..., half:], t % GRID_W)], axis=-1)


def _mixer_axial_gqa(q, k, v, gq, gk):
    bn, seq = q.shape[0], q.shape[1]
    q = _axial_rope(_rmsnorm(q, gq), seq)
    k = _axial_rope(_rmsnorm(k, gk), seq)
    grp = A_Q_HEADS // A_KV_HEADS
    nb = seq // Q_BLOCK
    qb = q.reshape(bn, nb, Q_BLOCK, A_KV_HEADS, grp, HEAD_DIM).transpose(1, 0, 2, 3, 4, 5)
    scale = HEAD_DIM ** -0.5

    def one_block(qblk):
        s = jnp.einsum('bqkgd,bskd->bkgqs', qblk, k).astype(jnp.float32) * scale
        p = jax.nn.softmax(s, axis=-1)
        return jnp.einsum('bkgqs,bskd->bqkgd', p.astype(v.dtype), v)

    o = lax.map(one_block, qb)
    return o.transpose(1, 0, 2, 3, 4, 5).reshape(bn, seq, A_Q)


def _mixer_neighbourhood(q, k, v, rpb):
    bn, seq = q.shape[0], q.shape[1]
    rows = seq // GRID_W
    wr = min(NA_ROWS, rows)
    r = jnp.arange(rows)
    rs = jnp.clip(r - wr // 2, 0, rows - wr)
    row_idx = rs[:, None] + jnp.arange(wr)[None, :]

    def grid(a):
        return a.reshape(bn, rows, GRID_W, B_HEADS, HEAD_DIM)

    kg = grid(k)[:, row_idx].reshape(bn, rows, wr * GRID_W, B_HEADS, HEAD_DIM)
    vg = grid(v)[:, row_idx].reshape(bn, rows, wr * GRID_W, B_HEADS, HEAD_DIM)
    s = jnp.einsum('brqhd,brkhd->brhqk', grid(q), kg).astype(jnp.float32) * (HEAD_DIM ** -0.5)
    c = jnp.arange(GRID_W)
    cs = jnp.clip(c - NA_COLS // 2, 0, GRID_W - NA_COLS)
    col_ok = (c[None, :] >= cs[:, None]) & (c[None, :] < cs[:, None] + NA_COLS)
    dr = row_idx - r[:, None] + (NA_ROWS - 1)
    dc = jnp.clip(c[None, :] - c[:, None] + (NA_COLS - 1), 0, 2 * NA_COLS - 2)
    bias = rpb[:, dr[:, None, :, None], dc[None, :, None, :]]
    bias = bias.transpose(1, 0, 2, 3, 4).reshape(rows, B_HEADS, GRID_W, wr * GRID_W).astype(jnp.float32)
    mask = jnp.tile(col_ok, (1, wr))
    s = jnp.where(mask, s + bias[None], NEG)
    p = jax.nn.softmax(s, axis=-1)
    o = jnp.einsum('brhqk,brkhd->brqhd', p.astype(v.dtype), vg)
    return o.reshape(bn, seq, B_W)


def _band_attn(q, k, v, half, slope):
    n, length = q.shape[0], q.shape[1]
    kb_len = Q_BLOCK + 2 * half
    nb = -(-length // Q_BLOCK)
    lp = nb * Q_BLOCK
    qp = jnp.pad(q, ((0, 0), (0, lp - length), (0, 0), (0, 0)))
    kp = jnp.pad(k, ((0, 0), (half, lp - length + half), (0, 0), (0, 0)))
    vp = jnp.pad(v, ((0, 0), (half, lp - length + half), (0, 0), (0, 0)))
    blk = jnp.arange(nb)[:, None] * Q_BLOCK + jnp.arange(kb_len)[None, :]
    kb = kp[:, blk]
    vb = vp[:, blk]
    qb = qp.reshape(n, nb, Q_BLOCK, q.shape[2], HEAD_DIM)
    s = jnp.einsum('nbqhd,nbkhd->nbhqk', qb, kb).astype(jnp.float32) * (HEAD_DIM ** -0.5)
    qi = jnp.arange(nb)[:, None] * Q_BLOCK + jnp.arange(Q_BLOCK)[None, :]
    kj = jnp.arange(nb)[:, None] * Q_BLOCK - half + jnp.arange(kb_len)[None, :]
    dist = jnp.abs(qi[:, :, None] - kj[:, None, :])
    valid = ((kj >= 0) & (kj < length))[:, None, :] & (dist <= half)
    s = s - slope[None, :, None, None] * dist[:, None, :, :].astype(jnp.float32)
    s = jnp.where(valid[:, None], s, NEG)
    m = jnp.max(s, axis=-1, keepdims=True)
    e = jnp.exp(s - m)
    den = jnp.sum(e, axis=-1, keepdims=True)
    lse = (m + jnp.log(den))[..., 0]
    p = e / den
    o = jnp.einsum('nbhqk,nbkhd->nbqhd', p.astype(v.dtype), vb).reshape(n, lp, q.shape[2], HEAD_DIM)[:, :length]
    lse = lse.transpose(0, 1, 3, 2).reshape(n, lp, q.shape[2])[:, :length]
    return o, lse


def _dilated_group(q, k, v, dil, half, slopes):
    bn, seq, h = q.shape[0], q.shape[1], q.shape[2]
    length = seq // dil

    def fold(a):
        return a.reshape(bn, length, dil, h, HEAD_DIM).transpose(0, 2, 1, 3, 4).reshape(bn * dil, length, h, HEAD_DIM)

    o, lse = _band_attn(fold(q), fold(k), fold(v), half, slopes * dil)
    o = o.reshape(bn, dil, length, h, HEAD_DIM).transpose(0, 2, 1, 3, 4).reshape(bn, seq, h, HEAD_DIM)
    lse = lse.reshape(bn, dil, length, h).transpose(0, 2, 1, 3).reshape(bn, seq, h)
    return o, lse


def _mixer_dilated(hn, w_in, w_out):
    bn, seq = hn.shape[0], hn.shape[1]
    qkv = (hn @ w_in).reshape(bn, seq, 3, C_HEADS, HEAD_DIM)
    q, k, v = qkv[:, :, 0], qkv[:, :, 1], qkv[:, :, 2]
    slopes = jnp.exp2(-8.0 * (jnp.arange(C_HEADS, dtype=jnp.float32) + 1.0) / C_HEADS)
    outs, lses = [], []
    for window, dil in C_GROUPS:
        o, l = _dilated_group(q, k, v, dil, window // (2 * dil), slopes)
        outs.append(o)
        lses.append(l)
    wts = jax.nn.softmax(jnp.stack(lses, axis=0), axis=0)
    o = jnp.sum(wts[..., None] * jnp.stack(outs, axis=0).astype(jnp.float32), axis=0).astype(hn.dtype)
    return o.reshape(bn, seq, C_W) @ w_out


def _mixer_even(hn, w_in, gq, gk, rpb, w_out):
    bn, seq = hn.shape[0], hn.shape[1]
    p = hn @ w_in
    cuts = [A_Q, A_Q + A_KV, A_Q + 2 * A_KV, A_Q + 2 * A_KV + B_W, A_Q + 2 * A_KV + 2 * B_W]
    qa, ka, va, qb, kb, vb = jnp.split(p, cuts, axis=-1)
    oa = _mixer_axial_gqa(qa.reshape(bn, seq, A_Q_HEADS, HEAD_DIM),
                          ka.reshape(bn, seq, A_KV_HEADS, HEAD_DIM),
                          va.reshape(bn, seq, A_KV_HEADS, HEAD_DIM), gq, gk)
    ob = _mixer_neighbourhood(qb.reshape(bn, seq, B_HEADS, HEAD_DIM),
                              kb.reshape(bn, seq, B_HEADS, HEAD_DIM),
                              vb.reshape(bn, seq, B_HEADS, HEAD_DIM), rpb)
    return jnp.concatenate([oa, ob], axis=-1) @ w_out


def _swiglu(hn, w_gu, w_dn):
    g, u = jnp.split(hn @ w_gu, 2, axis=-1)
    return (jax.nn.silu(g) * u) @ w_dn


def _trunk(x, norm_mix, norm_ffn, norm_final, w_in_even, a_q_norm, a_k_norm, na_rpb,
           w_out_even, w_in_odd, w_out_odd, w_gate_up, w_down):
    for i in range(DEPTH):
        hn = _rmsnorm(x, norm_mix[i])
        j = i // 2
        if i % 2 == 0:
            mix = _mixer_even(hn, w_in_even[j], a_q_norm[j], a_k_norm[j], na_rpb[j], w_out_even[j])
        else:
            mix = _mixer_dilated(hn, w_in_odd[j], w_out_odd[j])
        x = x + mix.astype(x.dtype)
        x = x + _swiglu(_rmsnorm(x, norm_ffn[i]), w_gate_up[i], w_down[i]).astype(x.dtype)
    return _rmsnorm(x, norm_final)


def setup_inputs(seed: int = 0) -> dict:
    key = jax.random.key(seed)
    ks = jax.random.split(key, 16)
    f32 = jnp.float32
    nrm = lambda k, s: jax.random.normal(k, s, dtype=f32)
    return {
        "x_prompt": nrm(ks[0], (BATCH, SEQ, D_MODEL)),
        "x_sample": nrm(ks[1], (DEC_BATCH, DEC_SEQ, D_MODEL)),
        "norm_mix": 1.0 + 0.02 * nrm(ks[2], (DEPTH, D_MODEL)),
        "norm_ffn": 1.0 + 0.02 * nrm(ks[3], (DEPTH, D_MODEL)),
        "norm_final": 1.0 + 0.02 * nrm(ks[4], (D_MODEL,)),
        "w_in_even": nrm(ks[5], (N_EVEN, D_MODEL, W_IN_EVEN)) * D_MODEL ** -0.5,
        "a_q_norm": 1.0 + 0.02 * nrm(ks[6], (N_EVEN, HEAD_DIM)),
        "a_k_norm": 1.0 + 0.02 * nrm(ks[7], (N_EVEN, HEAD_DIM)),
        "na_rpb": 0.1 * nrm(ks[8], (N_EVEN, B_HEADS, 2 * NA_ROWS - 1, 2 * NA_COLS - 1)),
        "w_out_even": nrm(ks[9], (N_EVEN, W_MIX_EVEN, D_MODEL)) * W_MIX_EVEN ** -0.5,
        "w_in_odd": nrm(ks[10], (N_ODD, D_MODEL, W_IN_ODD)) * D_MODEL ** -0.5,
        "w_out_odd": nrm(ks[11], (N_ODD, C_W, D_MODEL)) * C_W ** -0.5,
        "w_gate_up": nrm(ks[12], (DEPTH, D_MODEL, 2 * D_FF)) * D_MODEL ** -0.5,
        "w_down": nrm(ks[13], (DEPTH, D_FF, D_MODEL)) * D_FF ** -0.5,
    }


def reference(x_prompt, x_sample, norm_mix, norm_ffn, norm_final, w_in_even, a_q_norm, a_k_norm,
              na_rpb, w_out_even, w_in_odd, w_out_odd, w_gate_up, w_down):
    y_prompt = _trunk(x_prompt, norm_mix, norm_ffn, norm_final, w_in_even, a_q_norm, a_k_norm, na_rpb,
                      w_out_even, w_in_odd, w_out_odd, w_gate_up, w_down)
    y_sample = _trunk(x_sample, norm_mix, norm_ffn, norm_final, w_in_even, a_q_norm, a_k_norm, na_rpb,
                      w_out_even, w_in_odd, w_out_odd, w_gate_up, w_down)
    return (y_prompt, y_sample)
```

```python
import functools

import jax
import jax.numpy as jnp
import numpy as np
from jax import lax
from jax.experimental import pallas as pl
from jax.experimental.pallas import tpu as pltpu

F32 = jnp.float32
BF16 = jnp.bfloat16

D_MODEL = 1024
HEAD_DIM = 64
A_Q = 512
A_KV = 128
B_W = 512
C_W = 1024
C_HEADS = 16
B_HEADS = 8
GRID_W = 64
NA_ROWS = 8
NA_COLS = 16
C_GROUPS = ((128, 1), (512, 4), (2048, 16))
ROPE_THETA = 10000.0
D_FF = 2816
FF_CHUNK = 1408
EPS = 1e-6
NEG = -1e30
SCALE = HEAD_DIM ** -0.5

ROW_TILE = 512
A_TQ = 256
A_TK = 256
B_TQ = 256
B_TK = 3 * B_TQ
C_TQ = 128
C_HALF = 64
VMEM_LIMIT = 56 * 1024 * 1024

_NT = (((1,), (1,)), ((), ()))


def _resident(shape):
    zeros = (0,) * len(shape)
    return pl.BlockSpec(shape, lambda *_: zeros, pipeline_mode=pl.Buffered(1))


def _rms(x, gain):
    ms = jnp.mean(x * x, axis=-1, keepdims=True)
    return x * lax.rsqrt(ms + EPS) * gain


def _in_even_kernel(x_ref, g_ref, w_ref, gq_ref, gk_ref, cos_ref, sa_ref, sb_ref, bd_ref,
                    qa_ref, ka_ref, vat_ref, qb_ref, kb_ref, vb_ref):
    hn = _rms(x_ref[...], g_ref[...]).astype(BF16)
    p = jnp.dot(hn, w_ref[...], preferred_element_type=F32)
    cos, sa, sb, bd = cos_ref[...], sa_ref[...], sb_ref[...], bd_ref[...]

    def headnorm_rope(c, gain):
        c2 = c * c
        hi = c2.astype(BF16)
        lo = (c2 - hi.astype(F32)).astype(BF16)
        ms = (jnp.dot(hi, bd, preferred_element_type=F32)
              + jnp.dot(lo, bd, preferred_element_type=F32))
        y = c * lax.rsqrt(ms + EPS) * gain
        return y * cos + pltpu.roll(y, 112, 1) * sa + pltpu.roll(y, 16, 1) * sb

    for j in range(A_Q // 128):
        qa_ref[:, j * 128:(j + 1) * 128] = headnorm_rope(p[:, j * 128:(j + 1) * 128], gq_ref[...]).astype(BF16)
    ka_ref[...] = headnorm_rope(p[:, A_Q:A_Q + A_KV], gk_ref[...]).astype(BF16)
    va = p[:, A_Q + A_KV:A_Q + 2 * A_KV]
    for j in range(va.shape[0] // A_TK):
        vat_ref[j] = va[j * A_TK:(j + 1) * A_TK, :].T.astype(BF16)
    o = A_Q + 2 * A_KV
    qb_ref[...] = (p[:, o:o + B_W] * SCALE).astype(BF16)
    kb_ref[...] = p[:, o + B_W:o + 2 * B_W].astype(BF16)
    vb_ref[...] = p[:, o + 2 * B_W:o + 3 * B_W].astype(BF16)


def _in_even(x2, seq, g, w, gq, gk, cos, sa, sb, bd):
    n = x2.shape[0]
    tm = ROW_TILE
    nt = seq // tm
    row = lambda i: (i, 0)
    pos = lambda i: (i % nt, 0)
    out_shape = (
        jax.ShapeDtypeStruct((n, A_Q), BF16), jax.ShapeDtypeStruct((n, A_KV), BF16),
        jax.ShapeDtypeStruct((n // A_TK, A_KV, A_TK), BF16),
        jax.ShapeDtypeStruct((n, B_W), BF16), jax.ShapeDtypeStruct((n, B_W), BF16),
        jax.ShapeDtypeStruct((n, B_W), BF16))
    return pl.pallas_call(
        _in_even_kernel,
        grid=(n // tm,),
        in_specs=[pl.BlockSpec((tm, D_MODEL), row), _resident((1, D_MODEL)), _resident(w.shape),
                  _resident((1, 128)), _resident((1, 128)),
                  pl.BlockSpec((tm, 128), pos), pl.BlockSpec((tm, 128), pos), pl.BlockSpec((tm, 128), pos),
                  _resident((128, 128))],
        out_specs=(pl.BlockSpec((tm, A_Q), row), pl.BlockSpec((tm, A_KV), row),
                   pl.BlockSpec((tm // A_TK, A_KV, A_TK), lambda i: (i, 0, 0)),
                   pl.BlockSpec((tm, B_W), row), pl.BlockSpec((tm, B_W), row), pl.BlockSpec((tm, B_W), row)),
        out_shape=out_shape,
        compiler_params=pltpu.CompilerParams(dimension_semantics=("parallel",), vmem_limit_bytes=VMEM_LIMIT),
        name="in_even",
    )(x2, g, w, gq, gk, cos, sa, sb, bd)


def _mix_a_kernel(q_ref, k_ref, vt_ref, o_ref, qs_ref, m_ref, l_ref, acc_ref, *, n_chunks):
    tq = A_TQ
    lane = lax.broadcasted_iota(jnp.int32, (tq, 128), 1)
    low = lane < HEAD_DIM
    for jj in range(4):
        grp = jj // 2
        cf = q_ref[:, jj * 128:(jj + 1) * 128].astype(F32)
        cr = pltpu.roll(cf, HEAD_DIM, 1)
        if grp == 0:
            even, odd = jnp.where(low, cf, 0.0), jnp.where(low, cr, 0.0)
        else:
            even, odd = jnp.where(low, 0.0, cr), jnp.where(low, 0.0, cf)
        qs_ref[(2 * jj) * tq:(2 * jj + 1) * tq, :] = even.astype(BF16)
        qs_ref[(2 * jj + 1) * tq:(2 * jj + 2) * tq, :] = odd.astype(BF16)
    m_ref[...] = jnp.full(m_ref.shape, NEG, F32)
    l_ref[...] = jnp.zeros(l_ref.shape, F32)
    acc_ref[...] = jnp.zeros(acc_ref.shape, F32)

    def chunk(c, carry):
        k = k_ref[pl.ds(pl.multiple_of(c * A_TK, A_TK), A_TK), :]
        s = lax.dot_general(k, qs_ref[...], _NT, preferred_element_type=F32)
        m_old = m_ref[...]
        m_new = jnp.maximum(m_old, jnp.max(s, axis=0, keepdims=True))
        alpha = jnp.exp(m_old - m_new)
        e = jnp.exp(s - m_new)
        l_ref[...] = alpha * l_ref[...] + jnp.sum(e, axis=0, keepdims=True)
        m_ref[...] = m_new
        p = e.astype(BF16)
        for grp in range(2):
            cols = slice(grp * 4 * tq, (grp + 1) * 4 * tq)
            vt = vt_ref[c, grp * HEAD_DIM:(grp + 1) * HEAD_DIM, :]
            pv = jnp.dot(vt, p[:, cols], preferred_element_type=F32)
            acc_ref[grp] = alpha[:, cols] * acc_ref[grp] + pv
        return carry

    lax.fori_loop(0, n_chunks, chunk, 0)

    for grp in range(2):
        cols = slice(grp * 4 * tq, (grp + 1) * 4 * tq)
        o = acc_ref[grp] / l_ref[:, cols]
        for jj in range(2):
            pair = jnp.concatenate([o[:, (2 * jj) * tq:(2 * jj + 1) * tq],
                                    o[:, (2 * jj + 1) * tq:(2 * jj + 2) * tq]], axis=0)
            c0 = (2 * grp + jj) * 128
            o_ref[:, c0:c0 + 128] = pair.T.astype(BF16)


def _mix_a(qa, ka, vat, bn, seq):
    n = qa.shape[0]
    nq = seq // A_TQ
    nc = seq // A_TK
    return pl.pallas_call(
        functools.partial(_mix_a_kernel, n_chunks=nc),
        grid=(bn, nq),
        in_specs=[pl.BlockSpec((A_TQ, A_Q), lambda b, i: (b * nq + i, 0)),
                  pl.BlockSpec((seq, A_KV), lambda b, i: (b, 0)),
                  pl.BlockSpec((nc, A_KV, A_TK), lambda b, i: (b, 0, 0))],
        out_specs=pl.BlockSpec((A_TQ, A_Q), lambda b, i: (b * nq + i, 0)),
        out_shape=jax.ShapeDtypeStruct((n, A_Q), BF16),
        scratch_shapes=[pltpu.VMEM((8 * A_TQ, 128), BF16), pltpu.VMEM((1, 8 * A_TQ), F32),
                        pltpu.VMEM((1, 8 * A_TQ), F32), pltpu.VMEM((2, HEAD_DIM, 4 * A_TQ), F32)],
        compiler_params=pltpu.CompilerParams(dimension_semantics=("parallel", "parallel"),
                                             vmem_limit_bytes=VMEM_LIMIT),
        name="mix_a",
    )(qa, ka, vat)


def _mix_b_kernel(q_ref, k0_ref, k1_ref, k2_ref, v0_ref, v1_ref, v2_ref, bias_ref, o_ref):
    for h in range(B_HEADS):
        hs = slice(h * HEAD_DIM, (h + 1) * HEAD_DIM)
        q = q_ref[:, hs]
        k = jnp.concatenate([k0_ref[:, hs], k1_ref[:, hs], k2_ref[:, hs]], axis=0)
        v = jnp.concatenate([v0_ref[:, hs], v1_ref[:, hs], v2_ref[:, hs]], axis=0)
        s = lax.dot_general(q, k, _NT, preferred_element_type=F32) + bias_ref[0, h]
        m = jnp.max(s, axis=-1, keepdims=True)
        e = jnp.exp(s - m)
        p = (e / jnp.sum(e, axis=-1, keepdims=True)).astype(BF16)
        o_ref[:, hs] = jnp.dot(p, v, preferred_element_type=F32).astype(BF16)


def _mix_b(qb, kb, vb, bias, bn, seq):
    n = qb.shape[0]
    nb = seq // B_TQ
    q_map = lambda b, i: (b * nb + i, 0)

    def kv_map(j):
        return lambda b, i: (b * nb + jnp.clip(i - 1, 0, nb - 3) + j, 0)

    def bias_map(b, i):
        return (jnp.where(i == 0, 0, jnp.where(i == nb - 1, 2, 1)), 0, 0, 0)

    blk = lambda m: pl.BlockSpec((B_TQ, B_W), m)
    return pl.pallas_call(
        _mix_b_kernel,
        grid=(bn, nb),
        in_specs=[blk(q_map), blk(kv_map(0)), blk(kv_map(1)), blk(kv_map(2)),
                  blk(kv_map(0)), blk(kv_map(1)), blk(kv_map(2)),
                  pl.BlockSpec((1, B_HEADS, B_TQ, B_TK), bias_map)],
        out_specs=blk(q_map),
        out_shape=jax.ShapeDtypeStruct((n, B_W), BF16),
        compiler_params=pltpu.CompilerParams(dimension_semantics=("parallel", "parallel"),
                                             vmem_limit_bytes=VMEM_LIMIT),
        name="mix_b",
    )(qb, kb, kb, kb, vb, vb, vb, bias)


def _neighbourhood_bias(rpb, rows):
    rr = np.arange(B_TQ) // GRID_W
    qc = np.arange(B_TQ) % GRID_W
    kr = np.arange(B_TK) // GRID_W
    kc = np.arange(B_TK) % GRID_W
    cs = np.clip(qc - NA_COLS // 2, 0, GRID_W - NA_COLS)
    col_ok = (kc[None, :] >= cs[:, None]) & (kc[None, :] < cs[:, None] + NA_COLS)
    dc = np.clip(kc[None, :] - qc[:, None] + NA_COLS - 1, 0, 2 * NA_COLS - 2)
    tables = []
    for r0, ws in ((0, 0), (4, 0), (rows - 4, rows - 12)):
        r = r0 + rr
        rs = np.clip(r - NA_ROWS // 2, 0, rows - NA_ROWS)
        key_row = ws + kr
        row_ok = (key_row[None, :] >= rs[:, None]) & (key_row[None, :] < rs[:, None] + NA_ROWS)
        dr = np.clip(key_row[None, :] - r[:, None] + NA_ROWS - 1, 0, 2 * NA_ROWS - 2)
        vals = rpb[:, dr, dc].astype(F32)
        tables.append(jnp.where((row_ok & col_ok)[None], vals, NEG))
    return jnp.stack(tables, axis=0)


def _ffn(x1, gain, wg_ref, wu_ref, wd_ref):
    hn = _rms(x1, gain).astype(BF16)
    acc = None
    for c in range(D_FF // FF_CHUNK):
        cs = slice(c * FF_CHUNK, (c + 1) * FF_CHUNK)
        gate = jnp.dot(hn, wg_ref[:, cs], preferred_element_type=F32)
        up = jnp.dot(hn, wu_ref[:, cs], preferred_element_type=F32)
        act = (gate * (1.0 / (1.0 + jnp.exp(-gate))) * up).astype(BF16)
        d = jnp.dot(act, wd_ref[cs, :], preferred_element_type=F32)
        acc = d if acc is None else acc + d
    return x1 + acc


def _post_even_kernel(x_ref, oa_ref, ob_ref, woa_ref, wob_ref, g_ref, wg_ref, wu_ref, wd_ref, y_ref):
    mix = (jnp.dot(oa_ref[...], woa_ref[...], preferred_element_type=F32)
           + jnp.dot(ob_ref[...], wob_ref[...], preferred_element_type=F32))
    y_ref[...] = _ffn(x_ref[...] + mix, g_ref[...], wg_ref, wu_ref, wd_ref)


def _post_even(x2, oa, ob, woa, wob, g, wg, wu, wd):
    n = x2.shape[0]
    tm = ROW_TILE
    row = lambda i: (i, 0)
    return pl.pallas_call(
        _post_even_kernel,
        grid=(n // tm,),
        in_specs=[pl.BlockSpec((tm, D_MODEL), row), pl.BlockSpec((tm, A_Q), row), pl.BlockSpec((tm, B_W), row),
                  _resident(woa.shape), _resident(wob.shape), _resident((1, D_MODEL)),
                  _resident(wg.shape), _resident(wu.shape), _resident(wd.shape)],
        out_specs=pl.BlockSpec((tm, D_MODEL), row),
        out_shape=jax.ShapeDtypeStruct((n, D_MODEL), F32),
        compiler_params=pltpu.CompilerParams(dimension_semantics=("parallel",), vmem_limit_bytes=VMEM_LIMIT),
        name="post_even",
    )(x2, oa, ob, woa, wob, g, wg, wu, wd)


def _post_odd_kernel(x_ref, o1_ref, o2_ref, o3_ref, l1_ref, l2_ref, l3_ref, ex_ref, wo_ref, g_ref,
                     wg_ref, wu_ref, wd_ref, gf_ref, y_ref):
    l1, l2, l3 = l1_ref[...], l2_ref[...], l3_ref[...]
    mx = jnp.maximum(jnp.maximum(l1, l2), l3)
    e1, e2, e3 = jnp.exp(l1 - mx), jnp.exp(l2 - mx), jnp.exp(l3 - mx)
    den = e1 + e2 + e3
    ex = ex_ref[...]

    def widen(w):
        hi = w.astype(BF16)
        lo = (w - hi.astype(F32)).astype(BF16)
        return jnp.dot(hi, ex, preferred_element_type=F32) + jnp.dot(lo, ex, preferred_element_type=F32)

    o = widen(e1 / den) * o1_ref[...] + widen(e2 / den) * o2_ref[...] + widen(e3 / den) * o3_ref[...]
    mix = jnp.dot(o.astype(BF16), wo_ref[...], preferred_element_type=F32)
    y = _ffn(x_ref[...] + mix, g_ref[...], wg_ref, wu_ref, wd_ref)
    y_ref[...] = _rms(y, gf_ref[...])


def _post_odd(x2, os_, ls_, ex, wo, g, wg, wu, wd, gf):
    n = x2.shape[0]
    tm = ROW_TILE
    row = lambda i: (i, 0)
    wide = pl.BlockSpec((tm, D_MODEL), row)
    narrow = pl.BlockSpec((tm, C_HEADS), row)
    return pl.pallas_call(
        _post_odd_kernel,
        grid=(n // tm,),
        in_specs=[wide, wide, wide, wide, narrow, narrow, narrow, _resident(ex.shape), _resident(wo.shape),
                  _resident((1, D_MODEL)), _resident(wg.shape), _resident(wu.shape), _resident(wd.shape),
                  _resident((1, D_MODEL))],
        out_specs=wide,
        out_shape=jax.ShapeDtypeStruct((n, D_MODEL), F32),
        compiler_params=pltpu.CompilerParams(dimension_semantics=("parallel",), vmem_limit_bytes=VMEM_LIMIT),
        name="post_odd",
    )(x2, *os_, *ls_, ex, wo, g, wg, wu, wd, gf)


def _in_odd_kernel(x_ref, g_ref, w_ref, q_ref, k_ref, v_ref):
    hn = _rms(x_ref[...], g_ref[...]).astype(BF16)
    p = jnp.dot(hn, w_ref[...], preferred_element_type=F32)
    q_ref[...] = (p[:, :C_W] * SCALE).astype(BF16)
    k_ref[...] = p[:, C_W:2 * C_W].astype(BF16)
    v_ref[...] = p[:, 2 * C_W:].astype(BF16)


def _in_odd(x2, g, w):
    n = x2.shape[0]
    tm = ROW_TILE
    row = lambda i: (i, 0)
    wide = pl.BlockSpec((tm, C_W), row)
    shape = jax.ShapeDtypeStruct((n, C_W), BF16)
    return pl.pallas_call(
        _in_odd_kernel,
        grid=(n // tm,),
        in_specs=[pl.BlockSpec((tm, D_MODEL), row), _resident((1, D_MODEL)), _resident(w.shape)],
        out_specs=(wide, wide, wide),
        out_shape=(shape, shape, shape),
        compiler_params=pltpu.CompilerParams(dimension_semantics=("parallel",), vmem_limit_bytes=VMEM_LIMIT),
        name="in_odd",
    )(x2, g, w)


def _band_kernel(sl_ref, q_ref, k0_ref, k1_ref, k2_ref, k3_ref, v0_ref, v1_ref, v2_ref, v3_ref,
                 o_ref, lse_ref, *, length):
    i = pl.program_id(1)
    tk = C_TQ + 2 * C_HALF
    row = lax.broadcasted_iota(jnp.int32, (C_TQ, tk), 0)
    col = lax.broadcasted_iota(jnp.int32, (C_TQ, tk), 1)
    dist = jnp.abs(col - C_HALF - row)
    kj = i * C_TQ - C_HALF + col
    valid = (kj >= 0) & (kj < length) & (dist <= C_HALF)
    distf = dist.astype(F32)
    for h in range(C_HEADS):
        hs = slice(h * HEAD_DIM, (h + 1) * HEAD_DIM)
        q = q_ref[:, hs]
        k = jnp.concatenate([k0_ref[:, hs], k1_ref[:, hs], k2_ref[:, hs], k3_ref[:, hs]], axis=0)
        v = jnp.concatenate([v0_ref[:, hs], v1_ref[:, hs], v2_ref[:, hs], v3_ref[:, hs]], axis=0)
        s = lax.dot_general(q, k, _NT, preferred_element_type=F32)
        s = jnp.where(valid, s - sl_ref[h] * distf, NEG)
        m = jnp.max(s, axis=-1, keepdims=True)
        e = jnp.exp(s - m)
        den = jnp.sum(e, axis=-1, keepdims=True)
        p = (e / den).astype(BF16)
        o_ref[:, hs] = jnp.dot(p, v, preferred_element_type=F32)
        lse_ref[:, h:h + 1] = m + jnp.log(den)


def _band(slopes, q, k, v):
    nb_, length, _ = q.shape
    nq = length // C_TQ
    nkb = length // C_HALF
    q_map = lambda n, i: (n, i, 0)

    def kv_map(j):
        return lambda n, i: (n, jnp.clip(2 * i - 1 + j, 0, nkb - 1), 0)

    qblk = pl.BlockSpec((None, C_TQ, C_W), q_map)
    kblk = lambda j: pl.BlockSpec((None, C_HALF, C_W), kv_map(j))
    return pl.pallas_call(
        functools.partial(_band_kernel, length=length),
        grid=(nb_, nq),
        in_specs=[pl.BlockSpec(memory_space=pltpu.SMEM), qblk,
                  kblk(0), kblk(1), kblk(2), kblk(3), kblk(0), kblk(1), kblk(2), kblk(3)],
        out_specs=(pl.BlockSpec((None, C_TQ, C_W), q_map), pl.BlockSpec((None, C_TQ, C_HEADS), q_map)),
        out_shape=(jax.ShapeDtypeStruct((nb_, length, C_W), F32),
                   jax.ShapeDtypeStruct((nb_, length, C_HEADS), F32)),
        compiler_params=pltpu.CompilerParams(dimension_semantics=("parallel", "parallel"),
                                             vmem_limit_bytes=VMEM_LIMIT),
        name="band",
    )(slopes, q, k, k, k, k, v, v, v, v)


def _fold(a, bn, seq, dil):
    w = a.shape[-1]
    return a.reshape(bn, seq // dil, dil, w).transpose(0, 2, 1, 3).reshape(bn * dil, seq // dil, w)


def _unfold(a, bn, seq, dil):
    w = a.shape[-1]
    return a.reshape(bn, dil, seq // dil, w).transpose(0, 2, 1, 3).reshape(bn * seq, w)


def _rope_tables(seq):
    t = jnp.arange(seq)
    n = HEAD_DIM // 4
    freqs = jnp.power(ROPE_THETA, -jnp.arange(n, dtype=F32) / n)
    ang_r = (t // GRID_W).astype(F32)[:, None] * freqs[None, :]
    ang_c = (t % GRID_W).astype(F32)[:, None] * freqs[None, :]
    zero = jnp.zeros_like(ang_r)
    cos = jnp.concatenate([jnp.cos(ang_r)] * 2 + [jnp.cos(ang_c)] * 2, axis=-1)
    sin_r, sin_c = jnp.sin(ang_r), jnp.sin(ang_c)
    sa = jnp.concatenate([-sin_r, zero, -sin_c, zero], axis=-1)
    sb = jnp.concatenate([zero, sin_r, zero, sin_c], axis=-1)
    two = lambda a: jnp.concatenate([a, a], axis=-1)
    return two(cos), two(sa), two(sb)


def _trunk(x, prm):
    bn, seq, _ = x.shape
    n = bn * seq
    x2 = x.reshape(n, D_MODEL)
    cos, sa, sb = _rope_tables(seq)
    qa, ka, vat, qb, kb, vb = _in_even(x2, seq, prm["g_mix0"], prm["w_in_even"], prm["gq"], prm["gk"],
                                       cos, sa, sb, prm["bd"])
    oa = _mix_a(qa, ka, vat, bn, seq)
    ob = _mix_b(qb, kb, vb, _neighbourhood_bias(prm["rpb"], seq // GRID_W), bn, seq)
    x2 = _post_even(x2, oa, ob, prm["wo_a"], prm["wo_b"], prm["g_ffn0"], prm["wg0"], prm["wu0"], prm["wd0"])

    q, k, v = _in_odd(x2, prm["g_mix1"], prm["w_in_odd"])
    outs, lses = [], []
    for _, dil in C_GROUPS:
        fold = (lambda a: a.reshape(bn, seq, C_W)) if dil == 1 else (lambda a: _fold(a, bn, seq, dil))
        o, lse = _band(prm["slopes"] * dil, fold(q), fold(k), fold(v))
        outs.append(_unfold(o, bn, seq, dil))
        lses.append(_unfold(lse, bn, seq, dil))
    y = _post_odd(x2, outs, lses, prm["expand"], prm["wo_odd"], prm["g_ffn1"], prm["wg1"], prm["wu1"],
                  prm["wd1"], prm["g_final"])
    return y.reshape(bn, seq, D_MODEL)


def kernel(x_prompt, x_sample, norm_mix, norm_ffn, norm_final, w_in_even, a_q_norm, a_k_norm, na_rpb,
           w_out_even, w_in_odd, w_out_odd, w_gate_up, w_down):
    head = np.arange(128) // HEAD_DIM
    prm = {
        "g_mix0": norm_mix[0][None].astype(F32), "g_mix1": norm_mix[1][None].astype(F32),
        "g_ffn0": norm_ffn[0][None].astype(F32), "g_ffn1": norm_ffn[1][None].astype(F32),
        "g_final": norm_final[None].astype(F32),
        "w_in_even": w_in_even[0].astype(BF16), "w_in_odd": w_in_odd[0].astype(BF16),
        "gq": jnp.tile(a_q_norm[0].astype(F32) * SCALE, 2)[None], "gk": jnp.tile(a_k_norm[0].astype(F32), 2)[None],
        "bd": jnp.asarray((head[:, None] == head[None, :]) / HEAD_DIM, BF16),
        "rpb": na_rpb[0],
        "wo_a": w_out_even[0, :A_Q].astype(BF16), "wo_b": w_out_even[0, A_Q:].astype(BF16),
        "wo_odd": w_out_odd[0].astype(BF16),
        "wg0": w_gate_up[0, :, :D_FF].astype(BF16), "wu0": w_gate_up[0, :, D_FF:].astype(BF16),
        "wg1": w_gate_up[1, :, :D_FF].astype(BF16), "wu1": w_gate_up[1, :, D_FF:].astype(BF16),
        "wd0": w_down[0].astype(BF16), "wd1": w_down[1].astype(BF16),
        "slopes": jnp.exp2(-8.0 * (jnp.arange(C_HEADS, dtype=F32) + 1.0) / C_HEADS),
        "expand": jnp.asarray(np.arange(C_HEADS)[:, None] == (np.arange(C_W) // HEAD_DIM)[None, :], BF16),
    }
    return (_trunk(x_prompt, prm), _trunk(x_sample, prm))
```

```python
import functools

import jax
import jax.numpy as jnp
import numpy as np
from jax import lax
from jax.experimental import pallas as pl
from jax.experimental.pallas import tpu as pltpu

F32 = jnp.float32
BF16 = jnp.bfloat16

D_MODEL = 1024
HEAD_DIM = 64
A_Q = 512
A_KV = 128
B_W = 512
C_W = 1024
C_HEADS = 16
B_HEADS = 8
GRID_W = 64
NA_ROWS = 8
NA_COLS = 16
C_GROUPS = ((128, 1), (512, 4), (2048, 16))
ROPE_THETA = 10000.0
D_FF = 2816
FF_CHUNK = 1408
EPS = 1e-6
NEG = -1e30
SCALE = HEAD_DIM ** -0.5
LOG2E = 1.4426950408889634

V_ROWS = HEAD_DIM + 16
ROW_TILE = 512
A_TQ = 256
A_TK = 256
A_UNROLL = 4
A_AHEAD = 3
B_TQ = 256
B_TK = 3 * B_TQ
B_AHEAD = 2
C_TQ = 128
C_HALF = 64
VMEM_LIMIT = 56 * 1024 * 1024

_NT = (((1,), (1,)), ((), ()))


def _resident(shape):
    zeros = (0,) * len(shape)
    return pl.BlockSpec(shape, lambda *_: zeros, pipeline_mode=pl.Buffered(1))


def _rms(x, gain):
    ms = jnp.mean(x * x, axis=-1, keepdims=True)
    return x * lax.rsqrt(ms + EPS) * gain


def _in_even_kernel(x_ref, g_ref, w_ref, gq_ref, gk_ref, cos_ref, sa_ref, sb_ref, bd_ref,
                    qa_ref, ka_ref, vat_ref, qb_ref, kb_ref, vbt_ref):
    hn = _rms(x_ref[...], g_ref[...]).astype(BF16)
    p = jnp.dot(hn, w_ref[...], preferred_element_type=F32)
    cos, sa, sb, bd = cos_ref[...], sa_ref[...], sb_ref[...], bd_ref[...]

    def headnorm_rope(c, gain):
        c2 = c * c
        hi = c2.astype(BF16)
        lo = (c2 - hi.astype(F32)).astype(BF16)
        ms = (jnp.dot(hi, bd, preferred_element_type=F32)
              + jnp.dot(lo, bd, preferred_element_type=F32))
        y = c * lax.rsqrt(ms + EPS) * gain
        return y * cos + pltpu.roll(y, 112, 1) * sa + pltpu.roll(y, 16, 1) * sb

    for j in range(A_Q // 128):
        qa_ref[:, j * 128:(j + 1) * 128] = headnorm_rope(p[:, j * 128:(j + 1) * 128], gq_ref[...]).astype(BF16)
    ka_ref[...] = headnorm_rope(p[:, A_Q:A_Q + A_KV], gk_ref[...]).astype(BF16)
    o = A_Q + 2 * A_KV
    qb_ref[...] = (p[:, o:o + B_W] * (SCALE * LOG2E)).astype(BF16)
    kb_ref[...] = p[:, o + B_W:o + 2 * B_W].astype(BF16)

    def put_transposed(dst_ref, col0, n_pairs):
        ones = jnp.ones((V_ROWS - HEAD_DIM, A_TK), BF16)
        for j in range(p.shape[0] // A_TK):
            for c in range(n_pairs):
                t = p[j * A_TK:(j + 1) * A_TK, col0 + c * 128:col0 + (c + 1) * 128].T.astype(BF16)
                for hh in range(2):
                    r0 = (2 * c + hh) * V_ROWS
                    dst_ref[j, r0:r0 + HEAD_DIM, :] = t[hh * HEAD_DIM:(hh + 1) * HEAD_DIM, :]
                    dst_ref[j, r0 + HEAD_DIM:r0 + V_ROWS, :] = ones

    put_transposed(vat_ref, A_Q + A_KV, A_KV // 128)
    put_transposed(vbt_ref, o + 2 * B_W, B_W // 128)


def _in_even(x2, seq, g, w, gq, gk, cos, sa, sb, bd):
    n = x2.shape[0]
    tm = ROW_TILE
    nt = seq // tm
    row = lambda i: (i, 0)
    pos = lambda i: (i % nt, 0)
    va_rows, vb_rows = (A_KV // HEAD_DIM) * V_ROWS, B_HEADS * V_ROWS
    out_shape = (
        jax.ShapeDtypeStruct((n, A_Q), BF16), jax.ShapeDtypeStruct((n, A_KV), BF16),
        jax.ShapeDtypeStruct((n // A_TK, va_rows, A_TK), BF16),
        jax.ShapeDtypeStruct((n, B_W), BF16), jax.ShapeDtypeStruct((n, B_W), BF16),
        jax.ShapeDtypeStruct((n // A_TK, vb_rows, A_TK), BF16))
    return pl.pallas_call(
        _in_even_kernel,
        grid=(n // tm,),
        in_specs=[pl.BlockSpec((tm, D_MODEL), row), _resident((1, D_MODEL)), _resident(w.shape),
                  _resident((1, 128)), _resident((1, 128)),
                  pl.BlockSpec((tm, 128), pos), pl.BlockSpec((tm, 128), pos), pl.BlockSpec((tm, 128), pos),
                  _resident((128, 128))],
        out_specs=(pl.BlockSpec((tm, A_Q), row), pl.BlockSpec((tm, A_KV), row),
                   pl.BlockSpec((tm // A_TK, va_rows, A_TK), lambda i: (i, 0, 0)),
                   pl.BlockSpec((tm, B_W), row), pl.BlockSpec((tm, B_W), row),
                   pl.BlockSpec((tm // A_TK, vb_rows, A_TK), lambda i: (i, 0, 0))),
        out_shape=out_shape,
        compiler_params=pltpu.CompilerParams(dimension_semantics=("parallel",), vmem_limit_bytes=VMEM_LIMIT),
        name="in_even",
    )(x2, g, w, gq, gk, cos, sa, sb, bd)


def _split_pair(qp):
    low = lax.broadcasted_iota(jnp.int32, qp.shape, 1) < HEAD_DIM
    zero = jnp.zeros(qp.shape, qp.dtype)
    return jnp.concatenate([jnp.where(low, qp, zero), jnp.where(low, zero, qp)], axis=0)


def _finish_pair(acc_lo, acc_hi):
    halves = [a[:HEAD_DIM] / a[HEAD_DIM:HEAD_DIM + 1] for a in (acc_lo, acc_hi)]
    return jnp.concatenate(halves, axis=0).T.astype(BF16)


def _mix_a_kernel(q_ref, k_ref, vt_ref, o_ref, qs_ref, m_ref, acc_ref, *, n_chunks):
    tq = A_TQ
    n_pairs = A_Q // 128
    for c in range(n_pairs):
        qs_ref[2 * c * tq:(2 * c + 2) * tq, :] = _split_pair(q_ref[:, c * 128:(c + 1) * 128])
    m_ref[...] = jnp.full(m_ref.shape, NEG, F32)
    acc_ref[...] = jnp.zeros(acc_ref.shape, F32)

    def scores(ci, c):
        k = k_ref[pl.ds(pl.multiple_of(ci * A_TK, A_TK), A_TK), :]
        return lax.dot_general(k, qs_ref[2 * c * tq:(2 * c + 2) * tq, :], _NT,
                               preferred_element_type=F32)

    def chunks(it, carry):
        items = [(it * A_UNROLL + u, c) for u in range(A_UNROLL) for c in range(n_pairs)]
        ahead = [scores(*item) for item in items[:A_AHEAD]]
        for t, (ci, c) in enumerate(items):
            s = ahead.pop(0)
            if t + A_AHEAD < len(items):
                ahead.append(scores(*items[t + A_AHEAD]))
            cols = slice(2 * c * tq, (2 * c + 2) * tq)
            m_old = m_ref[:, cols]
            m_new = jnp.maximum(m_old, jnp.max(s, axis=0, keepdims=True))
            alpha = jnp.exp2(m_old - m_new)
            p = jnp.exp2((s - m_new).astype(BF16))
            m_ref[:, cols] = m_new
            for g in range(2):
                vt = vt_ref[ci, g * V_ROWS:(g + 1) * V_ROWS, :]
                pv = jnp.dot(vt, p[:, g * tq:(g + 1) * tq], preferred_element_type=F32)
                acc_ref[2 * c + g] = alpha[:, g * tq:(g + 1) * tq] * acc_ref[2 * c + g] + pv
        return carry

    lax.fori_loop(0, n_chunks // A_UNROLL, chunks, 0)

    for c in range(n_pairs):
        o_ref[:, c * 128:(c + 1) * 128] = _finish_pair(acc_ref[2 * c], acc_ref[2 * c + 1])


def _mix_a(qa, ka, vat, bn, seq):
    n = qa.shape[0]
    nq = seq // A_TQ
    nc = seq // A_TK
    return pl.pallas_call(
        functools.partial(_mix_a_kernel, n_chunks=nc),
        grid=(bn, nq),
        in_specs=[pl.BlockSpec((A_TQ, A_Q), lambda b, i: (b * nq + i, 0)),
                  pl.BlockSpec((seq, A_KV), lambda b, i: (b, 0)),
                  pl.BlockSpec((nc, vat.shape[1], A_TK), lambda b, i: (b, 0, 0))],
        out_specs=pl.BlockSpec((A_TQ, A_Q), lambda b, i: (b * nq + i, 0)),
        out_shape=jax.ShapeDtypeStruct((n, A_Q), BF16),
        scratch_shapes=[pltpu.VMEM((8 * A_TQ, 128), BF16), pltpu.VMEM((1, 8 * A_TQ), F32),
                        pltpu.VMEM((8, V_ROWS, A_TQ), F32)],
        compiler_params=pltpu.CompilerParams(dimension_semantics=("parallel", "parallel"),
                                             vmem_limit_bytes=VMEM_LIMIT),
        name="mix_a",
    )(qa, ka, vat)


def _mix_b_kernel(q_ref, k0_ref, k1_ref, k2_ref, vt0_ref, vt1_ref, vt2_ref, bias_ref, o_ref):
    tq = B_TQ
    n_pairs = B_W // 128

    def scores(c):
        cs = slice(c * 128, (c + 1) * 128)
        qs = _split_pair(q_ref[:, cs])
        k = jnp.concatenate([k0_ref[:, cs], k1_ref[:, cs], k2_ref[:, cs]], axis=0)
        return lax.dot_general(k, qs, _NT, preferred_element_type=F32)

    ahead = [scores(c) for c in range(B_AHEAD)]
    for c in range(n_pairs):
        cs = slice(c * 128, (c + 1) * 128)
        s = ahead.pop(0) + bias_ref[0, c]
        if c + B_AHEAD < n_pairs:
            ahead.append(scores(c + B_AHEAD))
        m = jnp.max(s, axis=0, keepdims=True)
        p = jnp.exp2((s - m).astype(BF16))
        accs = []
        for hh in range(2):
            rows = slice((2 * c + hh) * V_ROWS, (2 * c + hh + 1) * V_ROWS)
            vt = jnp.concatenate([vt0_ref[rows, :], vt1_ref[rows, :], vt2_ref[rows, :]], axis=1)
            accs.append(jnp.dot(vt, p[:, hh * tq:(hh + 1) * tq], preferred_element_type=F32))
        o_ref[:, cs] = _finish_pair(*accs)


def _mix_b(qb, kb, vbt, bias, bn, seq):
    n = qb.shape[0]
    nb = seq // B_TQ
    q_map = lambda b, i: (b * nb + i, 0)

    def kv_map(j):
        return lambda b, i: (b * nb + jnp.clip(i - 1, 0, nb - 3) + j, 0)

    def vt_map(j):
        return lambda b, i: (b * nb + jnp.clip(i - 1, 0, nb - 3) + j, 0, 0)

    def bias_map(b, i):
        return (jnp.where(i == 0, 0, jnp.where(i == nb - 1, 2, 1)), 0, 0, 0)

    blk = lambda m: pl.BlockSpec((B_TQ, B_W), m)
    vblk = lambda m: pl.BlockSpec((None, vbt.shape[1], B_TQ), m)
    return pl.pallas_call(
        _mix_b_kernel,
        grid=(bn, nb),
        in_specs=[blk(q_map), blk(kv_map(0)), blk(kv_map(1)), blk(kv_map(2)),
                  vblk(vt_map(0)), vblk(vt_map(1)), vblk(vt_map(2)),
                  pl.BlockSpec((1, B_W // 128, B_TK, 2 * B_TQ), bias_map)],
        out_specs=blk(q_map),
        out_shape=jax.ShapeDtypeStruct((n, B_W), BF16),
        compiler_params=pltpu.CompilerParams(dimension_semantics=("parallel", "parallel"),
                                             vmem_limit_bytes=VMEM_LIMIT),
        name="mix_b",
    )(qb, kb, kb, kb, vbt, vbt, vbt, bias)


def _neighbourhood_bias(rpb, rows):
    n_dc = 2 * NA_COLS - 1
    c = np.arange(GRID_W)
    cs = np.clip(c - NA_COLS // 2, 0, GRID_W - NA_COLS)
    col_ok = (c[:, None] >= cs[None, :]) & (c[:, None] < cs[None, :] + NA_COLS)
    dc = np.clip(c[:, None] - c[None, :] + NA_COLS - 1, 0, n_dc - 1)
    pick = (dc.reshape(-1)[None, :] == np.arange(n_dc)[:, None]).astype(np.float32)
    by_col = jnp.einsum("hrd,dx->hrx", rpb.astype(F32) * LOG2E, jnp.asarray(pick),
                        precision=lax.Precision.HIGHEST)
    by_col = by_col.reshape(B_HEADS, 2 * NA_ROWS - 1, GRID_W, GRID_W)
    by_col = jnp.where(col_ok[None, None], by_col, NEG)
    masked = jnp.full((B_HEADS, GRID_W, GRID_W), NEG, F32)
    tables = []
    for r0, ws in ((0, 0), (4, 0), (rows - 4, rows - 12)):
        key_rows = []
        for kr in range(B_TK // GRID_W):
            blocks = []
            for rr in range(B_TQ // GRID_W):
                r = r0 + rr
                rs = min(max(r - NA_ROWS // 2, 0), rows - NA_ROWS)
                ok = rs <= ws + kr < rs + NA_ROWS
                blocks.append(by_col[:, ws + kr - r + NA_ROWS - 1] if ok else masked)
            key_rows.append(jnp.concatenate(blocks, axis=-1))
        t = jnp.concatenate(key_rows, axis=-2)
        t = t.reshape(B_HEADS // 2, 2, B_TK, B_TQ).transpose(0, 2, 1, 3).reshape(B_HEADS // 2, B_TK, 2 * B_TQ)
        tables.append(t)
    return jnp.stack(tables, axis=0)


def _ffn(x1, gain, wg_ref, wu_ref, wd_ref):
    hn = _rms(x1, gain).astype(BF16)
    acc = None
    for c in range(D_FF // FF_CHUNK):
        cs = slice(c * FF_CHUNK, (c + 1) * FF_CHUNK)
        gate = jnp.dot(hn, wg_ref[:, cs], preferred_element_type=F32)
        up = jnp.dot(hn, wu_ref[:, cs], preferred_element_type=F32)
        act = (gate * (1.0 / (1.0 + jnp.exp(-gate))) * up).astype(BF16)
        d = jnp.dot(act, wd_ref[cs, :], preferred_element_type=F32)
        acc = d if acc is None else acc + d
    return x1 + acc


def _post_even_kernel(x_ref, oa_ref, ob_ref, woa_ref, wob_ref, g_ref, wg_ref, wu_ref, wd_ref, y_ref):
    mix = (jnp.dot(oa_ref[...], woa_ref[...], preferred_element_type=F32)
           + jnp.dot(ob_ref[...], wob_ref[...], preferred_element_type=F32))
    y_ref[...] = _ffn(x_ref[...] + mix, g_ref[...], wg_ref, wu_ref, wd_ref)


def _post_even(x2, oa, ob, woa, wob, g, wg, wu, wd):
    n = x2.shape[0]
    tm = ROW_TILE
    row = lambda i: (i, 0)
    return pl.pallas_call(
        _post_even_kernel,
        grid=(n // tm,),
        in_specs=[pl.BlockSpec((tm, D_MODEL), row), pl.BlockSpec((tm, A_Q), row), pl.BlockSpec((tm, B_W), row),
                  _resident(woa.shape), _resident(wob.shape), _resident((1, D_MODEL)),
                  _resident(wg.shape), _resident(wu.shape), _resident(wd.shape)],
        out_specs=pl.BlockSpec((tm, D_MODEL), row),
        out_shape=jax.ShapeDtypeStruct((n, D_MODEL), F32),
        compiler_params=pltpu.CompilerParams(dimension_semantics=("parallel",), vmem_limit_bytes=VMEM_LIMIT),
        name="post_even",
    )(x2, oa, ob, woa, wob, g, wg, wu, wd)


def _post_odd_kernel(x_ref, o1_ref, o2_ref, o3_ref, l1_ref, l2_ref, l3_ref, ex_ref, wo_ref, g_ref,
                     wg_ref, wu_ref, wd_ref, gf_ref, y_ref):
    l1, l2, l3 = l1_ref[...], l2_ref[...], l3_ref[...]
    mx = jnp.maximum(jnp.maximum(l1, l2), l3)
    e1, e2, e3 = jnp.exp(l1 - mx), jnp.exp(l2 - mx), jnp.exp(l3 - mx)
    den = e1 + e2 + e3
    ex = ex_ref[...]

    def widen(w):
        hi = w.astype(BF16)
        lo = (w - hi.astype(F32)).astype(BF16)
        return jnp.dot(hi, ex, preferred_element_type=F32) + jnp.dot(lo, ex, preferred_element_type=F32)

    o = widen(e1 / den) * o1_ref[...] + widen(e2 / den) * o2_ref[...] + widen(e3 / den) * o3_ref[...]
    mix = jnp.dot(o.astype(BF16), wo_ref[...], preferred_element_type=F32)
    y = _ffn(x_ref[...] + mix, g_ref[...], wg_ref, wu_ref, wd_ref)
    y_ref[...] = _rms(y, gf_ref[...])


def _post_odd(x2, os_, ls_, ex, wo, g, wg, wu, wd, gf):
    n = x2.shape[0]
    tm = ROW_TILE
    row = lambda i: (i, 0)
    wide = pl.BlockSpec((tm, D_MODEL), row)
    narrow = pl.BlockSpec((tm, C_HEADS), row)
    return pl.pallas_call(
        _post_odd_kernel,
        grid=(n // tm,),
        in_specs=[wide, wide, wide, wide, narrow, narrow, narrow, _resident(ex.shape), _resident(wo.shape),
                  _resident((1, D_MODEL)), _resident(wg.shape), _resident(wu.shape), _resident(wd.shape),
                  _resident((1, D_MODEL))],
        out_specs=wide,
        out_shape=jax.ShapeDtypeStruct((n, D_MODEL), F32),
        compiler_params=pltpu.CompilerParams(dimension_semantics=("parallel",), vmem_limit_bytes=VMEM_LIMIT),
        name="post_odd",
    )(x2, *os_, *ls_, ex, wo, g, wg, wu, wd, gf)


def _in_odd_kernel(x_ref, g_ref, w_ref, q_ref, k_ref, v_ref):
    hn = _rms(x_ref[...], g_ref[...]).astype(BF16)
    p = jnp.dot(hn, w_ref[...], preferred_element_type=F32)
    q_ref[...] = (p[:, :C_W] * SCALE).astype(BF16)
    k_ref[...] = p[:, C_W:2 * C_W].astype(BF16)
    v_ref[...] = p[:, 2 * C_W:].astype(BF16)


def _in_odd(x2, g, w):
    n = x2.shape[0]
    tm = ROW_TILE
    row = lambda i: (i, 0)
    wide = pl.BlockSpec((tm, C_W), row)
    shape = jax.ShapeDtypeStruct((n, C_W), BF16)
    return pl.pallas_call(
        _in_odd_kernel,
        grid=(n // tm,),
        in_specs=[pl.BlockSpec((tm, D_MODEL), row), _resident((1, D_MODEL)), _resident(w.shape)],
        out_specs=(wide, wide, wide),
        out_shape=(shape, shape, shape),
        compiler_params=pltpu.CompilerParams(dimension_semantics=("parallel",), vmem_limit_bytes=VMEM_LIMIT),
        name="in_odd",
    )(x2, g, w)


def _band_kernel(sl_ref, q_ref, k0_ref, k1_ref, k2_ref, k3_ref, v0_ref, v1_ref, v2_ref, v3_ref,
                 o_ref, lse_ref, *, length):
    i = pl.program_id(1)
    tk = C_TQ + 2 * C_HALF
    row = lax.broadcasted_iota(jnp.int32, (C_TQ, tk), 0)
    col = lax.broadcasted_iota(jnp.int32, (C_TQ, tk), 1)
    dist = jnp.abs(col - C_HALF - row)
    kj = i * C_TQ - C_HALF + col
    valid = (kj >= 0) & (kj < length) & (dist <= C_HALF)
    distf = dist.astype(F32)
    for h in range(C_HEADS):
        hs = slice(h * HEAD_DIM, (h + 1) * HEAD_DIM)
        q = q_ref[:, hs]
        k = jnp.concatenate([k0_ref[:, hs], k1_ref[:, hs], k2_ref[:, hs], k3_ref[:, hs]], axis=0)
        v = jnp.concatenate([v0_ref[:, hs], v1_ref[:, hs], v2_ref[:, hs], v3_ref[:, hs]], axis=0)
        s = lax.dot_general(q, k, _NT, preferred_element_type=F32)
        s = jnp.where(valid, s - sl_ref[h] * distf, NEG)
        m = jnp.max(s, axis=-1, keepdims=True)
        e = jnp.exp(s - m)
        den = jnp.sum(e, axis=-1, keepdims=True)
        p = (e / den).astype(BF16)
        o_ref[:, hs] = jnp.dot(p, v, preferred_element_type=F32)
        lse_ref[:, h:h + 1] = m + jnp.log(den)


def _band(slopes, q, k, v):
    nb_, length, _ = q.shape
    nq = length // C_TQ
    nkb = length // C_HALF
    q_map = lambda n, i: (n, i, 0)

    def kv_map(j):
        return lambda n, i: (n, jnp.clip(2 * i - 1 + j, 0, nkb - 1), 0)

    qblk = pl.BlockSpec((None, C_TQ, C_W), q_map)
    kblk = lambda j: pl.BlockSpec((None, C_HALF, C_W), kv_map(j))
    return pl.pallas_call(
        functools.partial(_band_kernel, length=length),
        grid=(nb_, nq),
        in_specs=[pl.BlockSpec(memory_space=pltpu.SMEM), qblk,
                  kblk(0), kblk(1), kblk(2), kblk(3), kblk(0), kblk(1), kblk(2), kblk(3)],
        out_specs=(pl.BlockSpec((None, C_TQ, C_W), q_map), pl.BlockSpec((None, C_TQ, C_HEADS), q_map)),
        out_shape=(jax.ShapeDtypeStruct((nb_, length, C_W), F32),
                   jax.ShapeDtypeStruct((nb_, length, C_HEADS), F32)),
        compiler_params=pltpu.CompilerParams(dimension_semantics=("parallel", "parallel"),
                                             vmem_limit_bytes=VMEM_LIMIT),
        name="band",
    )(slopes, q, k, k, k, k, v, v, v, v)


def _fold(a, bn, seq, dil):
    w = a.shape[-1]
    return a.reshape(bn, seq // dil, dil, w).transpose(0, 2, 1, 3).reshape(bn * dil, seq // dil, w)


def _unfold(a, bn, seq, dil):
    w = a.shape[-1]
    return a.reshape(bn, dil, seq // dil, w).transpose(0, 2, 1, 3).reshape(bn * seq, w)


def _rope_tables(seq):
    t = jnp.arange(seq)
    n = HEAD_DIM // 4
    freqs = jnp.power(ROPE_THETA, -jnp.arange(n, dtype=F32) / n)
    ang_r = (t // GRID_W).astype(F32)[:, None] * freqs[None, :]
    ang_c = (t % GRID_W).astype(F32)[:, None] * freqs[None, :]
    zero = jnp.zeros_like(ang_r)
    cos = jnp.concatenate([jnp.cos(ang_r)] * 2 + [jnp.cos(ang_c)] * 2, axis=-1)
    sin_r, sin_c = jnp.sin(ang_r), jnp.sin(ang_c)
    sa = jnp.concatenate([-sin_r, zero, -sin_c, zero], axis=-1)
    sb = jnp.concatenate([zero, sin_r, zero, sin_c], axis=-1)
    two = lambda a: jnp.concatenate([a, a], axis=-1)
    return two(cos), two(sa), two(sb)


def _trunk(x, prm):
    bn, seq, _ = x.shape
    n = bn * seq
    x2 = x.reshape(n, D_MODEL)
    cos, sa, sb = _rope_tables(seq)
    qa, ka, vat, qb, kb, vbt = _in_even(x2, seq, prm["g_mix0"], prm["w_in_even"], prm["gq"], prm["gk"],
                                        cos, sa, sb, prm["bd"])
    oa = _mix_a(qa, ka, vat, bn, seq)
    ob = _mix_b(qb, kb, vbt, _neighbourhood_bias(prm["rpb"], seq // GRID_W), bn, seq)
    x2 = _post_even(x2, oa, ob, prm["wo_a"], prm["wo_b"], prm["g_ffn0"], prm["wg0"], prm["wu0"], prm["wd0"])

    q, k, v = _in_odd(x2, prm["g_mix1"], prm["w_in_odd"])
    outs, lses = [], []
    for _, dil in C_GROUPS:
        fold = (lambda a: a.reshape(bn, seq, C_W)) if dil == 1 else (lambda a: _fold(a, bn, seq, dil))
        o, lse = _band(prm["slopes"] * dil, fold(q), fold(k), fold(v))
        outs.append(_unfold(o, bn, seq, dil))
        lses.append(_unfold(lse, bn, seq, dil))
    y = _post_odd(x2, outs, lses, prm["expand"], prm["wo_odd"], prm["g_ffn1"], prm["wg1"], prm["wu1"],
                  prm["wd1"], prm["g_final"])
    return y.reshape(bn, seq, D_MODEL)


def kernel(x_prompt, x_sample, norm_mix, norm_ffn, norm_final, w_in_even, a_q_norm, a_k_norm, na_rpb,
           w_out_even, w_in_odd, w_out_odd, w_gate_up, w_down):
    head = np.arange(128) // HEAD_DIM
    order = np.array([0, 4, 1, 5, 2, 6, 3, 7])
    w_in0 = w_in_even[0]
    w_qa = w_in0[:, :A_Q].reshape(D_MODEL, A_Q // HEAD_DIM, HEAD_DIM)[:, order].reshape(D_MODEL, A_Q)
    w_in0 = jnp.concatenate([w_qa, w_in0[:, A_Q:]], axis=1)
    wo_a = w_out_even[0, :A_Q].reshape(A_Q // HEAD_DIM, HEAD_DIM, D_MODEL)[order].reshape(A_Q, D_MODEL)
    prm = {
        "g_mix0": norm_mix[0][None].astype(F32), "g_mix1": norm_mix[1][None].astype(F32),
        "g_ffn0": norm_ffn[0][None].astype(F32), "g_ffn1": norm_ffn[1][None].astype(F32),
        "g_final": norm_final[None].astype(F32),
        "w_in_even": w_in0.astype(BF16), "w_in_odd": w_in_odd[0].astype(BF16),
        "gq": jnp.tile(a_q_norm[0].astype(F32) * (SCALE * LOG2E), 2)[None], "gk": jnp.tile(a_k_norm[0].astype(F32), 2)[None],
        "bd": jnp.asarray((head[:, None] == head[None, :]) / HEAD_DIM, BF16),
        "rpb": na_rpb[0],
        "wo_a": wo_a.astype(BF16), "wo_b": w_out_even[0, A_Q:].astype(BF16),
        "wo_odd": w_out_odd[0].astype(BF16),
        "wg0": w_gate_up[0, :, :D_FF].astype(BF16), "wu0": w_gate_up[0, :, D_FF:].astype(BF16),
        "wg1": w_gate_up[1, :, :D_FF].astype(BF16), "wu1": w_gate_up[1, :, D_FF:].astype(BF16),
        "wd0": w_down[0].astype(BF16), "wd1": w_down[1].astype(BF16),
        "slopes": jnp.exp2(-8.0 * (jnp.arange(C_HEADS, dtype=F32) + 1.0) / C_HEADS),
        "expand": jnp.asarray(np.arange(C_HEADS)[:, None] == (np.arange(C_W) // HEAD_DIM)[None, :], BF16),
    }
    return (_trunk(x_prompt, prm), _trunk(x_sample, prm))
```

```python
import functools

import jax
import jax.numpy as jnp
import numpy as np
from jax import lax
from jax.experimental import pallas as pl
from jax.experimental.pallas import tpu as pltpu

F32 = jnp.float32
BF16 = jnp.bfloat16

D_MODEL = 1024
HEAD_DIM = 64
A_Q = 512
A_KV = 128
B_W = 512
C_W = 1024
C_HEADS = 16
B_HEADS = 8
GRID_W = 64
NA_ROWS = 8
NA_COLS = 16
C_GROUPS = ((128, 1), (512, 4), (2048, 16))
ROPE_THETA = 10000.0
D_FF = 2816
FF_CHUNK = 1408
EPS = 1e-6
NEG = -1e30
SCALE = HEAD_DIM ** -0.5
LOG2E = 1.4426950408889634

V_ROWS = HEAD_DIM + 16
ROW_TILE = 512
A_TQ = 256
A_TK = 256
A_UNROLL = 4
A_AHEAD = 3
B_TQ = 256
B_TK = 3 * B_TQ
B_AHEAD = 2
C_TQ = 128
C_HALF = 64
C_AHEAD = 3
VMEM_LIMIT = 56 * 1024 * 1024

_NT = (((1,), (1,)), ((), ()))


def _resident(shape):
    zeros = (0,) * len(shape)
    return pl.BlockSpec(shape, lambda *_: zeros, pipeline_mode=pl.Buffered(1))


def _rms(x, gain):
    ms = jnp.mean(x * x, axis=-1, keepdims=True)
    return x * lax.rsqrt(ms + EPS) * gain


def _in_even_kernel(x_ref, g_ref, w_ref, gq_ref, gk_ref, cos_ref, sa_ref, sb_ref, bd_ref,
                    qa_ref, ka_ref, vat_ref, qb_ref, kb_ref, vbt_ref):
    hn = _rms(x_ref[...], g_ref[...]).astype(BF16)
    p = jnp.dot(hn, w_ref[...], preferred_element_type=F32)
    cos, sa, sb, bd = cos_ref[...], sa_ref[...], sb_ref[...], bd_ref[...]

    def headnorm_rope(c, gain):
        c2 = c * c
        hi = c2.astype(BF16)
        lo = (c2 - hi.astype(F32)).astype(BF16)
        ms = (jnp.dot(hi, bd, preferred_element_type=F32)
              + jnp.dot(lo, bd, preferred_element_type=F32))
        y = c * lax.rsqrt(ms + EPS) * gain
        return y * cos + pltpu.roll(y, 112, 1) * sa + pltpu.roll(y, 16, 1) * sb

    for j in range(A_Q // 128):
        qa_ref[:, j * 128:(j + 1) * 128] = headnorm_rope(p[:, j * 128:(j + 1) * 128], gq_ref[...]).astype(BF16)
    ka_ref[...] = headnorm_rope(p[:, A_Q:A_Q + A_KV], gk_ref[...]).astype(BF16)
    o = A_Q + 2 * A_KV
    qb_ref[...] = (p[:, o:o + B_W] * (SCALE * LOG2E)).astype(BF16)
    kb_ref[...] = p[:, o + B_W:o + 2 * B_W].astype(BF16)

    def put_transposed(dst_ref, col0, n_pairs):
        ones = jnp.ones((V_ROWS - HEAD_DIM, A_TK), BF16)
        for j in range(p.shape[0] // A_TK):
            for c in range(n_pairs):
                t = p[j * A_TK:(j + 1) * A_TK, col0 + c * 128:col0 + (c + 1) * 128].T.astype(BF16)
                for hh in range(2):
                    r0 = (2 * c + hh) * V_ROWS
                    dst_ref[j, r0:r0 + HEAD_DIM, :] = t[hh * HEAD_DIM:(hh + 1) * HEAD_DIM, :]
                    dst_ref[j, r0 + HEAD_DIM:r0 + V_ROWS, :] = ones

    put_transposed(vat_ref, A_Q + A_KV, A_KV // 128)
    put_transposed(vbt_ref, o + 2 * B_W, B_W // 128)


def _in_even(x2, seq, g, w, gq, gk, cos, sa, sb, bd):
    n = x2.shape[0]
    tm = ROW_TILE
    nt = seq // tm
    row = lambda i: (i, 0)
    pos = lambda i: (i % nt, 0)
    va_rows, vb_rows = (A_KV // HEAD_DIM) * V_ROWS, B_HEADS * V_ROWS
    out_shape = (
        jax.ShapeDtypeStruct((n, A_Q), BF16), jax.ShapeDtypeStruct((n, A_KV), BF16),
        jax.ShapeDtypeStruct((n // A_TK, va_rows, A_TK), BF16),
        jax.ShapeDtypeStruct((n, B_W), BF16), jax.ShapeDtypeStruct((n, B_W), BF16),
        jax.ShapeDtypeStruct((n // A_TK, vb_rows, A_TK), BF16))
    return pl.pallas_call(
        _in_even_kernel,
        grid=(n // tm,),
        in_specs=[pl.BlockSpec((tm, D_MODEL), row), _resident((1, D_MODEL)), _resident(w.shape),
                  _resident((1, 128)), _resident((1, 128)),
                  pl.BlockSpec((tm, 128), pos), pl.BlockSpec((tm, 128), pos), pl.BlockSpec((tm, 128), pos),
                  _resident((128, 128))],
        out_specs=(pl.BlockSpec((tm, A_Q), row), pl.BlockSpec((tm, A_KV), row),
                   pl.BlockSpec((tm // A_TK, va_rows, A_TK), lambda i: (i, 0, 0)),
                   pl.BlockSpec((tm, B_W), row), pl.BlockSpec((tm, B_W), row),
                   pl.BlockSpec((tm // A_TK, vb_rows, A_TK), lambda i: (i, 0, 0))),
        out_shape=out_shape,
        compiler_params=pltpu.CompilerParams(dimension_semantics=("parallel",), vmem_limit_bytes=VMEM_LIMIT),
        name="in_even",
    )(x2, g, w, gq, gk, cos, sa, sb, bd)


def _split_pair(qp):
    low = lax.broadcasted_iota(jnp.int32, qp.shape, 1) < HEAD_DIM
    zero = jnp.zeros(qp.shape, qp.dtype)
    return jnp.concatenate([jnp.where(low, qp, zero), jnp.where(low, zero, qp)], axis=0)


def _finish_pair(acc_lo, acc_hi):
    halves = [a[:HEAD_DIM] / a[HEAD_DIM:HEAD_DIM + 1] for a in (acc_lo, acc_hi)]
    return jnp.concatenate(halves, axis=0).T.astype(BF16)


def _mix_a_kernel(q_ref, k_ref, vt_ref, o_ref, qs_ref, m_ref, acc_ref, *, n_chunks):
    tq = A_TQ
    n_pairs = A_Q // 128
    for c in range(n_pairs):
        qs_ref[2 * c * tq:(2 * c + 2) * tq, :] = _split_pair(q_ref[:, c * 128:(c + 1) * 128])
    m_ref[...] = jnp.full(m_ref.shape, NEG, F32)
    acc_ref[...] = jnp.zeros(acc_ref.shape, F32)

    def scores(ci, c):
        k = k_ref[pl.ds(pl.multiple_of(ci * A_TK, A_TK), A_TK), :]
        return lax.dot_general(k, qs_ref[2 * c * tq:(2 * c + 2) * tq, :], _NT,
                               preferred_element_type=F32)

    def chunks(it, carry):
        items = [(it * A_UNROLL + u, c) for u in range(A_UNROLL) for c in range(n_pairs)]
        ahead = [scores(*item) for item in items[:A_AHEAD]]
        for t, (ci, c) in enumerate(items):
            s = ahead.pop(0)
            if t + A_AHEAD < len(items):
                ahead.append(scores(*items[t + A_AHEAD]))
            cols = slice(2 * c * tq, (2 * c + 2) * tq)
            m_old = m_ref[:, cols]
            m_new = jnp.maximum(m_old, jnp.max(s, axis=0, keepdims=True))
            alpha = jnp.exp2(m_old - m_new)
            p = jnp.exp2((s - m_new).astype(BF16))
            m_ref[:, cols] = m_new
            for g in range(2):
                vt = vt_ref[ci, g * V_ROWS:(g + 1) * V_ROWS, :]
                pv = jnp.dot(vt, p[:, g * tq:(g + 1) * tq], preferred_element_type=F32)
                acc_ref[2 * c + g] = alpha[:, g * tq:(g + 1) * tq] * acc_ref[2 * c + g] + pv
        return carry

    lax.fori_loop(0, n_chunks // A_UNROLL, chunks, 0)

    for c in range(n_pairs):
        o_ref[:, c * 128:(c + 1) * 128] = _finish_pair(acc_ref[2 * c], acc_ref[2 * c + 1])


def _mix_a(qa, ka, vat, bn, seq):
    n = qa.shape[0]
    nq = seq // A_TQ
    nc = seq // A_TK
    return pl.pallas_call(
        functools.partial(_mix_a_kernel, n_chunks=nc),
        grid=(bn, nq),
        in_specs=[pl.BlockSpec((A_TQ, A_Q), lambda b, i: (b * nq + i, 0)),
                  pl.BlockSpec((seq, A_KV), lambda b, i: (b, 0)),
                  pl.BlockSpec((nc, vat.shape[1], A_TK), lambda b, i: (b, 0, 0))],
        out_specs=pl.BlockSpec((A_TQ, A_Q), lambda b, i: (b * nq + i, 0)),
        out_shape=jax.ShapeDtypeStruct((n, A_Q), BF16),
        scratch_shapes=[pltpu.VMEM((8 * A_TQ, 128), BF16), pltpu.VMEM((1, 8 * A_TQ), F32),
                        pltpu.VMEM((8, V_ROWS, A_TQ), F32)],
        compiler_params=pltpu.CompilerParams(dimension_semantics=("parallel", "parallel"),
                                             vmem_limit_bytes=VMEM_LIMIT),
        name="mix_a",
    )(qa, ka, vat)


def _mix_b_kernel(q_ref, k0_ref, k1_ref, k2_ref, vt0_ref, vt1_ref, vt2_ref, bias_ref, o_ref):
    tq = B_TQ
    n_pairs = B_W // 128

    k_refs = (k0_ref, k1_ref, k2_ref)
    vt_refs = (vt0_ref, vt1_ref, vt2_ref)

    def scores(c):
        cs = slice(c * 128, (c + 1) * 128)
        qs = _split_pair(q_ref[:, cs])
        return [lax.dot_general(k_ref[:, cs], qs, _NT, preferred_element_type=F32) for k_ref in k_refs]

    ahead = [scores(c) for c in range(B_AHEAD)]
    for c in range(n_pairs):
        cs = slice(c * 128, (c + 1) * 128)
        s = [sj + bias_ref[0, c, j * tq:(j + 1) * tq, :] for j, sj in enumerate(ahead.pop(0))]
        if c + B_AHEAD < n_pairs:
            ahead.append(scores(c + B_AHEAD))
        m = jnp.max(jnp.maximum(jnp.maximum(s[0], s[1]), s[2]), axis=0, keepdims=True)
        p = [jnp.exp2((sj - m).astype(BF16)) for sj in s]
        accs = []
        for hh in range(2):
            rows = slice((2 * c + hh) * V_ROWS, (2 * c + hh + 1) * V_ROWS)
            parts = [jnp.dot(vt_ref[rows, :], pj[:, hh * tq:(hh + 1) * tq], preferred_element_type=F32)
                     for vt_ref, pj in zip(vt_refs, p)]
            accs.append(parts[0] + parts[1] + parts[2])
        o_ref[:, cs] = _finish_pair(*accs)


def _mix_b(qb, kb, vbt, bias, bn, seq):
    n = qb.shape[0]
    nb = seq // B_TQ
    q_map = lambda b, i: (b * nb + i, 0)

    def kv_map(j):
        return lambda b, i: (b * nb + jnp.clip(i - 1, 0, nb - 3) + j, 0)

    def vt_map(j):
        return lambda b, i: (b * nb + jnp.clip(i - 1, 0, nb - 3) + j, 0, 0)

    def bias_map(b, i):
        return (jnp.where(i == 0, 0, jnp.where(i == nb - 1, 2, 1)), 0, 0, 0)

    blk = lambda m: pl.BlockSpec((B_TQ, B_W), m)
    vblk = lambda m: pl.BlockSpec((None, vbt.shape[1], B_TQ), m)
    return pl.pallas_call(
        _mix_b_kernel,
        grid=(bn, nb),
        in_specs=[blk(q_map), blk(kv_map(0)), blk(kv_map(1)), blk(kv_map(2)),
                  vblk(vt_map(0)), vblk(vt_map(1)), vblk(vt_map(2)),
                  pl.BlockSpec((1, B_W // 128, B_TK, 2 * B_TQ), bias_map)],
        out_specs=blk(q_map),
        out_shape=jax.ShapeDtypeStruct((n, B_W), BF16),
        compiler_params=pltpu.CompilerParams(dimension_semantics=("parallel", "parallel"),
                                             vmem_limit_bytes=VMEM_LIMIT),
        name="mix_b",
    )(qb, kb, kb, kb, vbt, vbt, vbt, bias)


def _neighbourhood_bias(rpb, rows):
    n_dc = 2 * NA_COLS - 1
    c = np.arange(GRID_W)
    cs = np.clip(c - NA_COLS // 2, 0, GRID_W - NA_COLS)
    col_ok = (c[:, None] >= cs[None, :]) & (c[:, None] < cs[None, :] + NA_COLS)
    dc = np.clip(c[:, None] - c[None, :] + NA_COLS - 1, 0, n_dc - 1)
    pick = (dc.reshape(-1)[None, :] == np.arange(n_dc)[:, None]).astype(np.float32)
    by_col = jnp.einsum("hrd,dx->hrx", rpb.astype(F32) * LOG2E, jnp.asarray(pick),
                        precision=lax.Precision.HIGHEST)
    by_col = by_col.reshape(B_HEADS, 2 * NA_ROWS - 1, GRID_W, GRID_W)
    by_col = jnp.where(col_ok[None, None], by_col, NEG)
    masked = jnp.full((B_HEADS, GRID_W, GRID_W), NEG, F32)
    tables = []
    for r0, ws in ((0, 0), (4, 0), (rows - 4, rows - 12)):
        key_rows = []
        for kr in range(B_TK // GRID_W):
            blocks = []
            for rr in range(B_TQ // GRID_W):
                r = r0 + rr
                rs = min(max(r - NA_ROWS // 2, 0), rows - NA_ROWS)
                ok = rs <= ws + kr < rs + NA_ROWS
                blocks.append(by_col[:, ws + kr - r + NA_ROWS - 1] if ok else masked)
            key_rows.append(jnp.concatenate(blocks, axis=-1))
        t = jnp.concatenate(key_rows, axis=-2)
        t = t.reshape(B_HEADS // 2, 2, B_TK, B_TQ).transpose(0, 2, 1, 3).reshape(B_HEADS // 2, B_TK, 2 * B_TQ)
        tables.append(t)
    return jnp.stack(tables, axis=0)


def _ffn(x1, gain, wg_ref, wu_ref, wd_ref):
    hn = _rms(x1, gain).astype(BF16)
    acc = None
    for c in range(D_FF // FF_CHUNK):
        cs = slice(c * FF_CHUNK, (c + 1) * FF_CHUNK)
        gate = jnp.dot(hn, wg_ref[:, cs], preferred_element_type=F32)
        up = jnp.dot(hn, wu_ref[:, cs], preferred_element_type=F32)
        act = (gate * (1.0 / (1.0 + jnp.exp(-gate))) * up).astype(BF16)
        d = jnp.dot(act, wd_ref[cs, :], preferred_element_type=F32)
        acc = d if acc is None else acc + d
    return x1 + acc


def _post_even_kernel(x_ref, oa_ref, ob_ref, woa_ref, wob_ref, g_ref, wg_ref, wu_ref, wd_ref, y_ref):
    mix = (jnp.dot(oa_ref[...], woa_ref[...], preferred_element_type=F32)
           + jnp.dot(ob_ref[...], wob_ref[...], preferred_element_type=F32))
    y_ref[...] = _ffn(x_ref[...] + mix, g_ref[...], wg_ref, wu_ref, wd_ref)


def _post_even(x2, oa, ob, woa, wob, g, wg, wu, wd):
    n = x2.shape[0]
    tm = ROW_TILE
    row = lambda i: (i, 0)
    return pl.pallas_call(
        _post_even_kernel,
        grid=(n // tm,),
        in_specs=[pl.BlockSpec((tm, D_MODEL), row), pl.BlockSpec((tm, A_Q), row), pl.BlockSpec((tm, B_W), row),
                  _resident(woa.shape), _resident(wob.shape), _resident((1, D_MODEL)),
                  _resident(wg.shape), _resident(wu.shape), _resident(wd.shape)],
        out_specs=pl.BlockSpec((tm, D_MODEL), row),
        out_shape=jax.ShapeDtypeStruct((n, D_MODEL), F32),
        compiler_params=pltpu.CompilerParams(dimension_semantics=("parallel",), vmem_limit_bytes=VMEM_LIMIT),
        name="post_even",
    )(x2, oa, ob, woa, wob, g, wg, wu, wd)


def _post_odd_kernel(x_ref, o1_ref, o2_ref, o3_ref, l1_ref, l2_ref, l3_ref, ex_ref, wo_ref, g_ref,
                     wg_ref, wu_ref, wd_ref, gf_ref, y_ref):
    l1, l2, l3 = l1_ref[...], l2_ref[...], l3_ref[...]
    mx = jnp.maximum(jnp.maximum(l1, l2), l3)
    e1, e2, e3 = jnp.exp2(l1 - mx), jnp.exp2(l2 - mx), jnp.exp2(l3 - mx)
    den = e1 + e2 + e3
    ex = ex_ref[...]

    def widen(w):
        hi = w.astype(BF16)
        lo = (w - hi.astype(F32)).astype(BF16)
        return jnp.dot(hi, ex, preferred_element_type=F32) + jnp.dot(lo, ex, preferred_element_type=F32)

    o = (widen(e1 / den) * o1_ref[...].astype(F32) + widen(e2 / den) * o2_ref[...].astype(F32)
         + widen(e3 / den) * o3_ref[...].astype(F32))
    mix = jnp.dot(o.astype(BF16), wo_ref[...], preferred_element_type=F32)
    y = _ffn(x_ref[...] + mix, g_ref[...], wg_ref, wu_ref, wd_ref)
    y_ref[...] = _rms(y, gf_ref[...])


def _post_odd(x2, os_, ls_, ex, wo, g, wg, wu, wd, gf):
    n = x2.shape[0]
    tm = ROW_TILE
    row = lambda i: (i, 0)
    wide = pl.BlockSpec((tm, D_MODEL), row)
    narrow = pl.BlockSpec((tm, C_HEADS), row)
    return pl.pallas_call(
        _post_odd_kernel,
        grid=(n // tm,),
        in_specs=[wide, wide, wide, wide, narrow, narrow, narrow, _resident(ex.shape), _resident(wo.shape),
                  _resident((1, D_MODEL)), _resident(wg.shape), _resident(wu.shape), _resident(wd.shape),
                  _resident((1, D_MODEL))],
        out_specs=wide,
        out_shape=jax.ShapeDtypeStruct((n, D_MODEL), F32),
        compiler_params=pltpu.CompilerParams(dimension_semantics=("parallel",), vmem_limit_bytes=VMEM_LIMIT),
        name="post_odd",
    )(x2, *os_, *ls_, ex, wo, g, wg, wu, wd, gf)


def _in_odd_kernel(x_ref, g_ref, w_ref, q_ref, k_ref, v_ref):
    hn = _rms(x_ref[...], g_ref[...]).astype(BF16)
    p = jnp.dot(hn, w_ref[...], preferred_element_type=F32)
    q_ref[...] = (p[:, :C_W] * (SCALE * LOG2E)).astype(BF16)
    k_ref[...] = p[:, C_W:2 * C_W].astype(BF16)
    v_ref[...] = p[:, 2 * C_W:].astype(BF16)


def _in_odd(x2, g, w):
    n = x2.shape[0]
    tm = ROW_TILE
    row = lambda i: (i, 0)
    wide = pl.BlockSpec((tm, C_W), row)
    shape = jax.ShapeDtypeStruct((n, C_W), BF16)
    return pl.pallas_call(
        _in_odd_kernel,
        grid=(n // tm,),
        in_specs=[pl.BlockSpec((tm, D_MODEL), row), _resident((1, D_MODEL)), _resident(w.shape)],
        out_specs=(wide, wide, wide),
        out_shape=(shape, shape, shape),
        compiler_params=pltpu.CompilerParams(dimension_semantics=("parallel",), vmem_limit_bytes=VMEM_LIMIT),
        name="in_odd",
    )(x2, g, w)


def _band_kernel(sl_ref, q_ref, kp_ref, kc_ref, kn_ref, vp_ref, vc_ref, vn_ref, o_ref, lse_ref,
                 kwin_ref, vt_ref, pen_ref, *, length, tile):
    t = pl.program_id(1)
    n_pairs = C_W // 128
    win = C_TQ + 2 * C_HALF
    span = tile + 2 * C_HALF
    kwin_ref[0:C_HALF, :] = kp_ref[C_TQ - C_HALF:C_TQ, :]
    kwin_ref[C_HALF:C_HALF + tile, :] = kc_ref[...]
    kwin_ref[C_HALF + tile:span, :] = kn_ref[0:C_HALF, :]

    key = lax.broadcasted_iota(jnp.int32, (win, 2 * C_TQ), 0)
    lane = lax.broadcasted_iota(jnp.int32, (win, 2 * C_TQ), 1)
    dist = jnp.abs(key - C_HALF - (lane & (C_TQ - 1)))
    distf = dist.astype(F32)
    ones = jnp.ones((V_ROWS - HEAD_DIM, span), BF16)
    for c in range(n_pairs):
        cs = slice(c * 128, (c + 1) * 128)
        vwin = jnp.concatenate([vp_ref[C_TQ - C_HALF:C_TQ, cs], vc_ref[:, cs], vn_ref[0:C_HALF, cs]], axis=0)
        vt = vwin.astype(F32).T.astype(BF16)
        for hh in range(2):
            vt_ref[c, hh * V_ROWS:hh * V_ROWS + HEAD_DIM, :] = vt[hh * HEAD_DIM:(hh + 1) * HEAD_DIM, :]
            vt_ref[c, hh * V_ROWS + HEAD_DIM:(hh + 1) * V_ROWS, :] = ones
        slope = jnp.where(lane < C_TQ, sl_ref[2 * c], sl_ref[2 * c + 1])
        pen_ref[c] = jnp.where(dist <= C_HALF, slope * distf, -NEG)

    def scores(j, c):
        cs = slice(c * 128, (c + 1) * 128)
        qs = _split_pair(q_ref[j * C_TQ:(j + 1) * C_TQ, cs])
        k = kwin_ref[j * C_TQ:j * C_TQ + win, cs]
        return lax.dot_general(k, qs, _NT, preferred_element_type=F32)

    n_blocks = tile // C_TQ
    first_key = t * tile - C_HALF + key
    edge = {0: jnp.where(first_key >= 0, 0.0, -NEG),
            n_blocks - 1: jnp.where(first_key + (n_blocks - 1) * C_TQ < length, 0.0, -NEG)}
    items = [(j, c) for j in range(n_blocks) for c in range(n_pairs)]
    ahead = [scores(*item) for item in items[:C_AHEAD]]
    for n_item, (j, c) in enumerate(items):
        rows = slice(j * C_TQ, (j + 1) * C_TQ)
        cs = slice(c * 128, (c + 1) * 128)
        s = ahead.pop(0) - pen_ref[c]
        if n_item + C_AHEAD < len(items):
            ahead.append(scores(*items[n_item + C_AHEAD]))
        if j in edge:
            s = s - edge[j]
        m = jnp.max(s, axis=0, keepdims=True)
        p = jnp.exp2((s - m).astype(BF16))
        acc = jnp.dot(vt_ref[c, :, j * C_TQ:j * C_TQ + win], p, preferred_element_type=F32)
        a0, a1 = acc[:V_ROWS, :C_TQ], acc[V_ROWS:, C_TQ:]
        o_ref[rows, cs] = _finish_pair(a0, a1)
        lse_ref[2 * c:2 * c + 1, rows] = m[:, :C_TQ] + jnp.log2(a0[HEAD_DIM:HEAD_DIM + 1])
        lse_ref[2 * c + 1:2 * c + 2, rows] = m[:, C_TQ:] + jnp.log2(a1[HEAD_DIM:HEAD_DIM + 1])


def _band(slopes, q, k, v):
    nb_, length, _ = q.shape
    tile = 512 if length % 512 == 0 else 256
    per = tile // C_TQ
    last = length // C_TQ - 1
    cur = lambda n, t: (n, t, 0)
    prev = lambda n, t: (n, jnp.maximum(t * per - 1, 0), 0)
    nxt = lambda n, t: (n, jnp.minimum((t + 1) * per, last), 0)
    big = lambda m: pl.BlockSpec((None, tile, C_W), m)
    halo = lambda m: pl.BlockSpec((None, C_TQ, C_W), m)
    span = tile + 2 * C_HALF
    return pl.pallas_call(
        functools.partial(_band_kernel, length=length, tile=tile),
        grid=(nb_, length // tile),
        in_specs=[pl.BlockSpec(memory_space=pltpu.SMEM), big(cur),
                  halo(prev), big(cur), halo(nxt), halo(prev), big(cur), halo(nxt)],
        out_specs=(big(cur), pl.BlockSpec((None, C_HEADS, tile), lambda n, t: (n, 0, t))),
        out_shape=(jax.ShapeDtypeStruct((nb_, length, C_W), BF16),
                   jax.ShapeDtypeStruct((nb_, C_HEADS, length), F32)),
        scratch_shapes=[pltpu.VMEM((span, C_W), BF16), pltpu.VMEM((C_W // 128, 2 * V_ROWS, span), BF16),
                        pltpu.VMEM((C_W // 128, C_TQ + 2 * C_HALF, 2 * C_TQ), F32)],
        compiler_params=pltpu.CompilerParams(dimension_semantics=("parallel", "parallel"),
                                             vmem_limit_bytes=VMEM_LIMIT),
        name="band",
    )(slopes, q, k, k, k, v, v, v)


def _fold(a, bn, seq, dil):
    w = a.shape[-1]
    return a.reshape(bn, seq // dil, dil, w).transpose(0, 2, 1, 3).reshape(bn * dil, seq // dil, w)


def _unfold(a, bn, seq, dil):
    w = a.shape[-1]
    return a.reshape(bn, dil, seq // dil, w).transpose(0, 2, 1, 3).reshape(bn * seq, w)


def _rope_tables(seq):
    t = jnp.arange(seq)
    n = HEAD_DIM // 4
    freqs = jnp.power(ROPE_THETA, -jnp.arange(n, dtype=F32) / n)
    ang_r = (t // GRID_W).astype(F32)[:, None] * freqs[None, :]
    ang_c = (t % GRID_W).astype(F32)[:, None] * freqs[None, :]
    zero = jnp.zeros_like(ang_r)
    cos = jnp.concatenate([jnp.cos(ang_r)] * 2 + [jnp.cos(ang_c)] * 2, axis=-1)
    sin_r, sin_c = jnp.sin(ang_r), jnp.sin(ang_c)
    sa = jnp.concatenate([-sin_r, zero, -sin_c, zero], axis=-1)
    sb = jnp.concatenate([zero, sin_r, zero, sin_c], axis=-1)
    two = lambda a: jnp.concatenate([a, a], axis=-1)
    return two(cos), two(sa), two(sb)


def _trunk(x, prm):
    bn, seq, _ = x.shape
    n = bn * seq
    x2 = x.reshape(n, D_MODEL)
    cos, sa, sb = _rope_tables(seq)
    qa, ka, vat, qb, kb, vbt = _in_even(x2, seq, prm["g_mix0"], prm["w_in_even"], prm["gq"], prm["gk"],
                                        cos, sa, sb, prm["bd"])
    oa = _mix_a(qa, ka, vat, bn, seq)
    ob = _mix_b(qb, kb, vbt, _neighbourhood_bias(prm["rpb"], seq // GRID_W), bn, seq)
    x2 = _post_even(x2, oa, ob, prm["wo_a"], prm["wo_b"], prm["g_ffn0"], prm["wg0"], prm["wu0"], prm["wd0"])

    q, k, v = _in_odd(x2, prm["g_mix1"], prm["w_in_odd"])
    outs, lses = [], []
    for _, dil in C_GROUPS:
        fold = (lambda a: a.reshape(bn, seq, C_W)) if dil == 1 else (lambda a: _fold(a, bn, seq, dil))
        o, lse = _band(prm["slopes"] * dil * LOG2E, fold(q), fold(k), fold(v))
        outs.append(_unfold(o, bn, seq, dil))
        lses.append(lse.reshape(bn, dil, C_HEADS, seq // dil).transpose(0, 3, 1, 2).reshape(n, C_HEADS))
    y = _post_odd(x2, outs, lses, prm["expand"], prm["wo_odd"], prm["g_ffn1"], prm["wg1"], prm["wu1"],
                  prm["wd1"], prm["g_final"])
    return y.reshape(bn, seq, D_MODEL)


def kernel(x_prompt, x_sample, norm_mix, norm_ffn, norm_final, w_in_even, a_q_norm, a_k_norm, na_rpb,
           w_out_even, w_in_odd, w_out_odd, w_gate_up, w_down):
    head = np.arange(128) // HEAD_DIM
    order = np.array([0, 4, 1, 5, 2, 6, 3, 7])
    w_in0 = w_in_even[0]
    w_qa = w_in0[:, :A_Q].reshape(D_MODEL, A_Q // HEAD_DIM, HEAD_DIM)[:, order].reshape(D_MODEL, A_Q)
    w_in0 = jnp.concatenate([w_qa, w_in0[:, A_Q:]], axis=1)
    wo_a = w_out_even[0, :A_Q].reshape(A_Q // HEAD_DIM, HEAD_DIM, D_MODEL)[order].reshape(A_Q, D_MODEL)
    prm = {
        "g_mix0": norm_mix[0][None].astype(F32), "g_mix1": norm_mix[1][None].astype(F32),
        "g_ffn0": norm_ffn[0][None].astype(F32), "g_ffn1": norm_ffn[1][None].astype(F32),
        "g_final": norm_final[None].astype(F32),
        "w_in_even": w_in0.astype(BF16), "w_in_odd": w_in_odd[0].astype(BF16),
        "gq": jnp.tile(a_q_norm[0].astype(F32) * (SCALE * LOG2E), 2)[None], "gk": jnp.tile(a_k_norm[0].astype(F32), 2)[None],
        "bd": jnp.asarray((head[:, None] == head[None, :]) / HEAD_DIM, BF16),
        "rpb": na_rpb[0],
        "wo_a": wo_a.astype(BF16), "wo_b": w_out_even[0, A_Q:].astype(BF16),
        "wo_odd": w_out_odd[0].astype(BF16),
        "wg0": w_gate_up[0, :, :D_FF].astype(BF16), "wu0": w_gate_up[0, :, D_FF:].astype(BF16),
        "wg1": w_gate_up[1, :, :D_FF].astype(BF16), "wu1": w_gate_up[1, :, D_FF:].astype(BF16),
        "wd0": w_down[0].astype(BF16), "wd1": w_down[1].astype(BF16),
        "slopes": jnp.exp2(-8.0 * (jnp.arange(C_HEADS, dtype=F32) + 1.0) / C_HEADS),
        "expand": jnp.asarray(np.arange(C_HEADS)[:, None] == (np.arange(C_W) // HEAD_DIM)[None, :], BF16),
    }
    return (_trunk(x_prompt, prm), _trunk(x_sample, prm))
```

```python
import functools

import jax
import jax.numpy as jnp
import numpy as np
from jax import lax
from jax.experimental import pallas as pl
from jax.experimental.pallas import tpu as pltpu

F32 = jnp.float32
BF16 = jnp.bfloat16

D_MODEL = 1024
HEAD_DIM = 64
A_Q = 512
A_KV = 128
B_W = 512
C_W = 1024
C_HEADS = 16
B_HEADS = 8
GRID_W = 64
NA_ROWS = 8
NA_COLS = 16
C_GROUPS = ((128, 1), (512, 4), (2048, 16))
ROPE_THETA = 10000.0
D_FF = 2816
FF_CHUNK = 1408
EPS = 1e-6
NEG = -1e30
SCALE = HEAD_DIM ** -0.5
LOG2E = 1.4426950408889634

V_ROWS = HEAD_DIM + 16
ROW_TILE = 512
A_TQ = 256
A_TK = 256
A_UNROLL = 4
A_AHEAD = 3
B_TQ = 256
B_TK = 3 * B_TQ
B_AHEAD = 2
C_TQ = 128
C_HALF = 64
C_AHEAD = 3
VMEM_LIMIT = 56 * 1024 * 1024

_NT = (((1,), (1,)), ((), ()))


def _resident(shape):
    zeros = (0,) * len(shape)
    return pl.BlockSpec(shape, lambda *_: zeros, pipeline_mode=pl.Buffered(1))


def _rms(x, gain):
    ms = jnp.mean(x * x, axis=-1, keepdims=True)
    return x * lax.rsqrt(ms + EPS) * gain


def _in_even_kernel(x_ref, g_ref, w_ref, gq_ref, gk_ref, cos_ref, sa_ref, sb_ref, bd_ref,
                    qa_ref, ka_ref, vat_ref, qb_ref, kb_ref, vbt_ref):
    hn = _rms(x_ref[...], g_ref[...]).astype(BF16)
    p = jnp.dot(hn, w_ref[...], preferred_element_type=F32)
    cos, sa, sb, bd = cos_ref[...], sa_ref[...], sb_ref[...], bd_ref[...]

    def headnorm_rope(c, gain):
        c2 = c * c
        hi = c2.astype(BF16)
        lo = (c2 - hi.astype(F32)).astype(BF16)
        ms = (jnp.dot(hi, bd, preferred_element_type=F32)
              + jnp.dot(lo, bd, preferred_element_type=F32))
        y = c * lax.rsqrt(ms + EPS) * gain
        return y * cos + pltpu.roll(y, 112, 1) * sa + pltpu.roll(y, 16, 1) * sb

    for j in range(A_Q // 128):
        qa_ref[:, j * 128:(j + 1) * 128] = headnorm_rope(p[:, j * 128:(j + 1) * 128], gq_ref[...]).astype(BF16)
    ka_ref[...] = headnorm_rope(p[:, A_Q:A_Q + A_KV], gk_ref[...]).astype(BF16)
    o = A_Q + 2 * A_KV
    qb_ref[...] = (p[:, o:o + B_W] * (SCALE * LOG2E)).astype(BF16)
    kb_ref[...] = p[:, o + B_W:o + 2 * B_W].astype(BF16)

    def put_transposed(dst_ref, col0, n_pairs):
        ones = jnp.ones((V_ROWS - HEAD_DIM, A_TK), BF16)
        for j in range(p.shape[0] // A_TK):
            for c in range(n_pairs):
                t = p[j * A_TK:(j + 1) * A_TK, col0 + c * 128:col0 + (c + 1) * 128].T.astype(BF16)
                for hh in range(2):
                    r0 = (2 * c + hh) * V_ROWS
                    dst_ref[j, r0:r0 + HEAD_DIM, :] = t[hh * HEAD_DIM:(hh + 1) * HEAD_DIM, :]
                    dst_ref[j, r0 + HEAD_DIM:r0 + V_ROWS, :] = ones

    put_transposed(vat_ref, A_Q + A_KV, A_KV // 128)
    put_transposed(vbt_ref, o + 2 * B_W, B_W // 128)


def _in_even(x2, seq, g, w, gq, gk, cos, sa, sb, bd):
    n = x2.shape[0]
    tm = ROW_TILE
    nt = seq // tm
    row = lambda i: (i, 0)
    pos = lambda i: (i % nt, 0)
    va_rows, vb_rows = (A_KV // HEAD_DIM) * V_ROWS, B_HEADS * V_ROWS
    out_shape = (
        jax.ShapeDtypeStruct((n, A_Q), BF16), jax.ShapeDtypeStruct((n, A_KV), BF16),
        jax.ShapeDtypeStruct((n // A_TK, va_rows, A_TK), BF16),
        jax.ShapeDtypeStruct((n, B_W), BF16), jax.ShapeDtypeStruct((n, B_W), BF16),
        jax.ShapeDtypeStruct((n // A_TK, vb_rows, A_TK), BF16))
    return pl.pallas_call(
        _in_even_kernel,
        grid=(n // tm,),
        in_specs=[pl.BlockSpec((tm, D_MODEL), row), _resident((1, D_MODEL)), _resident(w.shape),
                  _resident((1, 128)), _resident((1, 128)),
                  pl.BlockSpec((tm, 128), pos), pl.BlockSpec((tm, 128), pos), pl.BlockSpec((tm, 128), pos),
                  _resident((128, 128))],
        out_specs=(pl.BlockSpec((tm, A_Q), row), pl.BlockSpec((tm, A_KV), row),
                   pl.BlockSpec((tm // A_TK, va_rows, A_TK), lambda i: (i, 0, 0)),
                   pl.BlockSpec((tm, B_W), row), pl.BlockSpec((tm, B_W), row),
                   pl.BlockSpec((tm // A_TK, vb_rows, A_TK), lambda i: (i, 0, 0))),
        out_shape=out_shape,
        compiler_params=pltpu.CompilerParams(dimension_semantics=("parallel",), vmem_limit_bytes=VMEM_LIMIT),
        name="in_even",
    )(x2, g, w, gq, gk, cos, sa, sb, bd)


def _split_pair(qp):
    low = lax.broadcasted_iota(jnp.int32, qp.shape, 1) < HEAD_DIM
    zero = jnp.zeros(qp.shape, qp.dtype)
    return jnp.concatenate([jnp.where(low, qp, zero), jnp.where(low, zero, qp)], axis=0)


def _finish_pair(acc_lo, acc_hi):
    halves = [a[:HEAD_DIM] / a[HEAD_DIM:HEAD_DIM + 1] for a in (acc_lo, acc_hi)]
    return jnp.concatenate(halves, axis=0).T.astype(BF16)


def _mix_a_kernel(q_ref, k_ref, vt_ref, o_ref, qs_ref, m_ref, acc_ref, *, n_chunks):
    tq = A_TQ
    n_pairs = A_Q // 128
    for c in range(n_pairs):
        qs_ref[2 * c * tq:(2 * c + 2) * tq, :] = _split_pair(q_ref[:, c * 128:(c + 1) * 128])
    m_ref[...] = jnp.full(m_ref.shape, NEG, F32)
    acc_ref[...] = jnp.zeros(acc_ref.shape, F32)

    def scores(ci, c):
        k = k_ref[pl.ds(pl.multiple_of(ci * A_TK, A_TK), A_TK), :]
        return lax.dot_general(k, qs_ref[2 * c * tq:(2 * c + 2) * tq, :], _NT,
                               preferred_element_type=F32)

    def chunks(it, carry):
        items = [(it * A_UNROLL + u, c) for u in range(A_UNROLL) for c in range(n_pairs)]
        ahead = [scores(*item) for item in items[:A_AHEAD]]
        for t, (ci, c) in enumerate(items):
            s = ahead.pop(0)
            if t + A_AHEAD < len(items):
                ahead.append(scores(*items[t + A_AHEAD]))
            cols = slice(2 * c * tq, (2 * c + 2) * tq)
            m_old = m_ref[:, cols]
            m_new = jnp.maximum(m_old, jnp.max(s, axis=0, keepdims=True))
            alpha = jnp.exp2(m_old - m_new)
            p = jnp.exp2((s - m_new).astype(BF16))
            m_ref[:, cols] = m_new
            for g in range(2):
                vt = vt_ref[ci, g * V_ROWS:(g + 1) * V_ROWS, :]
                pv = jnp.dot(vt, p[:, g * tq:(g + 1) * tq], preferred_element_type=F32)
                acc_ref[2 * c + g] = alpha[:, g * tq:(g + 1) * tq] * acc_ref[2 * c + g] + pv
        return carry

    lax.fori_loop(0, n_chunks // A_UNROLL, chunks, 0)

    for c in range(n_pairs):
        o_ref[:, c * 128:(c + 1) * 128] = _finish_pair(acc_ref[2 * c], acc_ref[2 * c + 1])


def _mix_a(qa, ka, vat, bn, seq):
    n = qa.shape[0]
    nq = seq // A_TQ
    nc = seq // A_TK
    return pl.pallas_call(
        functools.partial(_mix_a_kernel, n_chunks=nc),
        grid=(bn, nq),
        in_specs=[pl.BlockSpec((A_TQ, A_Q), lambda b, i: (b * nq + i, 0)),
                  pl.BlockSpec((seq, A_KV), lambda b, i: (b, 0)),
                  pl.BlockSpec((nc, vat.shape[1], A_TK), lambda b, i: (b, 0, 0))],
        out_specs=pl.BlockSpec((A_TQ, A_Q), lambda b, i: (b * nq + i, 0)),
        out_shape=jax.ShapeDtypeStruct((n, A_Q), BF16),
        scratch_shapes=[pltpu.VMEM((8 * A_TQ, 128), BF16), pltpu.VMEM((1, 8 * A_TQ), F32),
                        pltpu.VMEM((8, V_ROWS, A_TQ), F32)],
        compiler_params=pltpu.CompilerParams(dimension_semantics=("parallel", "parallel"),
                                             vmem_limit_bytes=VMEM_LIMIT),
        name="mix_a",
    )(qa, ka, vat)


def _mix_b_kernel(q_ref, k0_ref, k1_ref, k2_ref, vt0_ref, vt1_ref, vt2_ref, bias_ref, o_ref):
    tq = B_TQ
    n_pairs = B_W // 128

    k_refs = (k0_ref, k1_ref, k2_ref)
    vt_refs = (vt0_ref, vt1_ref, vt2_ref)

    def scores(c):
        cs = slice(c * 128, (c + 1) * 128)
        qs = _split_pair(q_ref[:, cs])
        return [lax.dot_general(k_ref[:, cs], qs, _NT, preferred_element_type=F32) for k_ref in k_refs]

    ahead = [scores(c) for c in range(B_AHEAD)]
    for c in range(n_pairs):
        cs = slice(c * 128, (c + 1) * 128)
        s = [sj + bias_ref[0, c, j * tq:(j + 1) * tq, :] for j, sj in enumerate(ahead.pop(0))]
        if c + B_AHEAD < n_pairs:
            ahead.append(scores(c + B_AHEAD))
        m = jnp.max(jnp.maximum(jnp.maximum(s[0], s[1]), s[2]), axis=0, keepdims=True)
        p = [jnp.exp2((sj - m).astype(BF16)) for sj in s]
        accs = []
        for hh in range(2):
            rows = slice((2 * c + hh) * V_ROWS, (2 * c + hh + 1) * V_ROWS)
            parts = [jnp.dot(vt_ref[rows, :], pj[:, hh * tq:(hh + 1) * tq], preferred_element_type=F32)
                     for vt_ref, pj in zip(vt_refs, p)]
            accs.append(parts[0] + parts[1] + parts[2])
        o_ref[:, cs] = _finish_pair(*accs)


def _mix_b(qb, kb, vbt, bias, bn, seq):
    n = qb.shape[0]
    nb = seq // B_TQ
    q_map = lambda b, i: (b * nb + i, 0)

    def kv_map(j):
        return lambda b, i: (b * nb + jnp.clip(i - 1, 0, nb - 3) + j, 0)

    def vt_map(j):
        return lambda b, i: (b * nb + jnp.clip(i - 1, 0, nb - 3) + j, 0, 0)

    def bias_map(b, i):
        return (jnp.where(i == 0, 0, jnp.where(i == nb - 1, 2, 1)), 0, 0, 0)

    blk = lambda m: pl.BlockSpec((B_TQ, B_W), m)
    vblk = lambda m: pl.BlockSpec((None, vbt.shape[1], B_TQ), m)
    return pl.pallas_call(
        _mix_b_kernel,
        grid=(bn, nb),
        in_specs=[blk(q_map), blk(kv_map(0)), blk(kv_map(1)), blk(kv_map(2)),
                  vblk(vt_map(0)), vblk(vt_map(1)), vblk(vt_map(2)),
                  pl.BlockSpec((1, B_W // 128, B_TK, 2 * B_TQ), bias_map)],
        out_specs=blk(q_map),
        out_shape=jax.ShapeDtypeStruct((n, B_W), BF16),
        compiler_params=pltpu.CompilerParams(dimension_semantics=("parallel", "parallel"),
                                             vmem_limit_bytes=VMEM_LIMIT),
        name="mix_b",
    )(qb, kb, kb, kb, vbt, vbt, vbt, bias)


def _neighbourhood_bias(rpb, rows):
    n_dc = 2 * NA_COLS - 1
    c = np.arange(GRID_W)
    cs = np.clip(c - NA_COLS // 2, 0, GRID_W - NA_COLS)
    col_ok = (c[:, None] >= cs[None, :]) & (c[:, None] < cs[None, :] + NA_COLS)
    dc = np.clip(c[:, None] - c[None, :] + NA_COLS - 1, 0, n_dc - 1)
    pick = (dc.reshape(-1)[None, :] == np.arange(n_dc)[:, None]).astype(np.float32)
    by_col = jnp.einsum("hrd,dx->hrx", rpb.astype(F32) * LOG2E, jnp.asarray(pick),
                        precision=lax.Precision.HIGHEST)
    by_col = by_col.reshape(B_HEADS, 2 * NA_ROWS - 1, GRID_W, GRID_W)
    by_col = jnp.where(col_ok[None, None], by_col, NEG)
    masked = jnp.full((B_HEADS, GRID_W, GRID_W), NEG, F32)
    tables = []
    for r0, ws in ((0, 0), (4, 0), (rows - 4, rows - 12)):
        key_rows = []
        for kr in range(B_TK // GRID_W):
            blocks = []
            for rr in range(B_TQ // GRID_W):
                r = r0 + rr
                rs = min(max(r - NA_ROWS // 2, 0), rows - NA_ROWS)
                ok = rs <= ws + kr < rs + NA_ROWS
                blocks.append(by_col[:, ws + kr - r + NA_ROWS - 1] if ok else masked)
            key_rows.append(jnp.concatenate(blocks, axis=-1))
        t = jnp.concatenate(key_rows, axis=-2)
        t = t.reshape(B_HEADS // 2, 2, B_TK, B_TQ).transpose(0, 2, 1, 3).reshape(B_HEADS // 2, B_TK, 2 * B_TQ)
        tables.append(t)
    return jnp.stack(tables, axis=0)


def _ffn(x1, gain, wg_ref, wu_ref, wd_ref):
    hn = _rms(x1, gain).astype(BF16)
    acc = None
    for c in range(D_FF // FF_CHUNK):
        cs = slice(c * FF_CHUNK, (c + 1) * FF_CHUNK)
        gate = jnp.dot(hn, wg_ref[:, cs], preferred_element_type=F32)
        up = jnp.dot(hn, wu_ref[:, cs], preferred_element_type=F32)
        act = (gate * (1.0 / (1.0 + jnp.exp(-gate))) * up).astype(BF16)
        d = jnp.dot(act, wd_ref[cs, :], preferred_element_type=F32)
        acc = d if acc is None else acc + d
    return x1 + acc


def _post_even_kernel(x_ref, oa_ref, ob_ref, woa_ref, wob_ref, g_ref, wg_ref, wu_ref, wd_ref, y_ref):
    mix = (jnp.dot(oa_ref[...], woa_ref[...], preferred_element_type=F32)
           + jnp.dot(ob_ref[...], wob_ref[...], preferred_element_type=F32))
    y_ref[...] = _ffn(x_ref[...] + mix, g_ref[...], wg_ref, wu_ref, wd_ref)


def _post_even(x2, oa, ob, woa, wob, g, wg, wu, wd):
    n = x2.shape[0]
    tm = ROW_TILE
    row = lambda i: (i, 0)
    return pl.pallas_call(
        _post_even_kernel,
        grid=(n // tm,),
        in_specs=[pl.BlockSpec((tm, D_MODEL), row), pl.BlockSpec((tm, A_Q), row), pl.BlockSpec((tm, B_W), row),
                  _resident(woa.shape), _resident(wob.shape), _resident((1, D_MODEL)),
                  _resident(wg.shape), _resident(wu.shape), _resident(wd.shape)],
        out_specs=pl.BlockSpec((tm, D_MODEL), row),
        out_shape=jax.ShapeDtypeStruct((n, D_MODEL), F32),
        compiler_params=pltpu.CompilerParams(dimension_semantics=("parallel",), vmem_limit_bytes=VMEM_LIMIT),
        name="post_even",
    )(x2, oa, ob, woa, wob, g, wg, wu, wd)


def _post_odd_kernel(x_ref, o1_ref, of2_ref, of3_ref, l1_ref, l2_ref, l3_ref, ex_ref, wo_ref, g_ref,
                     wg_ref, wu_ref, wd_ref, gf_ref, y_ref, o2_ref, o3_ref):
    tm = x_ref.shape[0]
    n_cb = C_W // 128
    for (_, dil), src, dst in zip(C_GROUPS[1:], (of2_ref, of3_ref), (o2_ref, o3_ref)):
        for rho in range(dil):
            for cb in range(n_cb):
                dst[cb, pl.ds(rho, tm // dil, stride=dil), :] = src[0, rho, :, cb * 128:(cb + 1) * 128].astype(F32)
    o2 = jnp.concatenate([o2_ref[cb] for cb in range(n_cb)], axis=1)
    o3 = jnp.concatenate([o3_ref[cb] for cb in range(n_cb)], axis=1)
    l1, l2, l3 = l1_ref[...], l2_ref[...], l3_ref[...]
    mx = jnp.maximum(jnp.maximum(l1, l2), l3)
    e1, e2, e3 = jnp.exp2(l1 - mx), jnp.exp2(l2 - mx), jnp.exp2(l3 - mx)
    den = e1 + e2 + e3
    ex = ex_ref[...]

    def widen(w):
        hi = w.astype(BF16)
        lo = (w - hi.astype(F32)).astype(BF16)
        return jnp.dot(hi, ex, preferred_element_type=F32) + jnp.dot(lo, ex, preferred_element_type=F32)

    o = widen(e1 / den) * o1_ref[...].astype(F32) + widen(e2 / den) * o2 + widen(e3 / den) * o3
    mix = jnp.dot(o.astype(BF16), wo_ref[...], preferred_element_type=F32)
    y = _ffn(x_ref[...] + mix, g_ref[...], wg_ref, wu_ref, wd_ref)
    y_ref[...] = _rms(y, gf_ref[...])


def _post_odd(x2, seq, os_, ls_, ex, wo, g, wg, wu, wd, gf):
    n = x2.shape[0]
    tm = ROW_TILE
    nt = seq // tm
    row = lambda i: (i, 0)
    wide = pl.BlockSpec((tm, D_MODEL), row)
    narrow = pl.BlockSpec((tm, C_HEADS), row)
    folded = [pl.BlockSpec((1, dil, tm // dil, C_W), lambda i: (i // nt, 0, i % nt, 0)) for _, dil in C_GROUPS[1:]]
    os_ = [os_[0].reshape(n, C_W)] + [o.reshape(n // seq, dil, seq // dil, C_W)
                                      for o, (_, dil) in zip(os_[1:], C_GROUPS[1:])]
    return pl.pallas_call(
        _post_odd_kernel,
        grid=(n // tm,),
        in_specs=[wide, wide, *folded, narrow, narrow, narrow, _resident(ex.shape), _resident(wo.shape),
                  _resident((1, D_MODEL)), _resident(wg.shape), _resident(wu.shape), _resident(wd.shape),
                  _resident((1, D_MODEL))],
        out_specs=wide,
        out_shape=jax.ShapeDtypeStruct((n, D_MODEL), F32),
        scratch_shapes=[pltpu.VMEM((C_W // 128, tm, 128), F32), pltpu.VMEM((C_W // 128, tm, 128), F32)],
        compiler_params=pltpu.CompilerParams(dimension_semantics=("parallel",), vmem_limit_bytes=VMEM_LIMIT),
        name="post_odd",
    )(x2, *os_, *ls_, ex, wo, g, wg, wu, wd, gf)


def _in_odd_kernel(x_ref, g_ref, w_ref, *refs):
    out_refs, p_ref = refs[:-1], refs[-1]
    tm = x_ref.shape[0]
    hn = _rms(x_ref[...], g_ref[...]).astype(BF16)
    n_cb = C_W // 128

    def project(part):
        return jnp.dot(hn, w_ref[:, part * C_W:(part + 1) * C_W], preferred_element_type=F32)

    p_next = project(0)
    for part in range(3):
        p = p_next
        if part + 1 < 3:
            p_next = project(part + 1)
        for cb in range(n_cb):
            p_ref[part * n_cb + cb] = p[:, cb * 128:(cb + 1) * 128]
        for gi, (_, dil) in enumerate(C_GROUPS):
            dst = out_refs[part * len(C_GROUPS) + gi]
            for rho in range(dil):
                rows = slice(None) if dil == 1 else pl.ds(rho, tm // dil, stride=dil)
                for cb in range(n_cb):
                    val = p_ref[part * n_cb + cb, rows, :]
                    if part == 0:
                        val = val * (SCALE * LOG2E)
                    dst[0, rho, :, cb * 128:(cb + 1) * 128] = val.astype(BF16)


def _in_odd(x2, bn, seq, g, w):
    n = x2.shape[0]
    tm = ROW_TILE
    nt = seq // tm
    specs, shapes = [], []
    for _ in range(3):
        for _, dil in C_GROUPS:
            specs.append(pl.BlockSpec((1, dil, tm // dil, C_W), lambda i: (i // nt, 0, i % nt, 0)))
            shapes.append(jax.ShapeDtypeStruct((bn, dil, seq // dil, C_W), BF16))
    outs = pl.pallas_call(
        _in_odd_kernel,
        grid=(n // tm,),
        in_specs=[pl.BlockSpec((tm, D_MODEL), lambda i: (i, 0)), _resident((1, D_MODEL)), _resident(w.shape)],
        out_specs=tuple(specs),
        out_shape=tuple(shapes),
        scratch_shapes=[pltpu.VMEM((3 * C_W // 128, tm, 128), F32)],
        compiler_params=pltpu.CompilerParams(dimension_semantics=("parallel",), vmem_limit_bytes=VMEM_LIMIT),
        name="in_odd",
    )(x2, g, w)
    outs = [a.reshape(a.shape[0] * a.shape[1], a.shape[2], C_W) for a in outs]
    n_g = len(C_GROUPS)
    return outs[:n_g], outs[n_g:2 * n_g], outs[2 * n_g:]


def _band_kernel(sl_ref, q_ref, kp_ref, kc_ref, kn_ref, vp_ref, vc_ref, vn_ref, o_ref, lse_ref,
                 kwin_ref, vt_ref, pen_ref, *, length, tile):
    t = pl.program_id(1)
    n_pairs = C_W // 128
    win = C_TQ + 2 * C_HALF
    span = tile + 2 * C_HALF
    kwin_ref[0:C_HALF, :] = kp_ref[C_TQ - C_HALF:C_TQ, :]
    kwin_ref[C_HALF:C_HALF + tile, :] = kc_ref[...]
    kwin_ref[C_HALF + tile:span, :] = kn_ref[0:C_HALF, :]

    key = lax.broadcasted_iota(jnp.int32, (win, 2 * C_TQ), 0)
    lane = lax.broadcasted_iota(jnp.int32, (win, 2 * C_TQ), 1)
    dist = jnp.abs(key - C_HALF - (lane & (C_TQ - 1)))
    distf = dist.astype(F32)
    ones = jnp.ones((V_ROWS - HEAD_DIM, span), BF16)
    for c in range(n_pairs):
        cs = slice(c * 128, (c + 1) * 128)
        vwin = jnp.concatenate([vp_ref[C_TQ - C_HALF:C_TQ, cs], vc_ref[:, cs], vn_ref[0:C_HALF, cs]], axis=0)
        vt = vwin.astype(F32).T.astype(BF16)
        for hh in range(2):
            vt_ref[c, hh * V_ROWS:hh * V_ROWS + HEAD_DIM, :] = vt[hh * HEAD_DIM:(hh + 1) * HEAD_DIM, :]
            vt_ref[c, hh * V_ROWS + HEAD_DIM:(hh + 1) * V_ROWS, :] = ones
        slope = jnp.where(lane < C_TQ, sl_ref[2 * c], sl_ref[2 * c + 1])
        pen_ref[c] = jnp.where(dist <= C_HALF, slope * distf, -NEG)

    def scores(j, c):
        cs = slice(c * 128, (c + 1) * 128)
        qs = _split_pair(q_ref[j * C_TQ:(j + 1) * C_TQ, cs])
        k = kwin_ref[j * C_TQ:j * C_TQ + win, cs]
        return lax.dot_general(k, qs, _NT, preferred_element_type=F32)

    n_blocks = tile // C_TQ
    first_key = t * tile - C_HALF + key
    edge = {0: jnp.where(first_key >= 0, 0.0, -NEG),
            n_blocks - 1: jnp.where(first_key + (n_blocks - 1) * C_TQ < length, 0.0, -NEG)}
    items = [(j, c) for j in range(n_blocks) for c in range(n_pairs)]
    ahead = [scores(*item) for item in items[:C_AHEAD]]
    for n_item, (j, c) in enumerate(items):
        rows = slice(j * C_TQ, (j + 1) * C_TQ)
        cs = slice(c * 128, (c + 1) * 128)
        s = ahead.pop(0) - pen_ref[c]
        if n_item + C_AHEAD < len(items):
            ahead.append(scores(*items[n_item + C_AHEAD]))
        if j in edge:
            s = s - edge[j]
        m = jnp.max(s, axis=0, keepdims=True)
        p = jnp.exp2((s - m).astype(BF16))
        acc = jnp.dot(vt_ref[c, :, j * C_TQ:j * C_TQ + win], p, preferred_element_type=F32)
        a0, a1 = acc[:V_ROWS, :C_TQ], acc[V_ROWS:, C_TQ:]
        o_ref[rows, cs] = _finish_pair(a0, a1)
        lse_ref[2 * c:2 * c + 1, rows] = m[:, :C_TQ] + jnp.log2(a0[HEAD_DIM:HEAD_DIM + 1])
        lse_ref[2 * c + 1:2 * c + 2, rows] = m[:, C_TQ:] + jnp.log2(a1[HEAD_DIM:HEAD_DIM + 1])


def _band(slopes, q, k, v):
    nb_, length, _ = q.shape
    tile = 512 if length % 512 == 0 else 256
    per = tile // C_TQ
    last = length // C_TQ - 1
    cur = lambda n, t: (n, t, 0)
    prev = lambda n, t: (n, jnp.maximum(t * per - 1, 0), 0)
    nxt = lambda n, t: (n, jnp.minimum((t + 1) * per, last), 0)
    big = lambda m: pl.BlockSpec((None, tile, C_W), m)
    halo = lambda m: pl.BlockSpec((None, C_TQ, C_W), m)
    span = tile + 2 * C_HALF
    return pl.pallas_call(
        functools.partial(_band_kernel, length=length, tile=tile),
        grid=(nb_, length // tile),
        in_specs=[pl.BlockSpec(memory_space=pltpu.SMEM), big(cur),
                  halo(prev), big(cur), halo(nxt), halo(prev), big(cur), halo(nxt)],
        out_specs=(big(cur), pl.BlockSpec((None, C_HEADS, tile), lambda n, t: (n, 0, t))),
        out_shape=(jax.ShapeDtypeStruct((nb_, length, C_W), BF16),
                   jax.ShapeDtypeStruct((nb_, C_HEADS, length), F32)),
        scratch_shapes=[pltpu.VMEM((span, C_W), BF16), pltpu.VMEM((C_W // 128, 2 * V_ROWS, span), BF16),
                        pltpu.VMEM((C_W // 128, C_TQ + 2 * C_HALF, 2 * C_TQ), F32)],
        compiler_params=pltpu.CompilerParams(dimension_semantics=("parallel", "parallel"),
                                             vmem_limit_bytes=VMEM_LIMIT),
        name="band",
    )(slopes, q, k, k, k, v, v, v)


def _rope_tables(seq):
    t = jnp.arange(seq)
    n = HEAD_DIM // 4
    freqs = jnp.power(ROPE_THETA, -jnp.arange(n, dtype=F32) / n)
    ang_r = (t // GRID_W).astype(F32)[:, None] * freqs[None, :]
    ang_c = (t % GRID_W).astype(F32)[:, None] * freqs[None, :]
    zero = jnp.zeros_like(ang_r)
    cos = jnp.concatenate([jnp.cos(ang_r)] * 2 + [jnp.cos(ang_c)] * 2, axis=-1)
    sin_r, sin_c = jnp.sin(ang_r), jnp.sin(ang_c)
    sa = jnp.concatenate([-sin_r, zero, -sin_c, zero], axis=-1)
    sb = jnp.concatenate([zero, sin_r, zero, sin_c], axis=-1)
    two = lambda a: jnp.concatenate([a, a], axis=-1)
    return two(cos), two(sa), two(sb)


def _trunk(x, prm):
    bn, seq, _ = x.shape
    n = bn * seq
    x2 = x.reshape(n, D_MODEL)
    cos, sa, sb = _rope_tables(seq)
    qa, ka, vat, qb, kb, vbt = _in_even(x2, seq, prm["g_mix0"], prm["w_in_even"], prm["gq"], prm["gk"],
                                        cos, sa, sb, prm["bd"])
    oa = _mix_a(qa, ka, vat, bn, seq)
    ob = _mix_b(qb, kb, vbt, _neighbourhood_bias(prm["rpb"], seq // GRID_W), bn, seq)
    x2 = _post_even(x2, oa, ob, prm["wo_a"], prm["wo_b"], prm["g_ffn0"], prm["wg0"], prm["wu0"], prm["wd0"])

    qs, ks, vs = _in_odd(x2, bn, seq, prm["g_mix1"], prm["w_in_odd"])
    outs, lses = [], []
    for (_, dil), q, k, v in zip(C_GROUPS, qs, ks, vs):
        o, lse = _band(prm["slopes"] * dil * LOG2E, q, k, v)
        outs.append(o)
        lses.append(lse.reshape(bn, dil, C_HEADS, seq // dil).transpose(0, 3, 1, 2).reshape(n, C_HEADS))
    y = _post_odd(x2, seq, outs, lses, prm["expand"], prm["wo_odd"], prm["g_ffn1"], prm["wg1"], prm["wu1"],
                  prm["wd1"], prm["g_final"])
    return y.reshape(bn, seq, D_MODEL)


def kernel(x_prompt, x_sample, norm_mix, norm_ffn, norm_final, w_in_even, a_q_norm, a_k_norm, na_rpb,
           w_out_even, w_in_odd, w_out_odd, w_gate_up, w_down):
    head = np.arange(128) // HEAD_DIM
    order = np.array([0, 4, 1, 5, 2, 6, 3, 7])
    w_in0 = w_in_even[0]
    w_qa = w_in0[:, :A_Q].reshape(D_MODEL, A_Q // HEAD_DIM, HEAD_DIM)[:, order].reshape(D_MODEL, A_Q)
    w_in0 = jnp.concatenate([w_qa, w_in0[:, A_Q:]], axis=1)
    wo_a = w_out_even[0, :A_Q].reshape(A_Q // HEAD_DIM, HEAD_DIM, D_MODEL)[order].reshape(A_Q, D_MODEL)
    prm = {
        "g_mix0": norm_mix[0][None].astype(F32), "g_mix1": norm_mix[1][None].astype(F32),
        "g_ffn0": norm_ffn[0][None].astype(F32), "g_ffn1": norm_ffn[1][None].astype(F32),
        "g_final": norm_final[None].astype(F32),
        "w_in_even": w_in0.astype(BF16), "w_in_odd": w_in_odd[0].astype(BF16),
        "gq": jnp.tile(a_q_norm[0].astype(F32) * (SCALE * LOG2E), 2)[None], "gk": jnp.tile(a_k_norm[0].astype(F32), 2)[None],
        "bd": jnp.asarray((head[:, None] == head[None, :]) / HEAD_DIM, BF16),
        "rpb": na_rpb[0],
        "wo_a": wo_a.astype(BF16), "wo_b": w_out_even[0, A_Q:].astype(BF16),
        "wo_odd": w_out_odd[0].astype(BF16),
        "wg0": w_gate_up[0, :, :D_FF].astype(BF16), "wu0": w_gate_up[0, :, D_FF:].astype(BF16),
        "wg1": w_gate_up[1, :, :D_FF].astype(BF16), "wu1": w_gate_up[1, :, D_FF:].astype(BF16),
        "wd0": w_down[0].astype(BF16), "wd1": w_down[1].astype(BF16),
        "slopes": jnp.exp2(-8.0 * (jnp.arange(C_HEADS, dtype=F32) + 1.0) / C_HEADS),
        "expand": jnp.asarray(np.arange(C_HEADS)[:, None] == (np.arange(C_W) // HEAD_DIM)[None, :], BF16),
    }
    return (_trunk(x_prompt, prm), _trunk(x_sample, prm))
```

```python
import functools

import jax
import jax.numpy as jnp
import numpy as np
from jax import lax
from jax.experimental import pallas as pl
from jax.experimental.pallas import tpu as pltpu

F32 = jnp.float32
BF16 = jnp.bfloat16

D_MODEL = 1024
HEAD_DIM = 64
A_Q = 512
A_KV = 128
B_W = 512
C_W = 1024
C_HEADS = 16
B_HEADS = 8
GRID_W = 64
NA_ROWS = 8
NA_COLS = 16
C_GROUPS = ((128, 1), (512, 4), (2048, 16))
ROPE_THETA = 10000.0
D_FF = 2816
FF_CHUNK = 1408
EPS = 1e-6
NEG = -1e30
SCALE = HEAD_DIM ** -0.5
LOG2E = 1.4426950408889634

V_ROWS = HEAD_DIM + 16
ROW_TILE = 512
A_TQ = 256
A_TK = 256
A_UNROLL = 4
A_AHEAD = 3
B_TQ = 256
B_TK = 3 * B_TQ
B_AHEAD = 2
C_TQ = 128
C_HALF = 64
IN_ODD_SLAB = 256
IN_ODD_AHEAD = 2
C_AHEAD = 3
VMEM_LIMIT = 56 * 1024 * 1024

_NT = (((1,), (1,)), ((), ()))


def _resident(shape):
    zeros = (0,) * len(shape)
    return pl.BlockSpec(shape, lambda *_: zeros, pipeline_mode=pl.Buffered(1))


def _rms(x, gain):
    ms = jnp.mean(x * x, axis=-1, keepdims=True)
    return x * lax.rsqrt(ms + EPS) * gain


def _in_even_kernel(x_ref, g_ref, w_ref, gq_ref, gk_ref, cos_ref, sa_ref, sb_ref, bd_ref,
                    qa_ref, ka_ref, vat_ref, qb_ref, kb_ref, vbt_ref):
    hn = _rms(x_ref[...], g_ref[...]).astype(BF16)
    p = jnp.dot(hn, w_ref[...], preferred_element_type=F32)
    cos, sa, sb, bd = cos_ref[...], sa_ref[...], sb_ref[...], bd_ref[...]

    def headnorm_rope(c, gain):
        c2 = c * c
        hi = c2.astype(BF16)
        lo = (c2 - hi.astype(F32)).astype(BF16)
        ms = (jnp.dot(hi, bd, preferred_element_type=F32)
              + jnp.dot(lo, bd, preferred_element_type=F32))
        y = c * lax.rsqrt(ms + EPS) * gain
        return y * cos + pltpu.roll(y, 112, 1) * sa + pltpu.roll(y, 16, 1) * sb

    for j in range(A_Q // 128):
        qa_ref[:, j * 128:(j + 1) * 128] = headnorm_rope(p[:, j * 128:(j + 1) * 128], gq_ref[...]).astype(BF16)
    ka_ref[...] = headnorm_rope(p[:, A_Q:A_Q + A_KV], gk_ref[...]).astype(BF16)
    o = A_Q + 2 * A_KV
    qb_ref[...] = (p[:, o:o + B_W] * (SCALE * LOG2E)).astype(BF16)
    kb_ref[...] = p[:, o + B_W:o + 2 * B_W].astype(BF16)

    def put_transposed(dst_ref, col0, n_pairs):
        ones = jnp.ones((V_ROWS - HEAD_DIM, A_TK), BF16)
        for j in range(p.shape[0] // A_TK):
            for c in range(n_pairs):
                t = p[j * A_TK:(j + 1) * A_TK, col0 + c * 128:col0 + (c + 1) * 128].T.astype(BF16)
                for hh in range(2):
                    r0 = (2 * c + hh) * V_ROWS
                    dst_ref[j, r0:r0 + HEAD_DIM, :] = t[hh * HEAD_DIM:(hh + 1) * HEAD_DIM, :]
                    dst_ref[j, r0 + HEAD_DIM:r0 + V_ROWS, :] = ones

    put_transposed(vat_ref, A_Q + A_KV, A_KV // 128)
    put_transposed(vbt_ref, o + 2 * B_W, B_W // 128)


def _in_even(x2, seq, g, w, gq, gk, cos, sa, sb, bd):
    n = x2.shape[0]
    tm = ROW_TILE
    nt = seq // tm
    row = lambda i: (i, 0)
    pos = lambda i: (i % nt, 0)
    va_rows, vb_rows = (A_KV // HEAD_DIM) * V_ROWS, B_HEADS * V_ROWS
    out_shape = (
        jax.ShapeDtypeStruct((n, A_Q), BF16), jax.ShapeDtypeStruct((n, A_KV), BF16),
        jax.ShapeDtypeStruct((n // A_TK, va_rows, A_TK), BF16),
        jax.ShapeDtypeStruct((n, B_W), BF16), jax.ShapeDtypeStruct((n, B_W), BF16),
        jax.ShapeDtypeStruct((n // A_TK, vb_rows, A_TK), BF16))
    return pl.pallas_call(
        _in_even_kernel,
        grid=(n // tm,),
        in_specs=[pl.BlockSpec((tm, D_MODEL), row), _resident((1, D_MODEL)), _resident(w.shape),
                  _resident((1, 128)), _resident((1, 128)),
                  pl.BlockSpec((tm, 128), pos), pl.BlockSpec((tm, 128), pos), pl.BlockSpec((tm, 128), pos),
                  _resident((128, 128))],
        out_specs=(pl.BlockSpec((tm, A_Q), row), pl.BlockSpec((tm, A_KV), row),
                   pl.BlockSpec((tm // A_TK, va_rows, A_TK), lambda i: (i, 0, 0)),
                   pl.BlockSpec((tm, B_W), row), pl.BlockSpec((tm, B_W), row),
                   pl.BlockSpec((tm // A_TK, vb_rows, A_TK), lambda i: (i, 0, 0))),
        out_shape=out_shape,
        compiler_params=pltpu.CompilerParams(dimension_semantics=("parallel",), vmem_limit_bytes=VMEM_LIMIT),
        name="in_even",
    )(x2, g, w, gq, gk, cos, sa, sb, bd)


def _split_pair(qp):
    low = lax.broadcasted_iota(jnp.int32, qp.shape, 1) < HEAD_DIM
    zero = jnp.zeros(qp.shape, qp.dtype)
    return jnp.concatenate([jnp.where(low, qp, zero), jnp.where(low, zero, qp)], axis=0)


def _finish_pair(acc_lo, acc_hi):
    halves = [a[:HEAD_DIM] / a[HEAD_DIM:HEAD_DIM + 1] for a in (acc_lo, acc_hi)]
    return jnp.concatenate(halves, axis=0).T.astype(BF16)


def _mix_a_kernel(q_ref, k_ref, vt_ref, o_ref, qs_ref, m_ref, acc_ref, *, n_chunks):
    tq = A_TQ
    n_pairs = A_Q // 128
    for c in range(n_pairs):
        qs_ref[:, 2 * c * tq:(2 * c + 2) * tq] = (
            _split_pair(q_ref[:, c * 128:(c + 1) * 128]).astype(F32).T.astype(BF16))
    m_ref[...] = jnp.full(m_ref.shape, NEG, F32)
    acc_ref[...] = jnp.zeros(acc_ref.shape, F32)

    def scores(ci, c):
        k = k_ref[pl.ds(pl.multiple_of(ci * A_TK, A_TK), A_TK), :]
        return jnp.dot(k, qs_ref[:, 2 * c * tq:(2 * c + 2) * tq], preferred_element_type=F32)

    def chunks(it, carry):
        items = [(it * A_UNROLL + u, c) for u in range(A_UNROLL) for c in range(n_pairs)]
        ahead = [scores(*item) for item in items[:A_AHEAD]]
        for t, (ci, c) in enumerate(items):
            s = ahead.pop(0)
            if t + A_AHEAD < len(items):
                ahead.append(scores(*items[t + A_AHEAD]))
            cols = slice(2 * c * tq, (2 * c + 2) * tq)
            m_old = m_ref[:, cols]
            m_new = jnp.maximum(m_old, jnp.max(s, axis=0, keepdims=True))
            alpha = jnp.exp2(m_old - m_new)
            p = jnp.exp2((s - m_new).astype(BF16))
            m_ref[:, cols] = m_new
            for g in range(2):
                vt = vt_ref[ci, g * V_ROWS:(g + 1) * V_ROWS, :]
                pv = jnp.dot(vt, p[:, g * tq:(g + 1) * tq], preferred_element_type=F32)
                acc_ref[2 * c + g] = alpha[:, g * tq:(g + 1) * tq] * acc_ref[2 * c + g] + pv
        return carry

    lax.fori_loop(0, n_chunks // A_UNROLL, chunks, 0)

    for c in range(n_pairs):
        o_ref[:, c * 128:(c + 1) * 128] = _finish_pair(acc_ref[2 * c], acc_ref[2 * c + 1])


def _mix_a(qa, ka, vat, bn, seq):
    n = qa.shape[0]
    nq = seq // A_TQ
    nc = seq // A_TK
    return pl.pallas_call(
        functools.partial(_mix_a_kernel, n_chunks=nc),
        grid=(bn, nq),
        in_specs=[pl.BlockSpec((A_TQ, A_Q), lambda b, i: (b * nq + i, 0)),
                  pl.BlockSpec((seq, A_KV), lambda b, i: (b, 0)),
                  pl.BlockSpec((nc, vat.shape[1], A_TK), lambda b, i: (b, 0, 0))],
        out_specs=pl.BlockSpec((A_TQ, A_Q), lambda b, i: (b * nq + i, 0)),
        out_shape=jax.ShapeDtypeStruct((n, A_Q), BF16),
        scratch_shapes=[pltpu.VMEM((128, 8 * A_TQ), BF16), pltpu.VMEM((1, 8 * A_TQ), F32),
                        pltpu.VMEM((8, V_ROWS, A_TQ), F32)],
        compiler_params=pltpu.CompilerParams(dimension_semantics=("parallel", "parallel"),
                                             vmem_limit_bytes=VMEM_LIMIT),
        name="mix_a",
    )(qa, ka, vat)


def _mix_b_kernel(q_ref, k0_ref, k1_ref, k2_ref, vt0_ref, vt1_ref, vt2_ref, bias_ref, o_ref):
    tq = B_TQ
    n_pairs = B_W // 128

    k_refs = (k0_ref, k1_ref, k2_ref)
    vt_refs = (vt0_ref, vt1_ref, vt2_ref)

    def scores(c):
        cs = slice(c * 128, (c + 1) * 128)
        qs = _split_pair(q_ref[:, cs])
        return [lax.dot_general(k_ref[:, cs], qs, _NT, preferred_element_type=F32) for k_ref in k_refs]

    ahead = [scores(c) for c in range(B_AHEAD)]
    for c in range(n_pairs):
        cs = slice(c * 128, (c + 1) * 128)
        s = [sj + bias_ref[0, c, j * tq:(j + 1) * tq, :] for j, sj in enumerate(ahead.pop(0))]
        if c + B_AHEAD < n_pairs:
            ahead.append(scores(c + B_AHEAD))
        m = jnp.max(jnp.maximum(jnp.maximum(s[0], s[1]), s[2]), axis=0, keepdims=True)
        p = [jnp.exp2((sj - m).astype(BF16)) for sj in s]
        accs = []
        for hh in range(2):
            rows = slice((2 * c + hh) * V_ROWS, (2 * c + hh + 1) * V_ROWS)
            parts = [jnp.dot(vt_ref[rows, :], pj[:, hh * tq:(hh + 1) * tq], preferred_element_type=F32)
                     for vt_ref, pj in zip(vt_refs, p)]
            accs.append(parts[0] + parts[1] + parts[2])
        o_ref[:, cs] = _finish_pair(*accs)


def _mix_b(qb, kb, vbt, bias, bn, seq):
    n = qb.shape[0]
    nb = seq // B_TQ
    q_map = lambda b, i: (b * nb + i, 0)

    def kv_map(j):
        return lambda b, i: (b * nb + jnp.clip(i - 1, 0, nb - 3) + j, 0)

    def vt_map(j):
        return lambda b, i: (b * nb + jnp.clip(i - 1, 0, nb - 3) + j, 0, 0)

    def bias_map(b, i):
        return (jnp.where(i == 0, 0, jnp.where(i == nb - 1, 2, 1)), 0, 0, 0)

    blk = lambda m: pl.BlockSpec((B_TQ, B_W), m)
    vblk = lambda m: pl.BlockSpec((None, vbt.shape[1], B_TQ), m)
    return pl.pallas_call(
        _mix_b_kernel,
        grid=(bn, nb),
        in_specs=[blk(q_map), blk(kv_map(0)), blk(kv_map(1)), blk(kv_map(2)),
                  vblk(vt_map(0)), vblk(vt_map(1)), vblk(vt_map(2)),
                  pl.BlockSpec((1, B_W // 128, B_TK, 2 * B_TQ), bias_map)],
        out_specs=blk(q_map),
        out_shape=jax.ShapeDtypeStruct((n, B_W), BF16),
        compiler_params=pltpu.CompilerParams(dimension_semantics=("parallel", "parallel"),
                                             vmem_limit_bytes=VMEM_LIMIT),
        name="mix_b",
    )(qb, kb, kb, kb, vbt, vbt, vbt, bias)


def _neighbourhood_bias(rpb, rows):
    n_dc = 2 * NA_COLS - 1
    c = np.arange(GRID_W)
    cs = np.clip(c - NA_COLS // 2, 0, GRID_W - NA_COLS)
    col_ok = (c[:, None] >= cs[None, :]) & (c[:, None] < cs[None, :] + NA_COLS)
    dc = np.clip(c[:, None] - c[None, :] + NA_COLS - 1, 0, n_dc - 1)
    pick = (dc.reshape(-1)[None, :] == np.arange(n_dc)[:, None]).astype(np.float32)
    by_col = jnp.einsum("hrd,dx->hrx", rpb.astype(F32) * LOG2E, jnp.asarray(pick),
                        precision=lax.Precision.HIGHEST)
    by_col = by_col.reshape(B_HEADS, 2 * NA_ROWS - 1, GRID_W, GRID_W)
    by_col = jnp.where(col_ok[None, None], by_col, NEG)
    masked = jnp.full((B_HEADS, GRID_W, GRID_W), NEG, F32)
    tables = []
    for r0, ws in ((0, 0), (4, 0), (rows - 4, rows - 12)):
        key_rows = []
        for kr in range(B_TK // GRID_W):
            blocks = []
            for rr in range(B_TQ // GRID_W):
                r = r0 + rr
                rs = min(max(r - NA_ROWS // 2, 0), rows - NA_ROWS)
                ok = rs <= ws + kr < rs + NA_ROWS
                blocks.append(by_col[:, ws + kr - r + NA_ROWS - 1] if ok else masked)
            key_rows.append(jnp.concatenate(blocks, axis=-1))
        t = jnp.concatenate(key_rows, axis=-2)
        t = t.reshape(B_HEADS // 2, 2, B_TK, B_TQ).transpose(0, 2, 1, 3).reshape(B_HEADS // 2, B_TK, 2 * B_TQ)
        tables.append(t)
    return jnp.stack(tables, axis=0)


def _ffn(x1, gain, wg_ref, wu_ref, wd_ref):
    hn = _rms(x1, gain).astype(BF16)
    acc = None
    for c in range(D_FF // FF_CHUNK):
        cs = slice(c * FF_CHUNK, (c + 1) * FF_CHUNK)
        gate = jnp.dot(hn, wg_ref[:, cs], preferred_element_type=F32)
        up = jnp.dot(hn, wu_ref[:, cs], preferred_element_type=F32)
        act = (gate * (1.0 / (1.0 + jnp.exp(-gate))) * up).astype(BF16)
        d = jnp.dot(act, wd_ref[cs, :], preferred_element_type=F32)
        acc = d if acc is None else acc + d
    return x1 + acc


def _post_even_kernel(x_ref, oa_ref, ob_ref, woa_ref, wob_ref, g_ref, wg_ref, wu_ref, wd_ref, y_ref):
    mix = (jnp.dot(oa_ref[...], woa_ref[...], preferred_element_type=F32)
           + jnp.dot(ob_ref[...], wob_ref[...], preferred_element_type=F32))
    y_ref[...] = _ffn(x_ref[...] + mix, g_ref[...], wg_ref, wu_ref, wd_ref)


def _post_even(x2, oa, ob, woa, wob, g, wg, wu, wd):
    n = x2.shape[0]
    tm = ROW_TILE
    row = lambda i: (i, 0)
    return pl.pallas_call(
        _post_even_kernel,
        grid=(n // tm,),
        in_specs=[pl.BlockSpec((tm, D_MODEL), row), pl.BlockSpec((tm, A_Q), row), pl.BlockSpec((tm, B_W), row),
                  _resident(woa.shape), _resident(wob.shape), _resident((1, D_MODEL)),
                  _resident(wg.shape), _resident(wu.shape), _resident(wd.shape)],
        out_specs=pl.BlockSpec((tm, D_MODEL), row),
        out_shape=jax.ShapeDtypeStruct((n, D_MODEL), F32),
        compiler_params=pltpu.CompilerParams(dimension_semantics=("parallel",), vmem_limit_bytes=VMEM_LIMIT),
        name="post_even",
    )(x2, oa, ob, woa, wob, g, wg, wu, wd)


def _post_odd_kernel(x_ref, o1_ref, of2_ref, of3_ref, l1_ref, l2_ref, l3_ref, ex_ref, wo_ref, g_ref,
                     wg_ref, wu_ref, wd_ref, gf_ref, y_ref, o2_ref, o3_ref):
    tm = x_ref.shape[0]
    n_cb = C_W // 128
    for (_, dil), src, dst in zip(C_GROUPS[1:], (of2_ref, of3_ref), (o2_ref, o3_ref)):
        for rho in range(dil):
            for cb in range(n_cb):
                dst[cb, pl.ds(rho, tm // dil, stride=dil), :] = src[0, rho, :, cb * 128:(cb + 1) * 128].astype(F32)
    o2 = jnp.concatenate([o2_ref[cb] for cb in range(n_cb)], axis=1)
    o3 = jnp.concatenate([o3_ref[cb] for cb in range(n_cb)], axis=1)
    l1, l2, l3 = l1_ref[...], l2_ref[...], l3_ref[...]
    mx = jnp.maximum(jnp.maximum(l1, l2), l3)
    e1, e2, e3 = jnp.exp2(l1 - mx), jnp.exp2(l2 - mx), jnp.exp2(l3 - mx)
    den = e1 + e2 + e3
    ex = ex_ref[...]

    def widen(w):
        hi = w.astype(BF16)
        lo = (w - hi.astype(F32)).astype(BF16)
        return jnp.dot(hi, ex, preferred_element_type=F32) + jnp.dot(lo, ex, preferred_element_type=F32)

    o = widen(e1 / den) * o1_ref[...].astype(F32) + widen(e2 / den) * o2 + widen(e3 / den) * o3
    mix = jnp.dot(o.astype(BF16), wo_ref[...], preferred_element_type=F32)
    y = _ffn(x_ref[...] + mix, g_ref[...], wg_ref, wu_ref, wd_ref)
    y_ref[...] = _rms(y, gf_ref[...])


def _post_odd(x2, seq, os_, ls_, ex, wo, g, wg, wu, wd, gf):
    n = x2.shape[0]
    tm = ROW_TILE
    nt = seq // tm
    row = lambda i: (i, 0)
    wide = pl.BlockSpec((tm, D_MODEL), row)
    narrow = pl.BlockSpec((tm, C_HEADS), row)
    folded = [pl.BlockSpec((1, dil, tm // dil, C_W), lambda i: (i // nt, 0, i % nt, 0)) for _, dil in C_GROUPS[1:]]
    os_ = [os_[0].reshape(n, C_W)] + [o.reshape(n // seq, dil, seq // dil, C_W)
                                      for o, (_, dil) in zip(os_[1:], C_GROUPS[1:])]
    return pl.pallas_call(
        _post_odd_kernel,
        grid=(n // tm,),
        in_specs=[wide, wide, *folded, narrow, narrow, narrow, _resident(ex.shape), _resident(wo.shape),
                  _resident((1, D_MODEL)), _resident(wg.shape), _resident(wu.shape), _resident(wd.shape),
                  _resident((1, D_MODEL))],
        out_specs=wide,
        out_shape=jax.ShapeDtypeStruct((n, D_MODEL), F32),
        scratch_shapes=[pltpu.VMEM((C_W // 128, tm, 128), F32), pltpu.VMEM((C_W // 128, tm, 128), F32)],
        compiler_params=pltpu.CompilerParams(dimension_semantics=("parallel",), vmem_limit_bytes=VMEM_LIMIT),
        name="post_odd",
    )(x2, *os_, *ls_, ex, wo, g, wg, wu, wd, gf)


def _in_odd_kernel(x_ref, g_ref, w_ref, *refs):
    n_slabs = 3 * C_W // IN_ODD_SLAB
    out_refs, p_refs, f_refs = refs[:-4], refs[-4:-2], refs[-2:]
    tm = x_ref.shape[0]
    assert [d for _, d in C_GROUPS] == [1, 4, 16]
    hn = _rms(x_ref[...], g_ref[...]).astype(BF16)

    def project(slab):
        cols = slice(slab * IN_ODD_SLAB, (slab + 1) * IN_ODD_SLAB)
        return jnp.dot(hn, w_ref[:, cols], preferred_element_type=F32)

    ahead = [project(s) for s in range(IN_ODD_AHEAD)]
    for slab in range(n_slabs):
        p = ahead.pop(0)
        if slab + IN_ODD_AHEAD < n_slabs:
            ahead.append(project(slab + IN_ODD_AHEAD))
        part, col0 = divmod(slab * IN_ODD_SLAB, C_W)
        if part == 0:
            p = p * (SCALE * LOG2E)
        p_ref, f_ref = p_refs[slab % 2], f_refs[slab % 2]
        dst1, dst4, dst16 = out_refs[part * 3:part * 3 + 3]
        dst1[0, 0, :, col0:col0 + IN_ODD_SLAB] = p.astype(BF16)
        for cb in range(IN_ODD_SLAB // 128):
            cols = slice(col0 + cb * 128, col0 + (cb + 1) * 128)
            p_ref[cb] = p[:, cb * 128:(cb + 1) * 128]
            for r4 in range(4):
                by4 = p_ref[cb, pl.ds(r4, tm // 4, stride=4), :]
                dst4[0, r4, :, cols] = by4.astype(BF16)
                f_ref[cb, r4] = by4
                for j in range(4):
                    by16 = f_ref[cb, r4, pl.ds(j, tm // 16, stride=4), :]
                    dst16[0, r4 + 4 * j, :, cols] = by16.astype(BF16)


def _in_odd(x2, bn, seq, g, w):
    n = x2.shape[0]
    tm = ROW_TILE
    nt = seq // tm
    specs, shapes = [], []
    for _ in range(3):
        for _, dil in C_GROUPS:
            specs.append(pl.BlockSpec((1, dil, tm // dil, C_W), lambda i: (i // nt, 0, i % nt, 0)))
            shapes.append(jax.ShapeDtypeStruct((bn, dil, seq // dil, C_W), BF16))
    outs = pl.pallas_call(
        _in_odd_kernel,
        grid=(n // tm,),
        in_specs=[pl.BlockSpec((tm, D_MODEL), lambda i: (i, 0)), _resident((1, D_MODEL)), _resident(w.shape)],
        out_specs=tuple(specs),
        out_shape=tuple(shapes),
        scratch_shapes=[pltpu.VMEM((IN_ODD_SLAB // 128, tm, 128), F32)] * 2
        + [pltpu.VMEM((IN_ODD_SLAB // 128, 4, tm // 4, 128), F32)] * 2,
        compiler_params=pltpu.CompilerParams(dimension_semantics=("parallel",), vmem_limit_bytes=VMEM_LIMIT),
        name="in_odd",
    )(x2, g, w)
    outs = [a.reshape(a.shape[0] * a.shape[1], a.shape[2], C_W) for a in outs]
    n_g = len(C_GROUPS)
    return outs[:n_g], outs[n_g:2 * n_g], outs[2 * n_g:]


def _band_kernel(sl_ref, q_ref, kp_ref, kc_ref, kn_ref, vp_ref, vc_ref, vn_ref, o_ref, lse_ref,
                 kwin_ref, vt_ref, pen_ref, *, length, tile):
    t = pl.program_id(1)
    n_pairs = C_W // 128
    win = C_TQ + 2 * C_HALF
    span = tile + 2 * C_HALF
    kwin_ref[0:C_HALF, :] = kp_ref[C_TQ - C_HALF:C_TQ, :]
    kwin_ref[C_HALF:C_HALF + tile, :] = kc_ref[...]
    kwin_ref[C_HALF + tile:span, :] = kn_ref[0:C_HALF, :]

    key = lax.broadcasted_iota(jnp.int32, (win, 2 * C_TQ), 0)
    lane = lax.broadcasted_iota(jnp.int32, (win, 2 * C_TQ), 1)
    dist = jnp.abs(key - C_HALF - (lane & (C_TQ - 1)))
    distf = dist.astype(F32)
    ones = jnp.ones((V_ROWS - HEAD_DIM, span), BF16)
    for c in range(n_pairs):
        cs = slice(c * 128, (c + 1) * 128)
        vwin = jnp.concatenate([vp_ref[C_TQ - C_HALF:C_TQ, cs], vc_ref[:, cs], vn_ref[0:C_HALF, cs]], axis=0)
        vt = vwin.astype(F32).T.astype(BF16)
        for hh in range(2):
            vt_ref[c, hh * V_ROWS:hh * V_ROWS + HEAD_DIM, :] = vt[hh * HEAD_DIM:(hh + 1) * HEAD_DIM, :]
            vt_ref[c, hh * V_ROWS + HEAD_DIM:(hh + 1) * V_ROWS, :] = ones
        slope = jnp.where(lane < C_TQ, sl_ref[2 * c], sl_ref[2 * c + 1])
        pen_ref[c] = jnp.where(dist <= C_HALF, slope * distf, -NEG)

    def scores(j, c):
        cs = slice(c * 128, (c + 1) * 128)
        qs = _split_pair(q_ref[j * C_TQ:(j + 1) * C_TQ, cs])
        k = kwin_ref[j * C_TQ:j * C_TQ + win, cs]
        return lax.dot_general(k, qs, _NT, preferred_element_type=F32)

    n_blocks = tile // C_TQ
    first_key = t * tile - C_HALF + key
    edge = {0: jnp.where(first_key >= 0, 0.0, -NEG),
            n_blocks - 1: jnp.where(first_key + (n_blocks - 1) * C_TQ < length, 0.0, -NEG)}
    items = [(j, c) for j in range(n_blocks) for c in range(n_pairs)]
    ahead = [scores(*item) for item in items[:C_AHEAD]]
    for n_item, (j, c) in enumerate(items):
        rows = slice(j * C_TQ, (j + 1) * C_TQ)
        cs = slice(c * 128, (c + 1) * 128)
        s = ahead.pop(0) - pen_ref[c]
        if n_item + C_AHEAD < len(items):
            ahead.append(scores(*items[n_item + C_AHEAD]))
        if j in edge:
            s = s - edge[j]
        m = jnp.max(s, axis=0, keepdims=True)
        p = jnp.exp2((s - m).astype(BF16))
        acc = jnp.dot(vt_ref[c, :, j * C_TQ:j * C_TQ + win], p, preferred_element_type=F32)
        a0, a1 = acc[:V_ROWS, :C_TQ], acc[V_ROWS:, C_TQ:]
        o_ref[rows, cs] = _finish_pair(a0, a1)
        lse_ref[2 * c:2 * c + 1, rows] = m[:, :C_TQ] + jnp.log2(a0[HEAD_DIM:HEAD_DIM + 1])
        lse_ref[2 * c + 1:2 * c + 2, rows] = m[:, C_TQ:] + jnp.log2(a1[HEAD_DIM:HEAD_DIM + 1])


def _band(slopes, q, k, v):
    nb_, length, _ = q.shape
    tile = 512 if length % 512 == 0 else 256
    per = tile // C_TQ
    last = length // C_TQ - 1
    cur = lambda n, t: (n, t, 0)
    prev = lambda n, t: (n, jnp.maximum(t * per - 1, 0), 0)
    nxt = lambda n, t: (n, jnp.minimum((t + 1) * per, last), 0)
    big = lambda m: pl.BlockSpec((None, tile, C_W), m)
    halo = lambda m: pl.BlockSpec((None, C_TQ, C_W), m)
    span = tile + 2 * C_HALF
    return pl.pallas_call(
        functools.partial(_band_kernel, length=length, tile=tile),
        grid=(nb_, length // tile),
        in_specs=[pl.BlockSpec(memory_space=pltpu.SMEM), big(cur),
                  halo(prev), big(cur), halo(nxt), halo(prev), big(cur), halo(nxt)],
        out_specs=(big(cur), pl.BlockSpec((None, C_HEADS, tile), lambda n, t: (n, 0, t))),
        out_shape=(jax.ShapeDtypeStruct((nb_, length, C_W), BF16),
                   jax.ShapeDtypeStruct((nb_, C_HEADS, length), F32)),
        scratch_shapes=[pltpu.VMEM((span, C_W), BF16), pltpu.VMEM((C_W // 128, 2 * V_ROWS, span), BF16),
                        pltpu.VMEM((C_W // 128, C_TQ + 2 * C_HALF, 2 * C_TQ), F32)],
        compiler_params=pltpu.CompilerParams(dimension_semantics=("parallel", "parallel"),
                                             vmem_limit_bytes=VMEM_LIMIT),
        name="band",
    )(slopes, q, k, k, k, v, v, v)


def _rope_tables(seq):
    t = jnp.arange(seq)
    n = HEAD_DIM // 4
    freqs = jnp.power(ROPE_THETA, -jnp.arange(n, dtype=F32) / n)
    ang_r = (t // GRID_W).astype(F32)[:, None] * freqs[None, :]
    ang_c = (t % GRID_W).astype(F32)[:, None] * freqs[None, :]
    zero = jnp.zeros_like(ang_r)
    cos = jnp.concatenate([jnp.cos(ang_r)] * 2 + [jnp.cos(ang_c)] * 2, axis=-1)
    sin_r, sin_c = jnp.sin(ang_r), jnp.sin(ang_c)
    sa = jnp.concatenate([-sin_r, zero, -sin_c, zero], axis=-1)
    sb = jnp.concatenate([zero, sin_r, zero, sin_c], axis=-1)
    two = lambda a: jnp.concatenate([a, a], axis=-1)
    return two(cos), two(sa), two(sb)


def _trunk(x, prm):
    bn, seq, _ = x.shape
    n = bn * seq
    x2 = x.reshape(n, D_MODEL)
    cos, sa, sb = _rope_tables(seq)
    qa, ka, vat, qb, kb, vbt = _in_even(x2, seq, prm["g_mix0"], prm["w_in_even"], prm["gq"], prm["gk"],
                                        cos, sa, sb, prm["bd"])
    oa = _mix_a(qa, ka, vat, bn, seq)
    ob = _mix_b(qb, kb, vbt, _neighbourhood_bias(prm["rpb"], seq // GRID_W), bn, seq)
    x2 = _post_even(x2, oa, ob, prm["wo_a"], prm["wo_b"], prm["g_ffn0"], prm["wg0"], prm["wu0"], prm["wd0"])

    qs, ks, vs = _in_odd(x2, bn, seq, prm["g_mix1"], prm["w_in_odd"])
    outs, lses = [], []
    for (_, dil), q, k, v in zip(C_GROUPS, qs, ks, vs):
        o, lse = _band(prm["slopes"] * dil * LOG2E, q, k, v)
        outs.append(o)
        lses.append(lse.reshape(bn, dil, C_HEADS, seq // dil).transpose(0, 3, 1, 2).reshape(n, C_HEADS))
    y = _post_odd(x2, seq, outs, lses, prm["expand"], prm["wo_odd"], prm["g_ffn1"], prm["wg1"], prm["wu1"],
                  prm["wd1"], prm["g_final"])
    return y.reshape(bn, seq, D_MODEL)


def kernel(x_prompt, x_sample, norm_mix, norm_ffn, norm_final, w_in_even, a_q_norm, a_k_norm, na_rpb,
           w_out_even, w_in_odd, w_out_odd, w_gate_up, w_down):
    head = np.arange(128) // HEAD_DIM
    order = np.array([0, 4, 1, 5, 2, 6, 3, 7])
    w_in0 = w_in_even[0]
    w_qa = w_in0[:, :A_Q].reshape(D_MODEL, A_Q // HEAD_DIM, HEAD_DIM)[:, order].reshape(D_MODEL, A_Q)
    w_in0 = jnp.concatenate([w_qa, w_in0[:, A_Q:]], axis=1)
    wo_a = w_out_even[0, :A_Q].reshape(A_Q // HEAD_DIM, HEAD_DIM, D_MODEL)[order].reshape(A_Q, D_MODEL)
    prm = {
        "g_mix0": norm_mix[0][None].astype(F32), "g_mix1": norm_mix[1][None].astype(F32),
        "g_ffn0": norm_ffn[0][None].astype(F32), "g_ffn1": norm_ffn[1][None].astype(F32),
        "g_final": norm_final[None].astype(F32),
        "w_in_even": w_in0.astype(BF16), "w_in_odd": w_in_odd[0].astype(BF16),
        "gq": jnp.tile(a_q_norm[0].astype(F32) * (SCALE * LOG2E), 2)[None], "gk": jnp.tile(a_k_norm[0].astype(F32), 2)[None],
        "bd": jnp.asarray((head[:, None] == head[None, :]) / HEAD_DIM, BF16),
        "rpb": na_rpb[0],
        "wo_a": wo_a.astype(BF16), "wo_b": w_out_even[0, A_Q:].astype(BF16),
        "wo_odd": w_out_odd[0].astype(BF16),
        "wg0": w_gate_up[0, :, :D_FF].astype(BF16), "wu0": w_gate_up[0, :, D_FF:].astype(BF16),
        "wg1": w_gate_up[1, :, :D_FF].astype(BF16), "wu1": w_gate_up[1, :, D_FF:].astype(BF16),
        "wd0": w_down[0].astype(BF16), "wd1": w_down[1].astype(BF16),
        "slopes": jnp.exp2(-8.0 * (jnp.arange(C_HEADS, dtype=F32) + 1.0) / C_HEADS),
        "expand": jnp.asarray(np.arange(C_HEADS)[:, None] == (np.arange(C_W) // HEAD_DIM)[None, :], BF16),
    }
    return (_trunk(x_prompt, prm), _trunk(x_sample, prm))
```

```python
import functools

import jax
import jax.numpy as jnp
import numpy as np
from jax import lax
from jax.experimental import pallas as pl
from jax.experimental.pallas import tpu as pltpu

F32 = jnp.float32
BF16 = jnp.bfloat16

D_MODEL = 1024
HEAD_DIM = 64
A_Q = 512
A_KV = 128
B_W = 512
C_W = 1024
C_HEADS = 16
B_HEADS = 8
GRID_W = 64
NA_ROWS = 8
NA_COLS = 16
C_GROUPS = ((128, 1), (512, 4), (2048, 16))
ROPE_THETA = 10000.0
D_FF = 2816
FF_CHUNK = 1408
EPS = 1e-6
NEG = -1e30
SCALE = HEAD_DIM ** -0.5
LOG2E = 1.4426950408889634

V_ROWS = HEAD_DIM + 16
ROW_TILE = 512
A_TQ = 256
A_TK = 256
A_UNROLL = 4
A_AHEAD = 3
B_TQ = 256
B_TK = 3 * B_TQ
B_AHEAD = 2
C_TQ = 128
C_HALF = 64
IN_ODD_SLAB = 256
IN_ODD_AHEAD = 2
C_AHEAD = 3
VMEM_LIMIT = 56 * 1024 * 1024

_NT = (((1,), (1,)), ((), ()))


def _resident(shape):
    zeros = (0,) * len(shape)
    return pl.BlockSpec(shape, lambda *_: zeros, pipeline_mode=pl.Buffered(1))


def _rms(x, gain):
    ms = jnp.mean(x * x, axis=-1, keepdims=True)
    return x * lax.rsqrt(ms + EPS) * gain


def _in_even_kernel(x_ref, g_ref, w_ref, gq_ref, gk_ref, cos_ref, sa_ref, sb_ref, bd_ref,
                    qa_ref, ka_ref, vat_ref, qb_ref, kb_ref, vbt_ref):
    hn = _rms(x_ref[...], g_ref[...]).astype(BF16)
    p = jnp.dot(hn, w_ref[...], preferred_element_type=F32)
    cos, sa, sb, bd = cos_ref[...], sa_ref[...], sb_ref[...], bd_ref[...]

    def headnorm_rope(c, gain):
        c2 = c * c
        hi = c2.astype(BF16)
        lo = (c2 - hi.astype(F32)).astype(BF16)
        ms = (jnp.dot(hi, bd, preferred_element_type=F32)
              + jnp.dot(lo, bd, preferred_element_type=F32))
        y = c * lax.rsqrt(ms + EPS) * gain
        return y * cos + pltpu.roll(y, 112, 1) * sa + pltpu.roll(y, 16, 1) * sb

    for j in range(A_Q // 128):
        qa_ref[:, j * 128:(j + 1) * 128] = headnorm_rope(p[:, j * 128:(j + 1) * 128], gq_ref[...]).astype(BF16)
    ka_ref[...] = headnorm_rope(p[:, A_Q:A_Q + A_KV], gk_ref[...]).astype(BF16)
    o = A_Q + 2 * A_KV
    qb_ref[...] = (p[:, o:o + B_W] * (SCALE * LOG2E)).astype(BF16)
    kb_ref[...] = p[:, o + B_W:o + 2 * B_W].astype(BF16)

    def put_transposed(dst_ref, col0, n_pairs):
        ones = jnp.ones((V_ROWS - HEAD_DIM, A_TK), BF16)
        for j in range(p.shape[0] // A_TK):
            for c in range(n_pairs):
                t = p[j * A_TK:(j + 1) * A_TK, col0 + c * 128:col0 + (c + 1) * 128].T.astype(BF16)
                for hh in range(2):
                    r0 = (2 * c + hh) * V_ROWS
                    dst_ref[j, r0:r0 + HEAD_DIM, :] = t[hh * HEAD_DIM:(hh + 1) * HEAD_DIM, :]
                    dst_ref[j, r0 + HEAD_DIM:r0 + V_ROWS, :] = ones

    put_transposed(vat_ref, A_Q + A_KV, A_KV // 128)
    put_transposed(vbt_ref, o + 2 * B_W, B_W // 128)


def _in_even(x2, seq, g, w, gq, gk, cos, sa, sb, bd):
    n = x2.shape[0]
    tm = ROW_TILE
    nt = seq // tm
    row = lambda i: (i, 0)
    pos = lambda i: (i % nt, 0)
    va_rows, vb_rows = (A_KV // HEAD_DIM) * V_ROWS, B_HEADS * V_ROWS
    out_shape = (
        jax.ShapeDtypeStruct((n, A_Q), BF16), jax.ShapeDtypeStruct((n, A_KV), BF16),
        jax.ShapeDtypeStruct((n // A_TK, va_rows, A_TK), BF16),
        jax.ShapeDtypeStruct((n, B_W), BF16), jax.ShapeDtypeStruct((n, B_W), BF16),
        jax.ShapeDtypeStruct((n // A_TK, vb_rows, A_TK), BF16))
    return pl.pallas_call(
        _in_even_kernel,
        grid=(n // tm,),
        in_specs=[pl.BlockSpec((tm, D_MODEL), row), _resident((1, D_MODEL)), _resident(w.shape),
                  _resident((1, 128)), _resident((1, 128)),
                  pl.BlockSpec((tm, 128), pos), pl.BlockSpec((tm, 128), pos), pl.BlockSpec((tm, 128), pos),
                  _resident((128, 128))],
        out_specs=(pl.BlockSpec((tm, A_Q), row), pl.BlockSpec((tm, A_KV), row),
                   pl.BlockSpec((tm // A_TK, va_rows, A_TK), lambda i: (i, 0, 0)),
                   pl.BlockSpec((tm, B_W), row), pl.BlockSpec((tm, B_W), row),
                   pl.BlockSpec((tm // A_TK, vb_rows, A_TK), lambda i: (i, 0, 0))),
        out_shape=out_shape,
        compiler_params=pltpu.CompilerParams(dimension_semantics=("parallel",), vmem_limit_bytes=VMEM_LIMIT),
        name="in_even",
    )(x2, g, w, gq, gk, cos, sa, sb, bd)


def _split_pair(qp):
    low = lax.broadcasted_iota(jnp.int32, qp.shape, 1) < HEAD_DIM
    zero = jnp.zeros(qp.shape, qp.dtype)
    return jnp.concatenate([jnp.where(low, qp, zero), jnp.where(low, zero, qp)], axis=0)


def _finish_pair(acc_lo, acc_hi):
    halves = [a[:HEAD_DIM] / a[HEAD_DIM:HEAD_DIM + 1] for a in (acc_lo, acc_hi)]
    return jnp.concatenate(halves, axis=0).T.astype(BF16)


def _mix_a_kernel(q_ref, k_ref, vt_ref, o_ref, qs_ref, m_ref, acc_ref, *, n_chunks):
    tq = A_TQ
    n_pairs = A_Q // 128
    for c in range(n_pairs):
        qs_ref[:, 2 * c * tq:(2 * c + 2) * tq] = (
            _split_pair(q_ref[:, c * 128:(c + 1) * 128]).astype(F32).T.astype(BF16))
    m_ref[...] = jnp.full(m_ref.shape, NEG, F32)
    acc_ref[...] = jnp.zeros(acc_ref.shape, F32)

    def scores(ci, c):
        k = k_ref[pl.ds(pl.multiple_of(ci * A_TK, A_TK), A_TK), :]
        return jnp.dot(k, qs_ref[:, 2 * c * tq:(2 * c + 2) * tq], preferred_element_type=F32)

    def chunks(it, carry):
        items = [(it * A_UNROLL + u, c) for u in range(A_UNROLL) for c in range(n_pairs)]
        ahead = [scores(*item) for item in items[:A_AHEAD]]
        for t, (ci, c) in enumerate(items):
            s = ahead.pop(0)
            if t + A_AHEAD < len(items):
                ahead.append(scores(*items[t + A_AHEAD]))
            cols = slice(2 * c * tq, (2 * c + 2) * tq)
            m_old = m_ref[:, cols]
            m_new = jnp.maximum(m_old, jnp.max(s, axis=0, keepdims=True))
            alpha = jnp.exp2(m_old - m_new)
            p = jnp.exp2(s - m_new).astype(BF16)
            m_ref[:, cols] = m_new
            for g in range(2):
                vt = vt_ref[ci, g * V_ROWS:(g + 1) * V_ROWS, :]
                pv = jnp.dot(vt, p[:, g * tq:(g + 1) * tq], preferred_element_type=F32)
                acc_ref[2 * c + g] = alpha[:, g * tq:(g + 1) * tq] * acc_ref[2 * c + g] + pv
        return carry

    lax.fori_loop(0, n_chunks // A_UNROLL, chunks, 0)

    for c in range(n_pairs):
        o_ref[:, c * 128:(c + 1) * 128] = _finish_pair(acc_ref[2 * c], acc_ref[2 * c + 1])


def _mix_a(qa, ka, vat, bn, seq):
    n = qa.shape[0]
    nq = seq // A_TQ
    nc = seq // A_TK
    return pl.pallas_call(
        functools.partial(_mix_a_kernel, n_chunks=nc),
        grid=(bn, nq),
        in_specs=[pl.BlockSpec((A_TQ, A_Q), lambda b, i: (b * nq + i, 0)),
                  pl.BlockSpec((seq, A_KV), lambda b, i: (b, 0)),
                  pl.BlockSpec((nc, vat.shape[1], A_TK), lambda b, i: (b, 0, 0))],
        out_specs=pl.BlockSpec((A_TQ, A_Q), lambda b, i: (b * nq + i, 0)),
        out_shape=jax.ShapeDtypeStruct((n, A_Q), BF16),
        scratch_shapes=[pltpu.VMEM((128, 8 * A_TQ), BF16), pltpu.VMEM((1, 8 * A_TQ), F32),
                        pltpu.VMEM((8, V_ROWS, A_TQ), F32)],
        compiler_params=pltpu.CompilerParams(dimension_semantics=("parallel", "parallel"),
                                             vmem_limit_bytes=VMEM_LIMIT),
        name="mix_a",
    )(qa, ka, vat)


def _mix_b_kernel(q_ref, k0_ref, k1_ref, k2_ref, vt0_ref, vt1_ref, vt2_ref, bias_ref, o_ref):
    tq = B_TQ
    n_pairs = B_W // 128

    k_refs = (k0_ref, k1_ref, k2_ref)
    vt_refs = (vt0_ref, vt1_ref, vt2_ref)

    def scores(c):
        cs = slice(c * 128, (c + 1) * 128)
        qs = _split_pair(q_ref[:, cs])
        return [lax.dot_general(k_ref[:, cs], qs, _NT, preferred_element_type=F32) for k_ref in k_refs]

    ahead = [scores(c) for c in range(B_AHEAD)]
    for c in range(n_pairs):
        cs = slice(c * 128, (c + 1) * 128)
        s = [sj + bias_ref[0, c, j * tq:(j + 1) * tq, :] for j, sj in enumerate(ahead.pop(0))]
        if c + B_AHEAD < n_pairs:
            ahead.append(scores(c + B_AHEAD))
        m = jnp.max(jnp.maximum(jnp.maximum(s[0], s[1]), s[2]), axis=0, keepdims=True)
        p = [jnp.exp2(sj - m).astype(BF16) for sj in s]
        accs = []
        for hh in range(2):
            rows = slice((2 * c + hh) * V_ROWS, (2 * c + hh + 1) * V_ROWS)
            parts = [jnp.dot(vt_ref[rows, :], pj[:, hh * tq:(hh + 1) * tq], preferred_element_type=F32)
                     for vt_ref, pj in zip(vt_refs, p)]
            accs.append(parts[0] + parts[1] + parts[2])
        o_ref[:, cs] = _finish_pair(*accs)


def _mix_b(qb, kb, vbt, bias, bn, seq):
    n = qb.shape[0]
    nb = seq // B_TQ
    q_map = lambda b, i: (b * nb + i, 0)

    def kv_map(j):
        return lambda b, i: (b * nb + jnp.clip(i - 1, 0, nb - 3) + j, 0)

    def vt_map(j):
        return lambda b, i: (b * nb + jnp.clip(i - 1, 0, nb - 3) + j, 0, 0)

    def bias_map(b, i):
        return (jnp.where(i == 0, 0, jnp.where(i == nb - 1, 2, 1)), 0, 0, 0)

    blk = lambda m: pl.BlockSpec((B_TQ, B_W), m)
    vblk = lambda m: pl.BlockSpec((None, vbt.shape[1], B_TQ), m)
    return pl.pallas_call(
        _mix_b_kernel,
        grid=(bn, nb),
        in_specs=[blk(q_map), blk(kv_map(0)), blk(kv_map(1)), blk(kv_map(2)),
                  vblk(vt_map(0)), vblk(vt_map(1)), vblk(vt_map(2)),
                  pl.BlockSpec((1, B_W // 128, B_TK, 2 * B_TQ), bias_map)],
        out_specs=blk(q_map),
        out_shape=jax.ShapeDtypeStruct((n, B_W), BF16),
        compiler_params=pltpu.CompilerParams(dimension_semantics=("parallel", "parallel"),
                                             vmem_limit_bytes=VMEM_LIMIT),
        name="mix_b",
    )(qb, kb, kb, kb, vbt, vbt, vbt, bias)


def _neighbourhood_bias(rpb, rows):
    n_dc = 2 * NA_COLS - 1
    c = np.arange(GRID_W)
    cs = np.clip(c - NA_COLS // 2, 0, GRID_W - NA_COLS)
    col_ok = (c[:, None] >= cs[None, :]) & (c[:, None] < cs[None, :] + NA_COLS)
    dc = np.clip(c[:, None] - c[None, :] + NA_COLS - 1, 0, n_dc - 1)
    pick = (dc.reshape(-1)[None, :] == np.arange(n_dc)[:, None]).astype(np.float32)
    by_col = jnp.einsum("hrd,dx->hrx", rpb.astype(F32) * LOG2E, jnp.asarray(pick),
                        precision=lax.Precision.HIGHEST)
    by_col = by_col.reshape(B_HEADS, 2 * NA_ROWS - 1, GRID_W, GRID_W)
    by_col = jnp.where(col_ok[None, None], by_col, NEG)
    masked = jnp.full((B_HEADS, GRID_W, GRID_W), NEG, F32)
    tables = []
    for r0, ws in ((0, 0), (4, 0), (rows - 4, rows - 12)):
        key_rows = []
        for kr in range(B_TK // GRID_W):
            blocks = []
            for rr in range(B_TQ // GRID_W):
                r = r0 + rr
                rs = min(max(r - NA_ROWS // 2, 0), rows - NA_ROWS)
                ok = rs <= ws + kr < rs + NA_ROWS
                blocks.append(by_col[:, ws + kr - r + NA_ROWS - 1] if ok else masked)
            key_rows.append(jnp.concatenate(blocks, axis=-1))
        t = jnp.concatenate(key_rows, axis=-2)
        t = t.reshape(B_HEADS // 2, 2, B_TK, B_TQ).transpose(0, 2, 1, 3).reshape(B_HEADS // 2, B_TK, 2 * B_TQ)
        tables.append(t)
    return jnp.stack(tables, axis=0)


def _ffn(x1, gain, wg_ref, wu_ref, wd_ref):
    hn = _rms(x1, gain).astype(BF16)
    acc = None
    for c in range(D_FF // FF_CHUNK):
        cs = slice(c * FF_CHUNK, (c + 1) * FF_CHUNK)
        gate = jnp.dot(hn, wg_ref[:, cs], preferred_element_type=F32)
        up = jnp.dot(hn, wu_ref[:, cs], preferred_element_type=F32)
        act = (gate * (1.0 / (1.0 + jnp.exp(-gate))) * up).astype(BF16)
        d = jnp.dot(act, wd_ref[cs, :], preferred_element_type=F32)
        acc = d if acc is None else acc + d
    return x1 + acc


def _post_even_kernel(x_ref, oa_ref, ob_ref, woa_ref, wob_ref, g_ref, wg_ref, wu_ref, wd_ref, y_ref):
    mix = (jnp.dot(oa_ref[...], woa_ref[...], preferred_element_type=F32)
           + jnp.dot(ob_ref[...], wob_ref[...], preferred_element_type=F32))
    y_ref[...] = _ffn(x_ref[...] + mix, g_ref[...], wg_ref, wu_ref, wd_ref)


def _post_even(x2, oa, ob, woa, wob, g, wg, wu, wd):
    n = x2.shape[0]
    tm = ROW_TILE
    row = lambda i: (i, 0)
    return pl.pallas_call(
        _post_even_kernel,
        grid=(n // tm,),
        in_specs=[pl.BlockSpec((tm, D_MODEL), row), pl.BlockSpec((tm, A_Q), row), pl.BlockSpec((tm, B_W), row),
                  _resident(woa.shape), _resident(wob.shape), _resident((1, D_MODEL)),
                  _resident(wg.shape), _resident(wu.shape), _resident(wd.shape)],
        out_specs=pl.BlockSpec((tm, D_MODEL), row),
        out_shape=jax.ShapeDtypeStruct((n, D_MODEL), F32),
        compiler_params=pltpu.CompilerParams(dimension_semantics=("parallel",), vmem_limit_bytes=VMEM_LIMIT),
        name="post_even",
    )(x2, oa, ob, woa, wob, g, wg, wu, wd)


def _post_odd_kernel(x_ref, o1_ref, of2_ref, of3_ref, l1_ref, l2_ref, l3_ref, ex_ref, wo_ref, g_ref,
                     wg_ref, wu_ref, wd_ref, gf_ref, y_ref, o2_ref, o3_ref):
    tm = x_ref.shape[0]
    n_cb = C_W // 128
    for (_, dil), src, dst in zip(C_GROUPS[1:], (of2_ref, of3_ref), (o2_ref, o3_ref)):
        for rho in range(dil):
            for cb in range(n_cb):
                dst[cb, pl.ds(rho, tm // dil, stride=dil), :] = src[0, rho, :, cb * 128:(cb + 1) * 128].astype(F32)
    o2 = jnp.concatenate([o2_ref[cb] for cb in range(n_cb)], axis=1)
    o3 = jnp.concatenate([o3_ref[cb] for cb in range(n_cb)], axis=1)
    l1, l2, l3 = l1_ref[...], l2_ref[...], l3_ref[...]
    mx = jnp.maximum(jnp.maximum(l1, l2), l3)
    e1, e2, e3 = jnp.exp2(l1 - mx), jnp.exp2(l2 - mx), jnp.exp2(l3 - mx)
    den = e1 + e2 + e3
    ex = ex_ref[...]

    def widen(w):
        hi = w.astype(BF16)
        lo = (w - hi.astype(F32)).astype(BF16)
        return jnp.dot(hi, ex, preferred_element_type=F32) + jnp.dot(lo, ex, preferred_element_type=F32)

    o = widen(e1 / den) * o1_ref[...].astype(F32) + widen(e2 / den) * o2 + widen(e3 / den) * o3
    mix = jnp.dot(o.astype(BF16), wo_ref[...], preferred_element_type=F32)
    y = _ffn(x_ref[...] + mix, g_ref[...], wg_ref, wu_ref, wd_ref)
    y_ref[...] = _rms(y, gf_ref[...])


def _post_odd(x2, seq, os_, ls_, ex, wo, g, wg, wu, wd, gf):
    n = x2.shape[0]
    tm = ROW_TILE
    nt = seq // tm
    row = lambda i: (i, 0)
    wide = pl.BlockSpec((tm, D_MODEL), row)
    narrow = pl.BlockSpec((tm, C_HEADS), row)
    folded = [pl.BlockSpec((1, dil, tm // dil, C_W), lambda i: (i // nt, 0, i % nt, 0)) for _, dil in C_GROUPS[1:]]
    os_ = [os_[0].reshape(n, C_W)] + [o.reshape(n // seq, dil, seq // dil, C_W)
                                      for o, (_, dil) in zip(os_[1:], C_GROUPS[1:])]
    return pl.pallas_call(
        _post_odd_kernel,
        grid=(n // tm,),
        in_specs=[wide, wide, *folded, narrow, narrow, narrow, _resident(ex.shape), _resident(wo.shape),
                  _resident((1, D_MODEL)), _resident(wg.shape), _resident(wu.shape), _resident(wd.shape),
                  _resident((1, D_MODEL))],
        out_specs=wide,
        out_shape=jax.ShapeDtypeStruct((n, D_MODEL), F32),
        scratch_shapes=[pltpu.VMEM((C_W // 128, tm, 128), F32), pltpu.VMEM((C_W // 128, tm, 128), F32)],
        compiler_params=pltpu.CompilerParams(dimension_semantics=("parallel",), vmem_limit_bytes=VMEM_LIMIT),
        name="post_odd",
    )(x2, *os_, *ls_, ex, wo, g, wg, wu, wd, gf)


def _in_odd_kernel(x_ref, g_ref, w_ref, *refs):
    n_slabs = 3 * C_W // IN_ODD_SLAB
    out_refs, p_refs, f_refs = refs[:-4], refs[-4:-2], refs[-2:]
    tm = x_ref.shape[0]
    assert [d for _, d in C_GROUPS] == [1, 4, 16]
    hn = _rms(x_ref[...], g_ref[...]).astype(BF16)

    def project(slab):
        cols = slice(slab * IN_ODD_SLAB, (slab + 1) * IN_ODD_SLAB)
        return jnp.dot(hn, w_ref[:, cols], preferred_element_type=F32)

    ahead = [project(s) for s in range(IN_ODD_AHEAD)]
    for slab in range(n_slabs):
        p = ahead.pop(0)
        if slab + IN_ODD_AHEAD < n_slabs:
            ahead.append(project(slab + IN_ODD_AHEAD))
        part, col0 = divmod(slab * IN_ODD_SLAB, C_W)
        if part == 0:
            p = p * (SCALE * LOG2E)
        p_ref, f_ref = p_refs[slab % 2], f_refs[slab % 2]
        dst1, dst4, dst16 = out_refs[part * 3:part * 3 + 3]
        dst1[0, 0, :, col0:col0 + IN_ODD_SLAB] = p.astype(BF16)
        for cb in range(IN_ODD_SLAB // 128):
            cols = slice(col0 + cb * 128, col0 + (cb + 1) * 128)
            p_ref[cb] = p[:, cb * 128:(cb + 1) * 128]
            for r4 in range(4):
                by4 = p_ref[cb, pl.ds(r4, tm // 4, stride=4), :]
                dst4[0, r4, :, cols] = by4.astype(BF16)
                f_ref[cb, r4] = by4
                for j in range(4):
                    by16 = f_ref[cb, r4, pl.ds(j, tm // 16, stride=4), :]
                    dst16[0, r4 + 4 * j, :, cols] = by16.astype(BF16)


def _in_odd(x2, bn, seq, g, w):
    n = x2.shape[0]
    tm = ROW_TILE
    nt = seq // tm
    specs, shapes = [], []
    for _ in range(3):
        for _, dil in C_GROUPS:
            specs.append(pl.BlockSpec((1, dil, tm // dil, C_W), lambda i: (i // nt, 0, i % nt, 0)))
            shapes.append(jax.ShapeDtypeStruct((bn, dil, seq // dil, C_W), BF16))
    outs = pl.pallas_call(
        _in_odd_kernel,
        grid=(n // tm,),
        in_specs=[pl.BlockSpec((tm, D_MODEL), lambda i: (i, 0)), _resident((1, D_MODEL)), _resident(w.shape)],
        out_specs=tuple(specs),
        out_shape=tuple(shapes),
        scratch_shapes=[pltpu.VMEM((IN_ODD_SLAB // 128, tm, 128), F32)] * 2
        + [pltpu.VMEM((IN_ODD_SLAB // 128, 4, tm // 4, 128), F32)] * 2,
        compiler_params=pltpu.CompilerParams(dimension_semantics=("parallel",), vmem_limit_bytes=VMEM_LIMIT),
        name="in_odd",
    )(x2, g, w)
    outs = [a.reshape(a.shape[0] * a.shape[1], a.shape[2], C_W) for a in outs]
    n_g = len(C_GROUPS)
    return outs[:n_g], outs[n_g:2 * n_g], outs[2 * n_g:]


def _band_kernel(sl_ref, q_ref, kp_ref, kc_ref, kn_ref, vp_ref, vc_ref, vn_ref, o_ref, lse_ref,
                 kwin_ref, vt_ref, pen_ref, *, length, tile):
    t = pl.program_id(1)
    n_pairs = C_W // 128
    win = C_TQ + 2 * C_HALF
    span = tile + 2 * C_HALF
    kwin_ref[0:C_HALF, :] = kp_ref[C_TQ - C_HALF:C_TQ, :]
    kwin_ref[C_HALF:C_HALF + tile, :] = kc_ref[...]
    kwin_ref[C_HALF + tile:span, :] = kn_ref[0:C_HALF, :]

    key = lax.broadcasted_iota(jnp.int32, (win, 2 * C_TQ), 0)
    lane = lax.broadcasted_iota(jnp.int32, (win, 2 * C_TQ), 1)
    dist = jnp.abs(key - C_HALF - (lane & (C_TQ - 1)))
    distf = dist.astype(F32)
    ones = jnp.ones((V_ROWS - HEAD_DIM, span), BF16)
    for c in range(n_pairs):
        cs = slice(c * 128, (c + 1) * 128)
        vwin = jnp.concatenate([vp_ref[C_TQ - C_HALF:C_TQ, cs], vc_ref[:, cs], vn_ref[0:C_HALF, cs]], axis=0)
        vt = vwin.astype(F32).T.astype(BF16)
        for hh in range(2):
            vt_ref[c, hh * V_ROWS:hh * V_ROWS + HEAD_DIM, :] = vt[hh * HEAD_DIM:(hh + 1) * HEAD_DIM, :]
            vt_ref[c, hh * V_ROWS + HEAD_DIM:(hh + 1) * V_ROWS, :] = ones
        slope = jnp.where(lane < C_TQ, sl_ref[2 * c], sl_ref[2 * c + 1])
        pen_ref[c] = jnp.where(dist <= C_HALF, slope * distf, -NEG)

    def scores(j, c):
        cs = slice(c * 128, (c + 1) * 128)
        qs = _split_pair(q_ref[j * C_TQ:(j + 1) * C_TQ, cs])
        k = kwin_ref[j * C_TQ:j * C_TQ + win, cs]
        return lax.dot_general(k, qs, _NT, preferred_element_type=F32)

    n_blocks = tile // C_TQ
    first_key = t * tile - C_HALF + key
    edge = {0: jnp.where(first_key >= 0, 0.0, -NEG),
            n_blocks - 1: jnp.where(first_key + (n_blocks - 1) * C_TQ < length, 0.0, -NEG)}
    items = [(j, c) for j in range(n_blocks) for c in range(n_pairs)]
    ahead = [scores(*item) for item in items[:C_AHEAD]]
    for n_item, (j, c) in enumerate(items):
        rows = slice(j * C_TQ, (j + 1) * C_TQ)
        cs = slice(c * 128, (c + 1) * 128)
        s = ahead.pop(0) - pen_ref[c]
        if n_item + C_AHEAD < len(items):
            ahead.append(scores(*items[n_item + C_AHEAD]))
        if j in edge:
            s = s - edge[j]
        m = jnp.max(s, axis=0, keepdims=True)
        p = jnp.exp2(s - m).astype(BF16)
        acc = jnp.dot(vt_ref[c, :, j * C_TQ:j * C_TQ + win], p, preferred_element_type=F32)
        a0, a1 = acc[:V_ROWS, :C_TQ], acc[V_ROWS:, C_TQ:]
        o_ref[rows, cs] = _finish_pair(a0, a1)
        lse_ref[2 * c:2 * c + 1, rows] = m[:, :C_TQ] + jnp.log2(a0[HEAD_DIM:HEAD_DIM + 1])
        lse_ref[2 * c + 1:2 * c + 2, rows] = m[:, C_TQ:] + jnp.log2(a1[HEAD_DIM:HEAD_DIM + 1])


def _band(slopes, q, k, v):
    nb_, length, _ = q.shape
    tile = 512 if length % 512 == 0 else 256
    per = tile // C_TQ
    last = length // C_TQ - 1
    cur = lambda n, t: (n, t, 0)
    prev = lambda n, t: (n, jnp.maximum(t * per - 1, 0), 0)
    nxt = lambda n, t: (n, jnp.minimum((t + 1) * per, last), 0)
    big = lambda m: pl.BlockSpec((None, tile, C_W), m)
    halo = lambda m: pl.BlockSpec((None, C_TQ, C_W), m)
    span = tile + 2 * C_HALF
    return pl.pallas_call(
        functools.partial(_band_kernel, length=length, tile=tile),
        grid=(nb_, length // tile),
        in_specs=[pl.BlockSpec(memory_space=pltpu.SMEM), big(cur),
                  halo(prev), big(cur), halo(nxt), halo(prev), big(cur), halo(nxt)],
        out_specs=(big(cur), pl.BlockSpec((None, C_HEADS, tile), lambda n, t: (n, 0, t))),
        out_shape=(jax.ShapeDtypeStruct((nb_, length, C_W), BF16),
                   jax.ShapeDtypeStruct((nb_, C_HEADS, length), F32)),
        scratch_shapes=[pltpu.VMEM((span, C_W), BF16), pltpu.VMEM((C_W // 128, 2 * V_ROWS, span), BF16),
                        pltpu.VMEM((C_W // 128, C_TQ + 2 * C_HALF, 2 * C_TQ), F32)],
        compiler_params=pltpu.CompilerParams(dimension_semantics=("parallel", "parallel"),
                                             vmem_limit_bytes=VMEM_LIMIT),
        name="band",
    )(slopes, q, k, k, k, v, v, v)


def _rope_tables(seq):
    t = jnp.arange(seq)
    n = HEAD_DIM // 4
    freqs = jnp.power(ROPE_THETA, -jnp.arange(n, dtype=F32) / n)
    ang_r = (t // GRID_W).astype(F32)[:, None] * freqs[None, :]
    ang_c = (t % GRID_W).astype(F32)[:, None] * freqs[None, :]
    zero = jnp.zeros_like(ang_r)
    cos = jnp.concatenate([jnp.cos(ang_r)] * 2 + [jnp.cos(ang_c)] * 2, axis=-1)
    sin_r, sin_c = jnp.sin(ang_r), jnp.sin(ang_c)
    sa = jnp.concatenate([-sin_r, zero, -sin_c, zero], axis=-1)
    sb = jnp.concatenate([zero, sin_r, zero, sin_c], axis=-1)
    two = lambda a: jnp.concatenate([a, a], axis=-1)
    return two(cos), two(sa), two(sb)


def _trunk(x, prm):
    bn, seq, _ = x.shape
    n = bn * seq
    x2 = x.reshape(n, D_MODEL)
    cos, sa, sb = _rope_tables(seq)
    qa, ka, vat, qb, kb, vbt = _in_even(x2, seq, prm["g_mix0"], prm["w_in_even"], prm["gq"], prm["gk"],
                                        cos, sa, sb, prm["bd"])
    oa = _mix_a(qa, ka, vat, bn, seq)
    ob = _mix_b(qb, kb, vbt, _neighbourhood_bias(prm["rpb"], seq // GRID_W), bn, seq)
    x2 = _post_even(x2, oa, ob, prm["wo_a"], prm["wo_b"], prm["g_ffn0"], prm["wg0"], prm["wu0"], prm["wd0"])

    qs, ks, vs = _in_odd(x2, bn, seq, prm["g_mix1"], prm["w_in_odd"])
    outs, lses = [], []
    for (_, dil), q, k, v in zip(C_GROUPS, qs, ks, vs):
        o, lse = _band(prm["slopes"] * dil * LOG2E, q, k, v)
        outs.append(o)
        lses.append(lse.reshape(bn, dil, C_HEADS, seq // dil).transpose(0, 3, 1, 2).reshape(n, C_HEADS))
    y = _post_odd(x2, seq, outs, lses, prm["expand"], prm["wo_odd"], prm["g_ffn1"], prm["wg1"], prm["wu1"],
                  prm["wd1"], prm["g_final"])
    return y.reshape(bn, seq, D_MODEL)


def kernel(x_prompt, x_sample, norm_mix, norm_ffn, norm_final, w_in_even, a_q_norm, a_k_norm, na_rpb,
           w_out_even, w_in_odd, w_out_odd, w_gate_up, w_down):
    head = np.arange(128) // HEAD_DIM
    order = np.array([0, 4, 1, 5, 2, 6, 3, 7])
    w_in0 = w_in_even[0]
    w_qa = w_in0[:, :A_Q].reshape(D_MODEL, A_Q // HEAD_DIM, HEAD_DIM)[:, order].reshape(D_MODEL, A_Q)
    w_in0 = jnp.concatenate([w_qa, w_in0[:, A_Q:]], axis=1)
    wo_a = w_out_even[0, :A_Q].reshape(A_Q // HEAD_DIM, HEAD_DIM, D_MODEL)[order].reshape(A_Q, D_MODEL)
    prm = {
        "g_mix0": norm_mix[0][None].astype(F32), "g_mix1": norm_mix[1][None].astype(F32),
        "g_ffn0": norm_ffn[0][None].astype(F32), "g_ffn1": norm_ffn[1][None].astype(F32),
        "g_final": norm_final[None].astype(F32),
        "w_in_even": w_in0.astype(BF16), "w_in_odd": w_in_odd[0].astype(BF16),
        "gq": jnp.tile(a_q_norm[0].astype(F32) * (SCALE * LOG2E), 2)[None], "gk": jnp.tile(a_k_norm[0].astype(F32), 2)[None],
        "bd": jnp.asarray((head[:, None] == head[None, :]) / HEAD_DIM, BF16),
        "rpb": na_rpb[0],
        "wo_a": wo_a.astype(BF16), "wo_b": w_out_even[0, A_Q:].astype(BF16),
        "wo_odd": w_out_odd[0].astype(BF16),
        "wg0": w_gate_up[0, :, :D_FF].astype(BF16), "wu0": w_gate_up[0, :, D_FF:].astype(BF16),
        "wg1": w_gate_up[1, :, :D_FF].astype(BF16), "wu1": w_gate_up[1, :, D_FF:].astype(BF16),
        "wd0": w_down[0].astype(BF16), "wd1": w_down[1].astype(BF16),
        "slopes": jnp.exp2(-8.0 * (jnp.arange(C_HEADS, dtype=F32) + 1.0) / C_HEADS),
        "expand": jnp.asarray(np.arange(C_HEADS)[:, None] == (np.arange(C_W) // HEAD_DIM)[None, :], BF16),
    }
    return (_trunk(x_prompt, prm), _trunk(x_sample, prm))
```

```python
import functools

import jax
import jax.numpy as jnp
import numpy as np
from jax import lax
from jax.experimental import pallas as pl
from jax.experimental.pallas import tpu as pltpu

F32 = jnp.float32
BF16 = jnp.bfloat16

D_MODEL = 1024
HEAD_DIM = 64
A_Q = 512
A_KV = 128
B_W = 512
C_W = 1024
C_HEADS = 16
B_HEADS = 8
GRID_W = 64
NA_ROWS = 8
NA_COLS = 16
C_GROUPS = ((128, 1), (512, 4), (2048, 16))
ROPE_THETA = 10000.0
D_FF = 2816
FF_CHUNK = 1408
EPS = 1e-6
NEG = -1e30
SCALE = HEAD_DIM ** -0.5
LOG2E = 1.4426950408889634

V_ROWS = HEAD_DIM + 16
ROW_TILE = 512
A_TQ = 256
A_TK = 256
A_UNROLL = 8
A_AHEAD = 5
B_TQ = 256
B_TK = 3 * B_TQ
B_AHEAD = 5
C_TQ = 128
C_HALF = 64
IN_ODD_SLAB = 256
IN_ODD_AHEAD = 2
C_AHEAD = 3
VMEM_LIMIT = 56 * 1024 * 1024

_NT = (((1,), (1,)), ((), ()))


def _resident(shape):
    zeros = (0,) * len(shape)
    return pl.BlockSpec(shape, lambda *_: zeros, pipeline_mode=pl.Buffered(1))


def _rms(x, gain):
    ms = jnp.mean(x * x, axis=-1, keepdims=True)
    return x * lax.rsqrt(ms + EPS) * gain


def _in_even_kernel(x_ref, g_ref, w_ref, gq_ref, gk_ref, cos_ref, sa_ref, sb_ref, bd_ref,
                    qa_ref, ka_ref, vat_ref, qb_ref, kb_ref, vbt_ref):
    hn = _rms(x_ref[...], g_ref[...]).astype(BF16)
    p = jnp.dot(hn, w_ref[...], preferred_element_type=F32)
    cos, sa, sb, bd = cos_ref[...], sa_ref[...], sb_ref[...], bd_ref[...]

    def headnorm_rope(c, gain):
        c2 = c * c
        hi = c2.astype(BF16)
        lo = (c2 - hi.astype(F32)).astype(BF16)
        ms = (jnp.dot(hi, bd, preferred_element_type=F32)
              + jnp.dot(lo, bd, preferred_element_type=F32))
        y = c * lax.rsqrt(ms + EPS) * gain
        return y * cos + pltpu.roll(y, 112, 1) * sa + pltpu.roll(y, 16, 1) * sb

    for j in range(A_Q // 128):
        qa_ref[:, j * 128:(j + 1) * 128] = headnorm_rope(p[:, j * 128:(j + 1) * 128], gq_ref[...]).astype(BF16)
    ka_ref[...] = headnorm_rope(p[:, A_Q:A_Q + A_KV], gk_ref[...]).astype(BF16)
    o = A_Q + 2 * A_KV
    qb_ref[...] = (p[:, o:o + B_W] * (SCALE * LOG2E)).astype(BF16)
    kb_ref[...] = p[:, o + B_W:o + 2 * B_W].astype(BF16)

    def put_transposed(dst_ref, col0, n_pairs):
        ones = jnp.ones((V_ROWS - HEAD_DIM, A_TK), BF16)
        for j in range(p.shape[0] // A_TK):
            for c in range(n_pairs):
                t = p[j * A_TK:(j + 1) * A_TK, col0 + c * 128:col0 + (c + 1) * 128].T.astype(BF16)
                for hh in range(2):
                    r0 = (2 * c + hh) * V_ROWS
                    dst_ref[j, r0:r0 + HEAD_DIM, :] = t[hh * HEAD_DIM:(hh + 1) * HEAD_DIM, :]
                    dst_ref[j, r0 + HEAD_DIM:r0 + V_ROWS, :] = ones

    put_transposed(vat_ref, A_Q + A_KV, A_KV // 128)
    put_transposed(vbt_ref, o + 2 * B_W, B_W // 128)


def _in_even(x2, seq, g, w, gq, gk, cos, sa, sb, bd):
    n = x2.shape[0]
    tm = ROW_TILE
    nt = seq // tm
    row = lambda i: (i, 0)
    pos = lambda i: (i % nt, 0)
    va_rows, vb_rows = (A_KV // HEAD_DIM) * V_ROWS, B_HEADS * V_ROWS
    out_shape = (
        jax.ShapeDtypeStruct((n, A_Q), BF16), jax.ShapeDtypeStruct((n, A_KV), BF16),
        jax.ShapeDtypeStruct((n // A_TK, va_rows, A_TK), BF16),
        jax.ShapeDtypeStruct((n, B_W), BF16), jax.ShapeDtypeStruct((n, B_W), BF16),
        jax.ShapeDtypeStruct((n // A_TK, vb_rows, A_TK), BF16))
    return pl.pallas_call(
        _in_even_kernel,
        grid=(n // tm,),
        in_specs=[pl.BlockSpec((tm, D_MODEL), row), _resident((1, D_MODEL)), _resident(w.shape),
                  _resident((1, 128)), _resident((1, 128)),
                  pl.BlockSpec((tm, 128), pos), pl.BlockSpec((tm, 128), pos), pl.BlockSpec((tm, 128), pos),
                  _resident((128, 128))],
        out_specs=(pl.BlockSpec((tm, A_Q), row), pl.BlockSpec((tm, A_KV), row),
                   pl.BlockSpec((tm // A_TK, va_rows, A_TK), lambda i: (i, 0, 0)),
                   pl.BlockSpec((tm, B_W), row), pl.BlockSpec((tm, B_W), row),
                   pl.BlockSpec((tm // A_TK, vb_rows, A_TK), lambda i: (i, 0, 0))),
        out_shape=out_shape,
        compiler_params=pltpu.CompilerParams(dimension_semantics=("parallel",), vmem_limit_bytes=VMEM_LIMIT),
        name="in_even",
    )(x2, g, w, gq, gk, cos, sa, sb, bd)


def _split_pair(qp):
    low = lax.broadcasted_iota(jnp.int32, qp.shape, 1) < HEAD_DIM
    zero = jnp.zeros(qp.shape, qp.dtype)
    return jnp.concatenate([jnp.where(low, qp, zero), jnp.where(low, zero, qp)], axis=0)


def _finish_pair(acc_lo, acc_hi):
    halves = [a[:HEAD_DIM] / a[HEAD_DIM:HEAD_DIM + 1] for a in (acc_lo, acc_hi)]
    return jnp.concatenate(halves, axis=0).T.astype(BF16)


def _mix_a_kernel(q_ref, k_ref, vt_ref, o_ref, qs_ref, m_ref, acc_ref, *, n_chunks):
    tq = A_TQ
    n_pairs = A_Q // 128
    for c in range(n_pairs):
        qs_ref[:, 2 * c * tq:(2 * c + 2) * tq] = (
            _split_pair(q_ref[:, c * 128:(c + 1) * 128]).astype(F32).T.astype(BF16))
    m_ref[...] = jnp.full(m_ref.shape, NEG, F32)
    acc_ref[...] = jnp.zeros(acc_ref.shape, F32)

    def scores(ci, h):
        k = k_ref[pl.ds(pl.multiple_of(ci * A_TK, A_TK), A_TK), :]
        return jnp.dot(k, qs_ref[:, h * tq:(h + 1) * tq], preferred_element_type=F32)

    def chunks(it, carry):
        items = [(it * A_UNROLL + u, h) for u in range(A_UNROLL) for h in range(2 * n_pairs)]
        ahead = [scores(*item) for item in items[:A_AHEAD]]
        for t, (ci, h) in enumerate(items):
            s = ahead.pop(0)
            if t + A_AHEAD < len(items):
                ahead.append(scores(*items[t + A_AHEAD]))
            cols = slice(h * tq, (h + 1) * tq)
            g = h % 2
            m_old = m_ref[:, cols]
            m_new = jnp.maximum(m_old, jnp.max(s, axis=0, keepdims=True))
            alpha = jnp.exp2(m_old - m_new)
            p = jnp.exp2(s - m_new).astype(BF16)
            m_ref[:, cols] = m_new
            vt = vt_ref[ci, g * V_ROWS:(g + 1) * V_ROWS, :]
            acc_ref[h] = alpha * acc_ref[h] + jnp.dot(vt, p, preferred_element_type=F32)
        return carry

    lax.fori_loop(0, n_chunks // A_UNROLL, chunks, 0)

    for c in range(n_pairs):
        o_ref[:, c * 128:(c + 1) * 128] = _finish_pair(acc_ref[2 * c], acc_ref[2 * c + 1])


def _mix_a(qa, ka, vat, bn, seq):
    n = qa.shape[0]
    nq = seq // A_TQ
    nc = seq // A_TK
    return pl.pallas_call(
        functools.partial(_mix_a_kernel, n_chunks=nc),
        grid=(bn, nq),
        in_specs=[pl.BlockSpec((A_TQ, A_Q), lambda b, i: (b * nq + i, 0)),
                  pl.BlockSpec((seq, A_KV), lambda b, i: (b, 0)),
                  pl.BlockSpec((nc, vat.shape[1], A_TK), lambda b, i: (b, 0, 0))],
        out_specs=pl.BlockSpec((A_TQ, A_Q), lambda b, i: (b * nq + i, 0)),
        out_shape=jax.ShapeDtypeStruct((n, A_Q), BF16),
        scratch_shapes=[pltpu.VMEM((128, 8 * A_TQ), BF16), pltpu.VMEM((1, 8 * A_TQ), F32),
                        pltpu.VMEM((8, V_ROWS, A_TQ), F32)],
        compiler_params=pltpu.CompilerParams(dimension_semantics=("parallel", "parallel"),
                                             vmem_limit_bytes=VMEM_LIMIT),
        name="mix_a",
    )(qa, ka, vat)


def _mix_b_kernel(q_ref, k0_ref, k1_ref, k2_ref, vt0_ref, vt1_ref, vt2_ref, bias_ref, o_ref,
                  qs_ref, m_ref, acc_ref):
    tq = B_TQ
    n_pairs = B_W // 128
    k_refs = (k0_ref, k1_ref, k2_ref)
    vt_refs = (vt0_ref, vt1_ref, vt2_ref)
    for c in range(n_pairs):
        qs_ref[:, 2 * c * tq:(2 * c + 2) * tq] = (
            _split_pair(q_ref[:, c * 128:(c + 1) * 128]).astype(F32).T.astype(BF16))

    def scores(h, j):
        c = h // 2
        return jnp.dot(k_refs[j][:, c * 128:(c + 1) * 128], qs_ref[:, h * tq:(h + 1) * tq],
                       preferred_element_type=F32)

    items = [(h, j) for h in range(2 * n_pairs) for j in range(len(k_refs))]
    ahead = [scores(*item) for item in items[:B_AHEAD]]
    for t, (h, j) in enumerate(items):
        c, hh = divmod(h, 2)
        s = bias_ref[0, c, j * tq:(j + 1) * tq, hh * tq:(hh + 1) * tq] + ahead.pop(0)
        if t + B_AHEAD < len(items):
            ahead.append(scores(*items[t + B_AHEAD]))
        rows = slice(h * V_ROWS, (h + 1) * V_ROWS)
        m_new = jnp.max(s, axis=0, keepdims=True)
        if j > 0:
            m_old = m_ref[h]
            m_new = jnp.maximum(m_old, m_new)
        pv = jnp.dot(vt_refs[j][rows, :], jnp.exp2(s - m_new).astype(BF16), preferred_element_type=F32)
        acc_ref[h] = pv if j == 0 else jnp.exp2(m_old - m_new) * acc_ref[h] + pv
        m_ref[h] = m_new
        if j == len(k_refs) - 1 and hh == 1:
            o_ref[:, c * 128:(c + 1) * 128] = _finish_pair(acc_ref[h - 1], acc_ref[h])


def _mix_b(qb, kb, vbt, bias, bn, seq):
    n = qb.shape[0]
    nb = seq // B_TQ
    q_map = lambda b, i: (b * nb + i, 0)

    def kv_map(j):
        return lambda b, i: (b * nb + jnp.clip(i - 1, 0, nb - 3) + j, 0)

    def vt_map(j):
        return lambda b, i: (b * nb + jnp.clip(i - 1, 0, nb - 3) + j, 0, 0)

    def bias_map(b, i):
        return (jnp.where(i == 0, 0, jnp.where(i == nb - 1, 2, 1)), 0, 0, 0)

    blk = lambda m: pl.BlockSpec((B_TQ, B_W), m)
    vblk = lambda m: pl.BlockSpec((None, vbt.shape[1], B_TQ), m)
    return pl.pallas_call(
        _mix_b_kernel,
        grid=(bn, nb),
        in_specs=[blk(q_map), blk(kv_map(0)), blk(kv_map(1)), blk(kv_map(2)),
                  vblk(vt_map(0)), vblk(vt_map(1)), vblk(vt_map(2)),
                  pl.BlockSpec((1, B_W // 128, B_TK, 2 * B_TQ), bias_map)],
        out_specs=blk(q_map),
        out_shape=jax.ShapeDtypeStruct((n, B_W), BF16),
        scratch_shapes=[pltpu.VMEM((128, B_HEADS * B_TQ), BF16), pltpu.VMEM((B_HEADS, 1, B_TQ), F32),
                        pltpu.VMEM((B_HEADS, V_ROWS, B_TQ), F32)],
        compiler_params=pltpu.CompilerParams(dimension_semantics=("parallel", "parallel"),
                                             vmem_limit_bytes=VMEM_LIMIT),
        name="mix_b",
    )(qb, kb, kb, kb, vbt, vbt, vbt, bias)


def _neighbourhood_bias(rpb, rows):
    n_dc = 2 * NA_COLS - 1
    c = np.arange(GRID_W)
    cs = np.clip(c - NA_COLS // 2, 0, GRID_W - NA_COLS)
    col_ok = (c[:, None] >= cs[None, :]) & (c[:, None] < cs[None, :] + NA_COLS)
    dc = np.clip(c[:, None] - c[None, :] + NA_COLS - 1, 0, n_dc - 1)
    pick = (dc.reshape(-1)[None, :] == np.arange(n_dc)[:, None]).astype(np.float32)
    by_col = jnp.einsum("hrd,dx->hrx", rpb.astype(F32) * LOG2E, jnp.asarray(pick),
                        precision=lax.Precision.HIGHEST)
    by_col = by_col.reshape(B_HEADS, 2 * NA_ROWS - 1, GRID_W, GRID_W)
    by_col = jnp.where(col_ok[None, None], by_col, NEG)
    masked = jnp.full((B_HEADS, GRID_W, GRID_W), NEG, F32)
    tables = []
    for r0, ws in ((0, 0), (4, 0), (rows - 4, rows - 12)):
        key_rows = []
        for kr in range(B_TK // GRID_W):
            blocks = []
            for rr in range(B_TQ // GRID_W):
                r = r0 + rr
                rs = min(max(r - NA_ROWS // 2, 0), rows - NA_ROWS)
                ok = rs <= ws + kr < rs + NA_ROWS
                blocks.append(by_col[:, ws + kr - r + NA_ROWS - 1] if ok else masked)
            key_rows.append(jnp.concatenate(blocks, axis=-1))
        t = jnp.concatenate(key_rows, axis=-2)
        t = t.reshape(B_HEADS // 2, 2, B_TK, B_TQ).transpose(0, 2, 1, 3).reshape(B_HEADS // 2, B_TK, 2 * B_TQ)
        tables.append(t)
    return jnp.stack(tables, axis=0)


def _ffn(x1, gain, wg_ref, wu_ref, wd_ref):
    hn = _rms(x1, gain).astype(BF16)
    acc = None
    for c in range(D_FF // FF_CHUNK):
        cs = slice(c * FF_CHUNK, (c + 1) * FF_CHUNK)
        gate = jnp.dot(hn, wg_ref[:, cs], preferred_element_type=F32)
        up = jnp.dot(hn, wu_ref[:, cs], preferred_element_type=F32)
        act = (gate * (1.0 / (1.0 + jnp.exp(-gate))) * up).astype(BF16)
        d = jnp.dot(act, wd_ref[cs, :], preferred_element_type=F32)
        acc = d if acc is None else acc + d
    return x1 + acc


def _post_even_kernel(x_ref, oa_ref, ob_ref, woa_ref, wob_ref, g_ref, wg_ref, wu_ref, wd_ref, y_ref):
    mix = (jnp.dot(oa_ref[...], woa_ref[...], preferred_element_type=F32)
           + jnp.dot(ob_ref[...], wob_ref[...], preferred_element_type=F32))
    y_ref[...] = _ffn(x_ref[...] + mix, g_ref[...], wg_ref, wu_ref, wd_ref)


def _post_even(x2, oa, ob, woa, wob, g, wg, wu, wd):
    n = x2.shape[0]
    tm = ROW_TILE
    row = lambda i: (i, 0)
    return pl.pallas_call(
        _post_even_kernel,
        grid=(n // tm,),
        in_specs=[pl.BlockSpec((tm, D_MODEL), row), pl.BlockSpec((tm, A_Q), row), pl.BlockSpec((tm, B_W), row),
                  _resident(woa.shape), _resident(wob.shape), _resident((1, D_MODEL)),
                  _resident(wg.shape), _resident(wu.shape), _resident(wd.shape)],
        out_specs=pl.BlockSpec((tm, D_MODEL), row),
        out_shape=jax.ShapeDtypeStruct((n, D_MODEL), F32),
        compiler_params=pltpu.CompilerParams(dimension_semantics=("parallel",), vmem_limit_bytes=VMEM_LIMIT),
        name="post_even",
    )(x2, oa, ob, woa, wob, g, wg, wu, wd)


def _post_odd_kernel(x_ref, o1_ref, of2_ref, of3_ref, l1_ref, l2_ref, l3_ref, ex_ref, wo_ref, g_ref,
                     wg_ref, wu_ref, wd_ref, gf_ref, y_ref, o2_ref, o3_ref):
    tm = x_ref.shape[0]
    n_cb = C_W // 128
    for (_, dil), src, dst in zip(C_GROUPS[1:], (of2_ref, of3_ref), (o2_ref, o3_ref)):
        for rho in range(dil):
            for cb in range(n_cb):
                dst[cb, pl.ds(rho, tm // dil, stride=dil), :] = src[0, rho, :, cb * 128:(cb + 1) * 128].astype(F32)
    o2 = jnp.concatenate([o2_ref[cb] for cb in range(n_cb)], axis=1)
    o3 = jnp.concatenate([o3_ref[cb] for cb in range(n_cb)], axis=1)
    l1, l2, l3 = l1_ref[...], l2_ref[...], l3_ref[...]
    mx = jnp.maximum(jnp.maximum(l1, l2), l3)
    e1, e2, e3 = jnp.exp2(l1 - mx), jnp.exp2(l2 - mx), jnp.exp2(l3 - mx)
    den = e1 + e2 + e3
    ex = ex_ref[...]

    def widen(w):
        hi = w.astype(BF16)
        lo = (w - hi.astype(F32)).astype(BF16)
        return jnp.dot(hi, ex, preferred_element_type=F32) + jnp.dot(lo, ex, preferred_element_type=F32)

    o = widen(e1 / den) * o1_ref[...].astype(F32) + widen(e2 / den) * o2 + widen(e3 / den) * o3
    mix = jnp.dot(o.astype(BF16), wo_ref[...], preferred_element_type=F32)
    y = _ffn(x_ref[...] + mix, g_ref[...], wg_ref, wu_ref, wd_ref)
    y_ref[...] = _rms(y, gf_ref[...])


def _post_odd(x2, seq, os_, ls_, ex, wo, g, wg, wu, wd, gf):
    n = x2.shape[0]
    tm = ROW_TILE
    nt = seq // tm
    row = lambda i: (i, 0)
    wide = pl.BlockSpec((tm, D_MODEL), row)
    narrow = pl.BlockSpec((tm, C_HEADS), row)
    folded = [pl.BlockSpec((1, dil, tm // dil, C_W), lambda i: (i // nt, 0, i % nt, 0)) for _, dil in C_GROUPS[1:]]
    os_ = [os_[0].reshape(n, C_W)] + [o.reshape(n // seq, dil, seq // dil, C_W)
                                      for o, (_, dil) in zip(os_[1:], C_GROUPS[1:])]
    return pl.pallas_call(
        _post_odd_kernel,
        grid=(n // tm,),
        in_specs=[wide, wide, *folded, narrow, narrow, narrow, _resident(ex.shape), _resident(wo.shape),
                  _resident((1, D_MODEL)), _resident(wg.shape), _resident(wu.shape), _resident(wd.shape),
                  _resident((1, D_MODEL))],
        out_specs=wide,
        out_shape=jax.ShapeDtypeStruct((n, D_MODEL), F32),
        scratch_shapes=[pltpu.VMEM((C_W // 128, tm, 128), F32), pltpu.VMEM((C_W // 128, tm, 128), F32)],
        compiler_params=pltpu.CompilerParams(dimension_semantics=("parallel",), vmem_limit_bytes=VMEM_LIMIT),
        name="post_odd",
    )(x2, *os_, *ls_, ex, wo, g, wg, wu, wd, gf)


def _in_odd_kernel(x_ref, g_ref, w_ref, *refs):
    n_slabs = 3 * C_W // IN_ODD_SLAB
    out_refs, p_refs, f_refs = refs[:-4], refs[-4:-2], refs[-2:]
    tm = x_ref.shape[0]
    assert [d for _, d in C_GROUPS] == [1, 4, 16]
    hn = _rms(x_ref[...], g_ref[...]).astype(BF16)

    def project(slab):
        cols = slice(slab * IN_ODD_SLAB, (slab + 1) * IN_ODD_SLAB)
        return jnp.dot(hn, w_ref[:, cols], preferred_element_type=F32)

    ahead = [project(s) for s in range(IN_ODD_AHEAD)]
    for slab in range(n_slabs):
        p = ahead.pop(0)
        if slab + IN_ODD_AHEAD < n_slabs:
            ahead.append(project(slab + IN_ODD_AHEAD))
        part, col0 = divmod(slab * IN_ODD_SLAB, C_W)
        if part == 0:
            p = p * (SCALE * LOG2E)
        p_ref, f_ref = p_refs[slab % 2], f_refs[slab % 2]
        dst1, dst4, dst16 = out_refs[part * 3:part * 3 + 3]
        dst1[0, 0, :, col0:col0 + IN_ODD_SLAB] = p.astype(BF16)
        for cb in range(IN_ODD_SLAB // 128):
            cols = slice(col0 + cb * 128, col0 + (cb + 1) * 128)
            p_ref[cb] = p[:, cb * 128:(cb + 1) * 128]
            for r4 in range(4):
                by4 = p_ref[cb, pl.ds(r4, tm // 4, stride=4), :]
                dst4[0, r4, :, cols] = by4.astype(BF16)
                f_ref[cb, r4] = by4
                for j in range(4):
                    by16 = f_ref[cb, r4, pl.ds(j, tm // 16, stride=4), :]
                    dst16[0, r4 + 4 * j, :, cols] = by16.astype(BF16)


def _in_odd(x2, bn, seq, g, w):
    n = x2.shape[0]
    tm = ROW_TILE
    nt = seq // tm
    specs, shapes = [], []
    for _ in range(3):
        for _, dil in C_GROUPS:
            specs.append(pl.BlockSpec((1, dil, tm // dil, C_W), lambda i: (i // nt, 0, i % nt, 0)))
            shapes.append(jax.ShapeDtypeStruct((bn, dil, seq // dil, C_W), BF16))
    outs = pl.pallas_call(
        _in_odd_kernel,
        grid=(n // tm,),
        in_specs=[pl.BlockSpec((tm, D_MODEL), lambda i: (i, 0)), _resident((1, D_MODEL)), _resident(w.shape)],
        out_specs=tuple(specs),
        out_shape=tuple(shapes),
        scratch_shapes=[pltpu.VMEM((IN_ODD_SLAB // 128, tm, 128), F32)] * 2
        + [pltpu.VMEM((IN_ODD_SLAB // 128, 4, tm // 4, 128), F32)] * 2,
        compiler_params=pltpu.CompilerParams(dimension_semantics=("parallel",), vmem_limit_bytes=VMEM_LIMIT),
        name="in_odd",
    )(x2, g, w)
    outs = [a.reshape(a.shape[0] * a.shape[1], a.shape[2], C_W) for a in outs]
    n_g = len(C_GROUPS)
    return outs[:n_g], outs[n_g:2 * n_g], outs[2 * n_g:]


def _band_kernel(sl_ref, q_ref, kp_ref, kc_ref, kn_ref, vp_ref, vc_ref, vn_ref, o_ref, lse_ref,
                 kwin_ref, vt_ref, pen_ref, *, length, tile):
    t = pl.program_id(1)
    n_pairs = C_W // 128
    win = C_TQ + 2 * C_HALF
    span = tile + 2 * C_HALF
    kwin_ref[0:C_HALF, :] = kp_ref[C_TQ - C_HALF:C_TQ, :]
    kwin_ref[C_HALF:C_HALF + tile, :] = kc_ref[...]
    kwin_ref[C_HALF + tile:span, :] = kn_ref[0:C_HALF, :]

    key = lax.broadcasted_iota(jnp.int32, (win, 2 * C_TQ), 0)
    lane = lax.broadcasted_iota(jnp.int32, (win, 2 * C_TQ), 1)
    dist = jnp.abs(key - C_HALF - (lane & (C_TQ - 1)))
    distf = dist.astype(F32)
    ones = jnp.ones((V_ROWS - HEAD_DIM, span), BF16)
    for c in range(n_pairs):
        cs = slice(c * 128, (c + 1) * 128)
        vwin = jnp.concatenate([vp_ref[C_TQ - C_HALF:C_TQ, cs], vc_ref[:, cs], vn_ref[0:C_HALF, cs]], axis=0)
        vt = vwin.astype(F32).T.astype(BF16)
        for hh in range(2):
            vt_ref[c, hh * V_ROWS:hh * V_ROWS + HEAD_DIM, :] = vt[hh * HEAD_DIM:(hh + 1) * HEAD_DIM, :]
            vt_ref[c, hh * V_ROWS + HEAD_DIM:(hh + 1) * V_ROWS, :] = ones
        slope = jnp.where(lane < C_TQ, sl_ref[2 * c], sl_ref[2 * c + 1])
        pen_ref[c] = jnp.where(dist <= C_HALF, slope * distf, -NEG)

    def scores(j, c):
        cs = slice(c * 128, (c + 1) * 128)
        qs = _split_pair(q_ref[j * C_TQ:(j + 1) * C_TQ, cs])
        k = kwin_ref[j * C_TQ:j * C_TQ + win, cs]
        return lax.dot_general(k, qs, _NT, preferred_element_type=F32)

    n_blocks = tile // C_TQ
    first_key = t * tile - C_HALF + key
    edge = {0: jnp.where(first_key >= 0, 0.0, -NEG),
            n_blocks - 1: jnp.where(first_key + (n_blocks - 1) * C_TQ < length, 0.0, -NEG)}
    items = [(j, c) for j in range(n_blocks) for c in range(n_pairs)]
    ahead = [scores(*item) for item in items[:C_AHEAD]]
    for n_item, (j, c) in enumerate(items):
        rows = slice(j * C_TQ, (j + 1) * C_TQ)
        cs = slice(c * 128, (c + 1) * 128)
        s = ahead.pop(0) - pen_ref[c]
        if n_item + C_AHEAD < len(items):
            ahead.append(scores(*items[n_item + C_AHEAD]))
        if j in edge:
            s = s - edge[j]
        m = jnp.max(s, axis=0, keepdims=True)
        p = jnp.exp2(s - m).astype(BF16)
        acc = jnp.dot(vt_ref[c, :, j * C_TQ:j * C_TQ + win], p, preferred_element_type=F32)
        a0, a1 = acc[:V_ROWS, :C_TQ], acc[V_ROWS:, C_TQ:]
        o_ref[rows, cs] = _finish_pair(a0, a1)
        lse_ref[2 * c:2 * c + 1, rows] = m[:, :C_TQ] + jnp.log2(a0[HEAD_DIM:HEAD_DIM + 1])
        lse_ref[2 * c + 1:2 * c + 2, rows] = m[:, C_TQ:] + jnp.log2(a1[HEAD_DIM:HEAD_DIM + 1])


def _band(slopes, q, k, v):
    nb_, length, _ = q.shape
    tile = 512 if length % 512 == 0 else 256
    per = tile // C_TQ
    last = length // C_TQ - 1
    cur = lambda n, t: (n, t, 0)
    prev = lambda n, t: (n, jnp.maximum(t * per - 1, 0), 0)
    nxt = lambda n, t: (n, jnp.minimum((t + 1) * per, last), 0)
    big = lambda m: pl.BlockSpec((None, tile, C_W), m)
    halo = lambda m: pl.BlockSpec((None, C_TQ, C_W), m)
    span = tile + 2 * C_HALF
    return pl.pallas_call(
        functools.partial(_band_kernel, length=length, tile=tile),
        grid=(nb_, length // tile),
        in_specs=[pl.BlockSpec(memory_space=pltpu.SMEM), big(cur),
                  halo(prev), big(cur), halo(nxt), halo(prev), big(cur), halo(nxt)],
        out_specs=(big(cur), pl.BlockSpec((None, C_HEADS, tile), lambda n, t: (n, 0, t))),
        out_shape=(jax.ShapeDtypeStruct((nb_, length, C_W), BF16),
                   jax.ShapeDtypeStruct((nb_, C_HEADS, length), F32)),
        scratch_shapes=[pltpu.VMEM((span, C_W), BF16), pltpu.VMEM((C_W // 128, 2 * V_ROWS, span), BF16),
                        pltpu.VMEM((C_W // 128, C_TQ + 2 * C_HALF, 2 * C_TQ), F32)],
        compiler_params=pltpu.CompilerParams(dimension_semantics=("parallel", "parallel"),
                                             vmem_limit_bytes=VMEM_LIMIT),
        name="band",
    )(slopes, q, k, k, k, v, v, v)


def _rope_tables(seq):
    t = jnp.arange(seq)
    n = HEAD_DIM // 4
    freqs = jnp.power(ROPE_THETA, -jnp.arange(n, dtype=F32) / n)
    ang_r = (t // GRID_W).astype(F32)[:, None] * freqs[None, :]
    ang_c = (t % GRID_W).astype(F32)[:, None] * freqs[None, :]
    zero = jnp.zeros_like(ang_r)
    cos = jnp.concatenate([jnp.cos(ang_r)] * 2 + [jnp.cos(ang_c)] * 2, axis=-1)
    sin_r, sin_c = jnp.sin(ang_r), jnp.sin(ang_c)
    sa = jnp.concatenate([-sin_r, zero, -sin_c, zero], axis=-1)
    sb = jnp.concatenate([zero, sin_r, zero, sin_c], axis=-1)
    two = lambda a: jnp.concatenate([a, a], axis=-1)
    return two(cos), two(sa), two(sb)


def _trunk(x, prm):
    bn, seq, _ = x.shape
    n = bn * seq
    x2 = x.reshape(n, D_MODEL)
    cos, sa, sb = _rope_tables(seq)
    qa, ka, vat, qb, kb, vbt = _in_even(x2, seq, prm["g_mix0"], prm["w_in_even"], prm["gq"], prm["gk"],
                                        cos, sa, sb, prm["bd"])
    oa = _mix_a(qa, ka, vat, bn, seq)
    ob = _mix_b(qb, kb, vbt, _neighbourhood_bias(prm["rpb"], seq // GRID_W), bn, seq)
    x2 = _post_even(x2, oa, ob, prm["wo_a"], prm["wo_b"], prm["g_ffn0"], prm["wg0"], prm["wu0"], prm["wd0"])

    qs, ks, vs = _in_odd(x2, bn, seq, prm["g_mix1"], prm["w_in_odd"])
    outs, lses = [], []
    for (_, dil), q, k, v in zip(C_GROUPS, qs, ks, vs):
        o, lse = _band(prm["slopes"] * dil * LOG2E, q, k, v)
        outs.append(o)
        lses.append(lse.reshape(bn, dil, C_HEADS, seq // dil).transpose(0, 3, 1, 2).reshape(n, C_HEADS))
    y = _post_odd(x2, seq, outs, lses, prm["expand"], prm["wo_odd"], prm["g_ffn1"], prm["wg1"], prm["wu1"],
                  prm["wd1"], prm["g_final"])
    return y.reshape(bn, seq, D_MODEL)


def kernel(x_prompt, x_sample, norm_mix, norm_ffn, norm_final, w_in_even, a_q_norm, a_k_norm, na_rpb,
           w_out_even, w_in_odd, w_out_odd, w_gate_up, w_down):
    head = np.arange(128) // HEAD_DIM
    order = np.array([0, 4, 1, 5, 2, 6, 3, 7])
    w_in0 = w_in_even[0]
    w_qa = w_in0[:, :A_Q].reshape(D_MODEL, A_Q // HEAD_DIM, HEAD_DIM)[:, order].reshape(D_MODEL, A_Q)
    w_in0 = jnp.concatenate([w_qa, w_in0[:, A_Q:]], axis=1)
    wo_a = w_out_even[0, :A_Q].reshape(A_Q // HEAD_DIM, HEAD_DIM, D_MODEL)[order].reshape(A_Q, D_MODEL)
    prm = {
        "g_mix0": norm_mix[0][None].astype(F32), "g_mix1": norm_mix[1][None].astype(F32),
        "g_ffn0": norm_ffn[0][None].astype(F32), "g_ffn1": norm_ffn[1][None].astype(F32),
        "g_final": norm_final[None].astype(F32),
        "w_in_even": w_in0.astype(BF16), "w_in_odd": w_in_odd[0].astype(BF16),
        "gq": jnp.tile(a_q_norm[0].astype(F32) * (SCALE * LOG2E), 2)[None], "gk": jnp.tile(a_k_norm[0].astype(F32), 2)[None],
        "bd": jnp.asarray((head[:, None] == head[None, :]) / HEAD_DIM, BF16),
        "rpb": na_rpb[0],
        "wo_a": wo_a.astype(BF16), "wo_b": w_out_even[0, A_Q:].astype(BF16),
        "wo_odd": w_out_odd[0].astype(BF16),
        "wg0": w_gate_up[0, :, :D_FF].astype(BF16), "wu0": w_gate_up[0, :, D_FF:].astype(BF16),
        "wg1": w_gate_up[1, :, :D_FF].astype(BF16), "wu1": w_gate_up[1, :, D_FF:].astype(BF16),
        "wd0": w_down[0].astype(BF16), "wd1": w_down[1].astype(BF16),
        "slopes": jnp.exp2(-8.0 * (jnp.arange(C_HEADS, dtype=F32) + 1.0) / C_HEADS),
        "expand": jnp.asarray(np.arange(C_HEADS)[:, None] == (np.arange(C_W) // HEAD_DIM)[None, :], BF16),
    }
    return (_trunk(x_prompt, prm), _trunk(x_sample, prm))
```

```python
import functools

import jax
import jax.numpy as jnp
import numpy as np
from jax import lax
from jax.experimental import pallas as pl
from jax.experimental.pallas import tpu as pltpu

F32 = jnp.float32
BF16 = jnp.bfloat16

D_MODEL = 1024
HEAD_DIM = 64
A_Q = 512
A_KV = 128
B_W = 512
C_W = 1024
C_HEADS = 16
B_HEADS = 8
GRID_W = 64
NA_ROWS = 8
NA_COLS = 16
C_GROUPS = ((128, 1), (512, 4), (2048, 16))
ROPE_THETA = 10000.0
D_FF = 2816
FF_CHUNK = 1408
EPS = 1e-6
NEG = -1e30
SCALE = HEAD_DIM ** -0.5
LOG2E = 1.4426950408889634

V_ROWS = HEAD_DIM + 16
ROW_TILE = 512
A_TQ = 256
A_TK = 256
A_UNROLL = 16
A_AHEAD = 5
B_TQ = 256
B_TK = 3 * B_TQ
B_AHEAD = 5
C_TQ = 128
C_HALF = 64
IN_SLAB = 256
IN_AHEAD = 2
C_AHEAD = 3
VMEM_LIMIT = 56 * 1024 * 1024

_NT = (((1,), (1,)), ((), ()))


def _resident(shape):
    zeros = (0,) * len(shape)
    return pl.BlockSpec(shape, lambda *_: zeros, pipeline_mode=pl.Buffered(1))


def _rms(x, gain):
    ms = jnp.mean(x * x, axis=-1, keepdims=True)
    return x * lax.rsqrt(ms + EPS) * gain


def _in_even_kernel(x_ref, g_ref, w_ref, gq_ref, gk_ref, cos_ref, sa_ref, sb_ref, bd_ref,
                    qa_ref, ka_ref, vat_ref, qb_ref, kb_ref, vbt_ref):
    hn = _rms(x_ref[...], g_ref[...]).astype(BF16)
    tm = hn.shape[0]
    cos, sa, sb, bd = cos_ref[...], sa_ref[...], sb_ref[...], bd_ref[...]

    def project(slab):
        cols = slice(slab * IN_SLAB, (slab + 1) * IN_SLAB)
        return jnp.dot(hn, w_ref[:, cols], preferred_element_type=F32)

    def headnorm_rope(c, gain):
        c2 = c * c
        hi = c2.astype(BF16)
        lo = (c2 - hi.astype(F32)).astype(BF16)
        ms = (jnp.dot(hi, bd, preferred_element_type=F32)
              + jnp.dot(lo, bd, preferred_element_type=F32))
        y = c * lax.rsqrt(ms + EPS) * gain
        return y * cos + pltpu.roll(y, 112, 1) * sa + pltpu.roll(y, 16, 1) * sb

    def put_transposed(dst_ref, chunk, c):
        ones = jnp.ones((V_ROWS - HEAD_DIM, A_TK), BF16)
        for j in range(tm // A_TK):
            t = chunk[j * A_TK:(j + 1) * A_TK, :].T.astype(BF16)
            for hh in range(2):
                r0 = (2 * c + hh) * V_ROWS
                dst_ref[j, r0:r0 + HEAD_DIM, :] = t[hh * HEAD_DIM:(hh + 1) * HEAD_DIM, :]
                dst_ref[j, r0 + HEAD_DIM:r0 + V_ROWS, :] = ones

    def write_out(chunk, col):
        lanes = lambda base: slice(col - base, col - base + 128)
        o = A_Q + 2 * A_KV
        if col < A_Q:
            qa_ref[:, lanes(0)] = headnorm_rope(chunk, gq_ref[...]).astype(BF16)
        elif col < A_Q + A_KV:
            ka_ref[...] = headnorm_rope(chunk, gk_ref[...]).astype(BF16)
        elif col < o:
            put_transposed(vat_ref, chunk, 0)
        elif col < o + B_W:
            qb_ref[:, lanes(o)] = (chunk * (SCALE * LOG2E)).astype(BF16)
        elif col < o + 2 * B_W:
            kb_ref[:, lanes(o + B_W)] = chunk.astype(BF16)
        else:
            put_transposed(vbt_ref, chunk, (col - o - 2 * B_W) // 128)

    n_slabs = w_ref.shape[1] // IN_SLAB
    ahead = [project(s) for s in range(IN_AHEAD)]
    for slab in range(n_slabs):
        p = ahead.pop(0)
        if slab + IN_AHEAD < n_slabs:
            ahead.append(project(slab + IN_AHEAD))
        for i in range(IN_SLAB // 128):
            write_out(p[:, i * 128:(i + 1) * 128], slab * IN_SLAB + i * 128)


def _in_even(x2, seq, g, w, gq, gk, cos, sa, sb, bd):
    n = x2.shape[0]
    tm = ROW_TILE
    nt = seq // tm
    row = lambda i: (i, 0)
    pos = lambda i: (i % nt, 0)
    va_rows, vb_rows = (A_KV // HEAD_DIM) * V_ROWS, B_HEADS * V_ROWS
    out_shape = (
        jax.ShapeDtypeStruct((n, A_Q), BF16), jax.ShapeDtypeStruct((n, A_KV), BF16),
        jax.ShapeDtypeStruct((n // A_TK, va_rows, A_TK), BF16),
        jax.ShapeDtypeStruct((n, B_W), BF16), jax.ShapeDtypeStruct((n, B_W), BF16),
        jax.ShapeDtypeStruct((n // A_TK, vb_rows, A_TK), BF16))
    return pl.pallas_call(
        _in_even_kernel,
        grid=(n // tm,),
        in_specs=[pl.BlockSpec((tm, D_MODEL), row), _resident((1, D_MODEL)), _resident(w.shape),
                  _resident((1, 128)), _resident((1, 128)),
                  pl.BlockSpec((tm, 128), pos), pl.BlockSpec((tm, 128), pos), pl.BlockSpec((tm, 128), pos),
                  _resident((128, 128))],
        out_specs=(pl.BlockSpec((tm, A_Q), row), pl.BlockSpec((tm, A_KV), row),
                   pl.BlockSpec((tm // A_TK, va_rows, A_TK), lambda i: (i, 0, 0)),
                   pl.BlockSpec((tm, B_W), row), pl.BlockSpec((tm, B_W), row),
                   pl.BlockSpec((tm // A_TK, vb_rows, A_TK), lambda i: (i, 0, 0))),
        out_shape=out_shape,
        compiler_params=pltpu.CompilerParams(dimension_semantics=("parallel",), vmem_limit_bytes=VMEM_LIMIT),
        name="in_even",
    )(x2, g, w, gq, gk, cos, sa, sb, bd)


def _split_pair(qp):
    low = lax.broadcasted_iota(jnp.int32, qp.shape, 1) < HEAD_DIM
    zero = jnp.zeros(qp.shape, qp.dtype)
    return jnp.concatenate([jnp.where(low, qp, zero), jnp.where(low, zero, qp)], axis=0)


def _finish_pair(acc_lo, acc_hi):
    halves = [a[:HEAD_DIM] / a[HEAD_DIM:HEAD_DIM + 1] for a in (acc_lo, acc_hi)]
    return jnp.concatenate(halves, axis=0).T.astype(BF16)


def _mix_a_kernel(q_ref, k_ref, vt_ref, o_ref, qs_ref, m_ref, acc_ref, *, n_chunks):
    tq = A_TQ
    n_pairs = A_Q // 128
    for c in range(n_pairs):
        qs_ref[:, 2 * c * tq:(2 * c + 2) * tq] = (
            _split_pair(q_ref[:, c * 128:(c + 1) * 128]).astype(F32).T.astype(BF16))
    m_ref[...] = jnp.full(m_ref.shape, NEG, F32)
    acc_ref[...] = jnp.zeros(acc_ref.shape, F32)

    def scores(ci, h):
        k = k_ref[pl.ds(pl.multiple_of(ci * A_TK, A_TK), A_TK), :]
        return jnp.dot(k, qs_ref[:, h * tq:(h + 1) * tq], preferred_element_type=F32)

    def chunks(it, carry):
        items = [(it * A_UNROLL + u, h) for u in range(A_UNROLL) for h in range(2 * n_pairs)]
        ahead = [scores(*item) for item in items[:A_AHEAD]]
        for t, (ci, h) in enumerate(items):
            s = ahead.pop(0)
            if t + A_AHEAD < len(items):
                ahead.append(scores(*items[t + A_AHEAD]))
            cols = slice(h * tq, (h + 1) * tq)
            g = h % 2
            m_old = m_ref[:, cols]
            m_new = jnp.maximum(m_old, jnp.max(s, axis=0, keepdims=True))
            alpha = jnp.exp2(m_old - m_new)
            p = jnp.exp2(s - m_new).astype(BF16)
            m_ref[:, cols] = m_new
            vt = vt_ref[ci, g * V_ROWS:(g + 1) * V_ROWS, :]
            acc_ref[h] = alpha * acc_ref[h] + jnp.dot(vt, p, preferred_element_type=F32)
        return carry

    lax.fori_loop(0, n_chunks // A_UNROLL, chunks, 0)

    for c in range(n_pairs):
        o_ref[:, c * 128:(c + 1) * 128] = _finish_pair(acc_ref[2 * c], acc_ref[2 * c + 1])


def _mix_a(qa, ka, vat, bn, seq):
    n = qa.shape[0]
    nq = seq // A_TQ
    nc = seq // A_TK
    return pl.pallas_call(
        functools.partial(_mix_a_kernel, n_chunks=nc),
        grid=(bn, nq),
        in_specs=[pl.BlockSpec((A_TQ, A_Q), lambda b, i: (b * nq + i, 0)),
                  pl.BlockSpec((seq, A_KV), lambda b, i: (b, 0)),
                  pl.BlockSpec((nc, vat.shape[1], A_TK), lambda b, i: (b, 0, 0))],
        out_specs=pl.BlockSpec((A_TQ, A_Q), lambda b, i: (b * nq + i, 0)),
        out_shape=jax.ShapeDtypeStruct((n, A_Q), BF16),
        scratch_shapes=[pltpu.VMEM((128, 8 * A_TQ), BF16), pltpu.VMEM((1, 8 * A_TQ), F32),
                        pltpu.VMEM((8, V_ROWS, A_TQ), F32)],
        compiler_params=pltpu.CompilerParams(dimension_semantics=("parallel", "parallel"),
                                             vmem_limit_bytes=VMEM_LIMIT),
        name="mix_a",
    )(qa, ka, vat)


def _mix_b_kernel(q_ref, k0_ref, k1_ref, k2_ref, vt0_ref, vt1_ref, vt2_ref, bias_ref, o_ref,
                  qs_ref, m_ref, acc_ref):
    tq = B_TQ
    n_pairs = B_W // 128
    k_refs = (k0_ref, k1_ref, k2_ref)
    vt_refs = (vt0_ref, vt1_ref, vt2_ref)
    for c in range(n_pairs):
        qs_ref[:, 2 * c * tq:(2 * c + 2) * tq] = (
            _split_pair(q_ref[:, c * 128:(c + 1) * 128]).astype(F32).T.astype(BF16))

    def scores(h, j):
        c = h // 2
        return jnp.dot(k_refs[j][:, c * 128:(c + 1) * 128], qs_ref[:, h * tq:(h + 1) * tq],
                       preferred_element_type=F32)

    items = [(h, j) for h in range(2 * n_pairs) for j in range(len(k_refs))]
    ahead = [scores(*item) for item in items[:B_AHEAD]]
    for t, (h, j) in enumerate(items):
        c, hh = divmod(h, 2)
        s = bias_ref[0, c, j * tq:(j + 1) * tq, hh * tq:(hh + 1) * tq] + ahead.pop(0)
        if t + B_AHEAD < len(items):
            ahead.append(scores(*items[t + B_AHEAD]))
        rows = slice(h * V_ROWS, (h + 1) * V_ROWS)
        m_new = jnp.max(s, axis=0, keepdims=True)
        if j > 0:
            m_old = m_ref[h]
            m_new = jnp.maximum(m_old, m_new)
        pv = jnp.dot(vt_refs[j][rows, :], jnp.exp2(s - m_new).astype(BF16), preferred_element_type=F32)
        acc_ref[h] = pv if j == 0 else jnp.exp2(m_old - m_new) * acc_ref[h] + pv
        m_ref[h] = m_new
        if j == len(k_refs) - 1 and hh == 1:
            o_ref[:, c * 128:(c + 1) * 128] = _finish_pair(acc_ref[h - 1], acc_ref[h])


def _mix_b(qb, kb, vbt, bias, bn, seq):
    n = qb.shape[0]
    nb = seq // B_TQ
    q_map = lambda b, i: (b * nb + i, 0)

    def kv_map(j):
        return lambda b, i: (b * nb + jnp.clip(i - 1, 0, nb - 3) + j, 0)

    def vt_map(j):
        return lambda b, i: (b * nb + jnp.clip(i - 1, 0, nb - 3) + j, 0, 0)

    def bias_map(b, i):
        return (jnp.where(i == 0, 0, jnp.where(i == nb - 1, 2, 1)), 0, 0, 0)

    blk = lambda m: pl.BlockSpec((B_TQ, B_W), m)
    vblk = lambda m: pl.BlockSpec((None, vbt.shape[1], B_TQ), m)
    return pl.pallas_call(
        _mix_b_kernel,
        grid=(bn, nb),
        in_specs=[blk(q_map), blk(kv_map(0)), blk(kv_map(1)), blk(kv_map(2)),
                  vblk(vt_map(0)), vblk(vt_map(1)), vblk(vt_map(2)),
                  pl.BlockSpec((1, B_W // 128, B_TK, 2 * B_TQ), bias_map)],
        out_specs=blk(q_map),
        out_shape=jax.ShapeDtypeStruct((n, B_W), BF16),
        scratch_shapes=[pltpu.VMEM((128, B_HEADS * B_TQ), BF16), pltpu.VMEM((B_HEADS, 1, B_TQ), F32),
                        pltpu.VMEM((B_HEADS, V_ROWS, B_TQ), F32)],
        compiler_params=pltpu.CompilerParams(dimension_semantics=("parallel", "parallel"),
                                             vmem_limit_bytes=VMEM_LIMIT),
        name="mix_b",
    )(qb, kb, kb, kb, vbt, vbt, vbt, bias)


def _neighbourhood_bias(rpb, rows):
    n_dc = 2 * NA_COLS - 1
    c = np.arange(GRID_W)
    cs = np.clip(c - NA_COLS // 2, 0, GRID_W - NA_COLS)
    col_ok = (c[:, None] >= cs[None, :]) & (c[:, None] < cs[None, :] + NA_COLS)
    dc = np.clip(c[:, None] - c[None, :] + NA_COLS - 1, 0, n_dc - 1)
    pick = (dc.reshape(-1)[None, :] == np.arange(n_dc)[:, None]).astype(np.float32)
    by_col = jnp.einsum("hrd,dx->hrx", rpb.astype(F32) * LOG2E, jnp.asarray(pick),
                        precision=lax.Precision.HIGHEST)
    by_col = by_col.reshape(B_HEADS, 2 * NA_ROWS - 1, GRID_W, GRID_W)
    by_col = jnp.where(col_ok[None, None], by_col, NEG)
    masked = jnp.full((B_HEADS, GRID_W, GRID_W), NEG, F32)
    tables = []
    for r0, ws in ((0, 0), (4, 0), (rows - 4, rows - 12)):
        key_rows = []
        for kr in range(B_TK // GRID_W):
            blocks = []
            for rr in range(B_TQ // GRID_W):
                r = r0 + rr
                rs = min(max(r - NA_ROWS // 2, 0), rows - NA_ROWS)
                ok = rs <= ws + kr < rs + NA_ROWS
                blocks.append(by_col[:, ws + kr - r + NA_ROWS - 1] if ok else masked)
            key_rows.append(jnp.concatenate(blocks, axis=-1))
        t = jnp.concatenate(key_rows, axis=-2)
        t = t.reshape(B_HEADS // 2, 2, B_TK, B_TQ).transpose(0, 2, 1, 3).reshape(B_HEADS // 2, B_TK, 2 * B_TQ)
        tables.append(t)
    return jnp.stack(tables, axis=0)


def _ffn(x1, gain, wg_ref, wu_ref, wd_ref):
    hn = _rms(x1, gain).astype(BF16)
    acc = None
    for c in range(D_FF // FF_CHUNK):
        cs = slice(c * FF_CHUNK, (c + 1) * FF_CHUNK)
        gate = jnp.dot(hn, wg_ref[:, cs], preferred_element_type=F32)
        up = jnp.dot(hn, wu_ref[:, cs], preferred_element_type=F32)
        act = (gate * (1.0 / (1.0 + jnp.exp(-gate))) * up).astype(BF16)
        d = jnp.dot(act, wd_ref[cs, :], preferred_element_type=F32)
        acc = d if acc is None else acc + d
    return x1 + acc


def _post_even_kernel(x_ref, oa_ref, ob_ref, woa_ref, wob_ref, g_ref, wg_ref, wu_ref, wd_ref, y_ref):
    mix = (jnp.dot(oa_ref[...], woa_ref[...], preferred_element_type=F32)
           + jnp.dot(ob_ref[...], wob_ref[...], preferred_element_type=F32))
    y_ref[...] = _ffn(x_ref[...] + mix, g_ref[...], wg_ref, wu_ref, wd_ref)


def _post_even(x2, oa, ob, woa, wob, g, wg, wu, wd):
    n = x2.shape[0]
    tm = ROW_TILE
    row = lambda i: (i, 0)
    return pl.pallas_call(
        _post_even_kernel,
        grid=(n // tm,),
        in_specs=[pl.BlockSpec((tm, D_MODEL), row), pl.BlockSpec((tm, A_Q), row), pl.BlockSpec((tm, B_W), row),
                  _resident(woa.shape), _resident(wob.shape), _resident((1, D_MODEL)),
                  _resident(wg.shape), _resident(wu.shape), _resident(wd.shape)],
        out_specs=pl.BlockSpec((tm, D_MODEL), row),
        out_shape=jax.ShapeDtypeStruct((n, D_MODEL), F32),
        compiler_params=pltpu.CompilerParams(dimension_semantics=("parallel",), vmem_limit_bytes=VMEM_LIMIT),
        name="post_even",
    )(x2, oa, ob, woa, wob, g, wg, wu, wd)


def _post_odd_kernel(x_ref, o1_ref, of2_ref, of3_ref, l1_ref, l2_ref, l3_ref, ex_ref, wo_ref, g_ref,
                     wg_ref, wu_ref, wd_ref, gf_ref, y_ref, o2_ref, o3_ref):
    tm = x_ref.shape[0]
    n_cb = C_W // 128
    for (_, dil), src, dst in zip(C_GROUPS[1:], (of2_ref, of3_ref), (o2_ref, o3_ref)):
        for rho in range(dil):
            for cb in range(n_cb):
                dst[cb, pl.ds(rho, tm // dil, stride=dil), :] = src[0, rho, :, cb * 128:(cb + 1) * 128].astype(F32)
    o2 = jnp.concatenate([o2_ref[cb] for cb in range(n_cb)], axis=1)
    o3 = jnp.concatenate([o3_ref[cb] for cb in range(n_cb)], axis=1)
    l1, l2, l3 = l1_ref[...], l2_ref[...], l3_ref[...]
    mx = jnp.maximum(jnp.maximum(l1, l2), l3)
    e1, e2, e3 = jnp.exp2(l1 - mx), jnp.exp2(l2 - mx), jnp.exp2(l3 - mx)
    den = e1 + e2 + e3
    ex = ex_ref[...]

    def widen(w):
        hi = w.astype(BF16)
        lo = (w - hi.astype(F32)).astype(BF16)
        return jnp.dot(hi, ex, preferred_element_type=F32) + jnp.dot(lo, ex, preferred_element_type=F32)

    o = o3 + widen(e1 / den) * (o1_ref[...].astype(F32) - o3) + widen(e2 / den) * (o2 - o3)
    mix = jnp.dot(o.astype(BF16), wo_ref[...], preferred_element_type=F32)
    y = _ffn(x_ref[...] + mix, g_ref[...], wg_ref, wu_ref, wd_ref)
    y_ref[...] = _rms(y, gf_ref[...])


def _post_odd(x2, seq, os_, ls_, ex, wo, g, wg, wu, wd, gf):
    n = x2.shape[0]
    tm = ROW_TILE
    nt = seq // tm
    row = lambda i: (i, 0)
    wide = pl.BlockSpec((tm, D_MODEL), row)
    narrow = pl.BlockSpec((tm, C_HEADS), row)
    folded = [pl.BlockSpec((1, dil, tm // dil, C_W), lambda i: (i // nt, 0, i % nt, 0)) for _, dil in C_GROUPS[1:]]
    os_ = [os_[0].reshape(n, C_W)] + [o.reshape(n // seq, dil, seq // dil, C_W)
                                      for o, (_, dil) in zip(os_[1:], C_GROUPS[1:])]
    return pl.pallas_call(
        _post_odd_kernel,
        grid=(n // tm,),
        in_specs=[wide, wide, *folded, narrow, narrow, narrow, _resident(ex.shape), _resident(wo.shape),
                  _resident((1, D_MODEL)), _resident(wg.shape), _resident(wu.shape), _resident(wd.shape),
                  _resident((1, D_MODEL))],
        out_specs=wide,
        out_shape=jax.ShapeDtypeStruct((n, D_MODEL), F32),
        scratch_shapes=[pltpu.VMEM((C_W // 128, tm, 128), F32), pltpu.VMEM((C_W // 128, tm, 128), F32)],
        compiler_params=pltpu.CompilerParams(dimension_semantics=("parallel",), vmem_limit_bytes=VMEM_LIMIT),
        name="post_odd",
    )(x2, *os_, *ls_, ex, wo, g, wg, wu, wd, gf)


def _in_odd_kernel(x_ref, g_ref, w_ref, *refs):
    n_slabs = 3 * C_W // IN_SLAB
    out_refs, p_refs, f_refs = refs[:-4], refs[-4:-2], refs[-2:]
    tm = x_ref.shape[0]
    assert [d for _, d in C_GROUPS] == [1, 4, 16]
    hn = _rms(x_ref[...], g_ref[...]).astype(BF16)

    def project(slab):
        cols = slice(slab * IN_SLAB, (slab + 1) * IN_SLAB)
        return jnp.dot(hn, w_ref[:, cols], preferred_element_type=F32)

    ahead = [project(s) for s in range(IN_AHEAD)]
    for slab in range(n_slabs):
        p = ahead.pop(0)
        if slab + IN_AHEAD < n_slabs:
            ahead.append(project(slab + IN_AHEAD))
        part, col0 = divmod(slab * IN_SLAB, C_W)
        if part == 0:
            p = p * (SCALE * LOG2E)
        p_ref, f_ref = p_refs[slab % 2], f_refs[slab % 2]
        dst1, dst4, dst16 = out_refs[part * 3:part * 3 + 3]
        dst1[0, 0, :, col0:col0 + IN_SLAB] = p.astype(BF16)
        for cb in range(IN_SLAB // 128):
            cols = slice(col0 + cb * 128, col0 + (cb + 1) * 128)
            p_ref[cb] = p[:, cb * 128:(cb + 1) * 128]
            for r4 in range(4):
                by4 = p_ref[cb, pl.ds(r4, tm // 4, stride=4), :]
                dst4[0, r4, :, cols] = by4.astype(BF16)
                f_ref[cb, r4] = by4
                for j in range(4):
                    by16 = f_ref[cb, r4, pl.ds(j, tm // 16, stride=4), :]
                    dst16[0, r4 + 4 * j, :, cols] = by16.astype(BF16)


def _in_odd(x2, bn, seq, g, w):
    n = x2.shape[0]
    tm = ROW_TILE
    nt = seq // tm
    specs, shapes = [], []
    for _ in range(3):
        for _, dil in C_GROUPS:
            specs.append(pl.BlockSpec((1, dil, tm // dil, C_W), lambda i: (i // nt, 0, i % nt, 0)))
            shapes.append(jax.ShapeDtypeStruct((bn, dil, seq // dil, C_W), BF16))
    outs = pl.pallas_call(
        _in_odd_kernel,
        grid=(n // tm,),
        in_specs=[pl.BlockSpec((tm, D_MODEL), lambda i: (i, 0)), _resident((1, D_MODEL)), _resident(w.shape)],
        out_specs=tuple(specs),
        out_shape=tuple(shapes),
        scratch_shapes=[pltpu.VMEM((IN_SLAB // 128, tm, 128), F32)] * 2
        + [pltpu.VMEM((IN_SLAB // 128, 4, tm // 4, 128), F32)] * 2,
        compiler_params=pltpu.CompilerParams(dimension_semantics=("parallel",), vmem_limit_bytes=VMEM_LIMIT),
        name="in_odd",
    )(x2, g, w)
    outs = [a.reshape(a.shape[0] * a.shape[1], a.shape[2], C_W) for a in outs]
    n_g = len(C_GROUPS)
    return outs[:n_g], outs[n_g:2 * n_g], outs[2 * n_g:]


def _band_kernel(sl_ref, q_ref, kp_ref, kc_ref, kn_ref, vp_ref, vc_ref, vn_ref, o_ref, lse_ref,
                 kwin_ref, vt_ref, pen_ref, *, length, tile):
    t = pl.program_id(1)
    n_pairs = C_W // 128
    win = C_TQ + 2 * C_HALF
    span = tile + 2 * C_HALF
    kwin_ref[0:C_HALF, :] = kp_ref[C_TQ - C_HALF:C_TQ, :]
    kwin_ref[C_HALF:C_HALF + tile, :] = kc_ref[...]
    kwin_ref[C_HALF + tile:span, :] = kn_ref[0:C_HALF, :]

    @pl.when((pl.program_id(0) == 0) & (t == 0))
    def _():
        key = lax.broadcasted_iota(jnp.int32, (win, 2 * C_TQ), 0)
        lane = lax.broadcasted_iota(jnp.int32, (win, 2 * C_TQ), 1)
        dist = jnp.abs(key - C_HALF - (lane & (C_TQ - 1)))
        distf = dist.astype(F32)
        for c in range(n_pairs):
            slope = jnp.where(lane < C_TQ, sl_ref[2 * c], sl_ref[2 * c + 1])
            pen = jnp.where(dist <= C_HALF, slope * distf, -NEG)
            pen_ref[0, c] = pen
            pen_ref[1, c] = jnp.where(key >= C_HALF, pen, -NEG)
            pen_ref[2, c] = jnp.where(key < win - C_HALF, pen, -NEG)

    ones = jnp.ones((V_ROWS - HEAD_DIM, span), BF16)
    for c in range(n_pairs):
        cs = slice(c * 128, (c + 1) * 128)
        vwin = jnp.concatenate([vp_ref[C_TQ - C_HALF:C_TQ, cs], vc_ref[:, cs], vn_ref[0:C_HALF, cs]], axis=0)
        vt = vwin.astype(F32).T.astype(BF16)
        for hh in range(2):
            vt_ref[c, hh * V_ROWS:hh * V_ROWS + HEAD_DIM, :] = vt[hh * HEAD_DIM:(hh + 1) * HEAD_DIM, :]
            vt_ref[c, hh * V_ROWS + HEAD_DIM:(hh + 1) * V_ROWS, :] = ones

    def scores(j, c):
        cs = slice(c * 128, (c + 1) * 128)
        qs = _split_pair(q_ref[j * C_TQ:(j + 1) * C_TQ, cs])
        k = kwin_ref[j * C_TQ:j * C_TQ + win, cs]
        return lax.dot_general(k, qs, _NT, preferred_element_type=F32)

    n_blocks = tile // C_TQ
    assert n_blocks >= 2
    table = {0: jnp.where(t == 0, 1, 0), n_blocks - 1: jnp.where(t == length // tile - 1, 2, 0)}
    items = [(j, c) for j in range(n_blocks) for c in range(n_pairs)]
    ahead = [scores(*item) for item in items[:C_AHEAD]]
    for n_item, (j, c) in enumerate(items):
        rows = slice(j * C_TQ, (j + 1) * C_TQ)
        cs = slice(c * 128, (c + 1) * 128)
        s = ahead.pop(0) - pen_ref[table.get(j, 0), c]
        if n_item + C_AHEAD < len(items):
            ahead.append(scores(*items[n_item + C_AHEAD]))
        m = jnp.max(s, axis=0, keepdims=True)
        p = jnp.exp2(s - m).astype(BF16)
        acc = jnp.dot(vt_ref[c, :, j * C_TQ:j * C_TQ + win], p, preferred_element_type=F32)
        a0, a1 = acc[:V_ROWS, :C_TQ], acc[V_ROWS:, C_TQ:]
        o_ref[rows, cs] = _finish_pair(a0, a1)
        lse_ref[2 * c:2 * c + 1, rows] = m[:, :C_TQ] + jnp.log2(a0[HEAD_DIM:HEAD_DIM + 1])
        lse_ref[2 * c + 1:2 * c + 2, rows] = m[:, C_TQ:] + jnp.log2(a1[HEAD_DIM:HEAD_DIM + 1])


def _band(slopes, q, k, v):
    nb_, length, _ = q.shape
    tile = 512 if length % 512 == 0 else 256
    per = tile // C_TQ
    last = length // C_TQ - 1
    cur = lambda n, t: (n, t, 0)
    prev = lambda n, t: (n, jnp.maximum(t * per - 1, 0), 0)
    nxt = lambda n, t: (n, jnp.minimum((t + 1) * per, last), 0)
    big = lambda m: pl.BlockSpec((None, tile, C_W), m)
    halo = lambda m: pl.BlockSpec((None, C_TQ, C_W), m)
    span = tile + 2 * C_HALF
    return pl.pallas_call(
        functools.partial(_band_kernel, length=length, tile=tile),
        grid=(nb_, length // tile),
        in_specs=[pl.BlockSpec(memory_space=pltpu.SMEM), big(cur),
                  halo(prev), big(cur), halo(nxt), halo(prev), big(cur), halo(nxt)],
        out_specs=(big(cur), pl.BlockSpec((None, C_HEADS, tile), lambda n, t: (n, 0, t))),
        out_shape=(jax.ShapeDtypeStruct((nb_, length, C_W), BF16),
                   jax.ShapeDtypeStruct((nb_, C_HEADS, length), F32)),
        scratch_shapes=[pltpu.VMEM((span, C_W), BF16), pltpu.VMEM((C_W // 128, 2 * V_ROWS, span), BF16),
                        pltpu.VMEM((3, C_W // 128, C_TQ + 2 * C_HALF, 2 * C_TQ), F32)],
        compiler_params=pltpu.CompilerParams(dimension_semantics=("arbitrary", "arbitrary"),
                                             vmem_limit_bytes=VMEM_LIMIT),
        name="band",
    )(slopes, q, k, k, k, v, v, v)


def _rope_tables(seq):
    t = jnp.arange(seq)
    n = HEAD_DIM // 4
    freqs = jnp.power(ROPE_THETA, -jnp.arange(n, dtype=F32) / n)
    ang_r = (t // GRID_W).astype(F32)[:, None] * freqs[None, :]
    ang_c = (t % GRID_W).astype(F32)[:, None] * freqs[None, :]
    zero = jnp.zeros_like(ang_r)
    cos = jnp.concatenate([jnp.cos(ang_r)] * 2 + [jnp.cos(ang_c)] * 2, axis=-1)
    sin_r, sin_c = jnp.sin(ang_r), jnp.sin(ang_c)
    sa = jnp.concatenate([-sin_r, zero, -sin_c, zero], axis=-1)
    sb = jnp.concatenate([zero, sin_r, zero, sin_c], axis=-1)
    two = lambda a: jnp.concatenate([a, a], axis=-1)
    return two(cos), two(sa), two(sb)


def _trunk(x, prm):
    bn, seq, _ = x.shape
    n = bn * seq
    x2 = x.reshape(n, D_MODEL)
    cos, sa, sb = _rope_tables(seq)
    qa, ka, vat, qb, kb, vbt = _in_even(x2, seq, prm["g_mix0"], prm["w_in_even"], prm["gq"], prm["gk"],
                                        cos, sa, sb, prm["bd"])
    oa = _mix_a(qa, ka, vat, bn, seq)
    ob = _mix_b(qb, kb, vbt, _neighbourhood_bias(prm["rpb"], seq // GRID_W), bn, seq)
    x2 = _post_even(x2, oa, ob, prm["wo_a"], prm["wo_b"], prm["g_ffn0"], prm["wg0"], prm["wu0"], prm["wd0"])

    qs, ks, vs = _in_odd(x2, bn, seq, prm["g_mix1"], prm["w_in_odd"])
    outs, lses = [], []
    for (_, dil), q, k, v in zip(C_GROUPS, qs, ks, vs):
        o, lse = _band(prm["slopes"] * dil * LOG2E, q, k, v)
        outs.append(o)
        lses.append(lse.reshape(bn, dil, C_HEADS, seq // dil).transpose(0, 3, 1, 2).reshape(n, C_HEADS))
    y = _post_odd(x2, seq, outs, lses, prm["expand"], prm["wo_odd"], prm["g_ffn1"], prm["wg1"], prm["wu1"],
                  prm["wd1"], prm["g_final"])
    return y.reshape(bn, seq, D_MODEL)


def kernel(x_prompt, x_sample, norm_mix, norm_ffn, norm_final, w_in_even, a_q_norm, a_k_norm, na_rpb,
           w_out_even, w_in_odd, w_out_odd, w_gate_up, w_down):
    head = np.arange(128) // HEAD_DIM
    order = np.array([0, 4, 1, 5, 2, 6, 3, 7])
    w_in0 = w_in_even[0]
    w_qa = w_in0[:, :A_Q].reshape(D_MODEL, A_Q // HEAD_DIM, HEAD_DIM)[:, order].reshape(D_MODEL, A_Q)
    w_in0 = jnp.concatenate([w_qa, w_in0[:, A_Q:]], axis=1)
    wo_a = w_out_even[0, :A_Q].reshape(A_Q // HEAD_DIM, HEAD_DIM, D_MODEL)[order].reshape(A_Q, D_MODEL)
    prm = {
        "g_mix0": norm_mix[0][None].astype(F32), "g_mix1": norm_mix[1][None].astype(F32),
        "g_ffn0": norm_ffn[0][None].astype(F32), "g_ffn1": norm_ffn[1][None].astype(F32),
        "g_final": norm_final[None].astype(F32),
        "w_in_even": w_in0.astype(BF16), "w_in_odd": w_in_odd[0].astype(BF16),
        "gq": jnp.tile(a_q_norm[0].astype(F32) * (SCALE * LOG2E), 2)[None], "gk": jnp.tile(a_k_norm[0].astype(F32), 2)[None],
        "bd": jnp.asarray((head[:, None] == head[None, :]) / HEAD_DIM, BF16),
        "rpb": na_rpb[0],
        "wo_a": wo_a.astype(BF16), "wo_b": w_out_even[0, A_Q:].astype(BF16),
        "wo_odd": w_out_odd[0].astype(BF16),
        "wg0": w_gate_up[0, :, :D_FF].astype(BF16), "wu0": w_gate_up[0, :, D_FF:].astype(BF16),
        "wg1": w_gate_up[1, :, :D_FF].astype(BF16), "wu1": w_gate_up[1, :, D_FF:].astype(BF16),
        "wd0": w_down[0].astype(BF16), "wd1": w_down[1].astype(BF16),
        "slopes": jnp.exp2(-8.0 * (jnp.arange(C_HEADS, dtype=F32) + 1.0) / C_HEADS),
        "expand": jnp.asarray(np.arange(C_HEADS)[:, None] == (np.arange(C_W) // HEAD_DIM)[None, :], BF16),
    }
    return (_trunk(x_prompt, prm), _trunk(x_sample, prm))
```

```python
import functools

import jax
import jax.numpy as jnp
import numpy as np
from jax import lax
from jax.experimental import pallas as pl
from jax.experimental.pallas import tpu as pltpu

F32 = jnp.float32
BF16 = jnp.bfloat16

D_MODEL = 1024
HEAD_DIM = 64
A_Q = 512
A_KV = 128
B_W = 512
C_W = 1024
C_HEADS = 16
B_HEADS = 8
GRID_W = 64
NA_ROWS = 8
NA_COLS = 16
C_GROUPS = ((128, 1), (512, 4), (2048, 16))
ROPE_THETA = 10000.0
D_FF = 2816
FF_CHUNK = 1408
EPS = 1e-6
NEG = -1e30
SCALE = HEAD_DIM ** -0.5
LOG2E = 1.4426950408889634

V_ROWS = HEAD_DIM + 16
ROW_TILE = 512
A_TQ = 256
A_TK = 256
A_UNROLL = 16
A_AHEAD = 5
B_TQ = 256
B_TK = 3 * B_TQ
B_AHEAD = 4
C_TQ = 128
C_HALF = 64
IN_SLAB = 256
IN_AHEAD = 2
C_AHEAD = 5
VMEM_LIMIT = 56 * 1024 * 1024

_NT = (((1,), (1,)), ((), ()))


def _resident(shape):
    zeros = (0,) * len(shape)
    return pl.BlockSpec(shape, lambda *_: zeros, pipeline_mode=pl.Buffered(1))


def _rms(x, gain):
    ms = jnp.mean(x * x, axis=-1, keepdims=True)
    return x * lax.rsqrt(ms + EPS) * gain


def _in_even_kernel(x_ref, g_ref, w_ref, gq_ref, gk_ref, cos_ref, sa_ref, sb_ref, bd_ref,
                    qa_ref, ka_ref, vat_ref, qb_ref, kb_ref, vbt_ref):
    hn = _rms(x_ref[...], g_ref[...]).astype(BF16)
    tm = hn.shape[0]
    cos, sa, sb, bd = cos_ref[...], sa_ref[...], sb_ref[...], bd_ref[...]

    def project(slab):
        cols = slice(slab * IN_SLAB, (slab + 1) * IN_SLAB)
        return jnp.dot(hn, w_ref[:, cols], preferred_element_type=F32)

    def headnorm_rope(c, gain):
        c2 = c * c
        hi = c2.astype(BF16)
        lo = (c2 - hi.astype(F32)).astype(BF16)
        ms = (jnp.dot(hi, bd, preferred_element_type=F32)
              + jnp.dot(lo, bd, preferred_element_type=F32))
        y = c * lax.rsqrt(ms + EPS) * gain
        return y * cos + pltpu.roll(y, 112, 1) * sa + pltpu.roll(y, 16, 1) * sb

    def put_transposed(dst_ref, chunk, c):
        ones = jnp.ones((V_ROWS - HEAD_DIM, A_TK), BF16)
        for j in range(tm // A_TK):
            t = chunk[j * A_TK:(j + 1) * A_TK, :].astype(BF16).T
            for hh in range(2):
                r0 = (2 * c + hh) * V_ROWS
                dst_ref[j, r0:r0 + HEAD_DIM, :] = t[hh * HEAD_DIM:(hh + 1) * HEAD_DIM, :]
                dst_ref[j, r0 + HEAD_DIM:r0 + V_ROWS, :] = ones

    def write_out(chunk, col):
        lanes = lambda base: slice(col - base, col - base + 128)
        o = A_Q + 2 * A_KV
        if col < A_Q:
            qa_ref[:, lanes(0)] = headnorm_rope(chunk, gq_ref[...]).astype(BF16)
        elif col < A_Q + A_KV:
            ka_ref[...] = headnorm_rope(chunk, gk_ref[...]).astype(BF16)
        elif col < o:
            put_transposed(vat_ref, chunk, 0)
        elif col < o + B_W:
            qb_ref[:, lanes(o)] = (chunk * (SCALE * LOG2E)).astype(BF16)
        elif col < o + 2 * B_W:
            kb_ref[:, lanes(o + B_W)] = chunk.astype(BF16)
        else:
            put_transposed(vbt_ref, chunk, (col - o - 2 * B_W) // 128)

    n_slabs = w_ref.shape[1] // IN_SLAB
    ahead = [project(s) for s in range(IN_AHEAD)]
    for slab in range(n_slabs):
        p = ahead.pop(0)
        if slab + IN_AHEAD < n_slabs:
            ahead.append(project(slab + IN_AHEAD))
        for i in range(IN_SLAB // 128):
            write_out(p[:, i * 128:(i + 1) * 128], slab * IN_SLAB + i * 128)


def _in_even(x2, seq, g, w, gq, gk, cos, sa, sb, bd):
    n = x2.shape[0]
    tm = ROW_TILE
    nt = seq // tm
    row = lambda i: (i, 0)
    pos = lambda i: (i % nt, 0)
    va_rows, vb_rows = (A_KV // HEAD_DIM) * V_ROWS, B_HEADS * V_ROWS
    out_shape = (
        jax.ShapeDtypeStruct((n, A_Q), BF16), jax.ShapeDtypeStruct((n, A_KV), BF16),
        jax.ShapeDtypeStruct((n // A_TK, va_rows, A_TK), BF16),
        jax.ShapeDtypeStruct((n, B_W), BF16), jax.ShapeDtypeStruct((n, B_W), BF16),
        jax.ShapeDtypeStruct((n // A_TK, vb_rows, A_TK), BF16))
    return pl.pallas_call(
        _in_even_kernel,
        grid=(n // tm,),
        in_specs=[pl.BlockSpec((tm, D_MODEL), row), _resident((1, D_MODEL)), _resident(w.shape),
                  _resident((1, 128)), _resident((1, 128)),
                  pl.BlockSpec((tm, 128), pos), pl.BlockSpec((tm, 128), pos), pl.BlockSpec((tm, 128), pos),
                  _resident((128, 128))],
        out_specs=(pl.BlockSpec((tm, A_Q), row), pl.BlockSpec((tm, A_KV), row),
                   pl.BlockSpec((tm // A_TK, va_rows, A_TK), lambda i: (i, 0, 0)),
                   pl.BlockSpec((tm, B_W), row), pl.BlockSpec((tm, B_W), row),
                   pl.BlockSpec((tm // A_TK, vb_rows, A_TK), lambda i: (i, 0, 0))),
        out_shape=out_shape,
        compiler_params=pltpu.CompilerParams(dimension_semantics=("parallel",), vmem_limit_bytes=VMEM_LIMIT),
        name="in_even",
    )(x2, g, w, gq, gk, cos, sa, sb, bd)


def _split_pair(qp):
    low = lax.broadcasted_iota(jnp.int32, qp.shape, 1) < HEAD_DIM
    zero = jnp.zeros(qp.shape, qp.dtype)
    return jnp.concatenate([jnp.where(low, qp, zero), jnp.where(low, zero, qp)], axis=0)


def _finish_pair(acc_lo, acc_hi):
    halves = [a[:HEAD_DIM] / a[HEAD_DIM:HEAD_DIM + 1] for a in (acc_lo, acc_hi)]
    return jnp.concatenate(halves, axis=0).astype(BF16).T


def _mix_a_kernel(q_ref, k_ref, vt_ref, o_ref, qs_ref, m_ref, acc_ref, *, n_chunks):
    tq = A_TQ
    n_pairs = A_Q // 128
    for c in range(n_pairs):
        qs_ref[:, 2 * c * tq:(2 * c + 2) * tq] = (
            _split_pair(q_ref[:, c * 128:(c + 1) * 128]).T)
    m_ref[...] = jnp.full(m_ref.shape, NEG, F32)
    acc_ref[...] = jnp.zeros(acc_ref.shape, F32)

    def scores(ci, h):
        k = k_ref[pl.ds(pl.multiple_of(ci * A_TK, A_TK), A_TK), :]
        return jnp.dot(k, qs_ref[:, h * tq:(h + 1) * tq], preferred_element_type=F32)

    def chunks(it, carry):
        items = [(it * A_UNROLL + u, h) for u in range(A_UNROLL) for h in range(2 * n_pairs)]
        ahead = [scores(*item) for item in items[:A_AHEAD]]
        for t, (ci, h) in enumerate(items):
            s = ahead.pop(0)
            if t + A_AHEAD < len(items):
                ahead.append(scores(*items[t + A_AHEAD]))
            cols = slice(h * tq, (h + 1) * tq)
            g = h % 2
            m_old = m_ref[:, cols]
            m_new = jnp.maximum(m_old, jnp.max(s, axis=0, keepdims=True))
            alpha = jnp.exp2(m_old - m_new)
            p = jnp.exp2(s - m_new).astype(BF16)
            m_ref[:, cols] = m_new
            vt = vt_ref[ci, g * V_ROWS:(g + 1) * V_ROWS, :]
            acc_ref[h] = alpha * acc_ref[h] + jnp.dot(vt, p, preferred_element_type=F32)
        return carry

    lax.fori_loop(0, n_chunks // A_UNROLL, chunks, 0)

    for c in range(n_pairs):
        o_ref[:, c * 128:(c + 1) * 128] = _finish_pair(acc_ref[2 * c], acc_ref[2 * c + 1])


def _mix_a(qa, ka, vat, bn, seq):
    n = qa.shape[0]
    nq = seq // A_TQ
    nc = seq // A_TK
    return pl.pallas_call(
        functools.partial(_mix_a_kernel, n_chunks=nc),
        grid=(bn, nq),
        in_specs=[pl.BlockSpec((A_TQ, A_Q), lambda b, i: (b * nq + i, 0)),
                  pl.BlockSpec((seq, A_KV), lambda b, i: (b, 0)),
                  pl.BlockSpec((nc, vat.shape[1], A_TK), lambda b, i: (b, 0, 0))],
        out_specs=pl.BlockSpec((A_TQ, A_Q), lambda b, i: (b * nq + i, 0)),
        out_shape=jax.ShapeDtypeStruct((n, A_Q), BF16),
        scratch_shapes=[pltpu.VMEM((128, 8 * A_TQ), BF16), pltpu.VMEM((1, 8 * A_TQ), F32),
                        pltpu.VMEM((8, V_ROWS, A_TQ), F32)],
        compiler_params=pltpu.CompilerParams(dimension_semantics=("parallel", "parallel"),
                                             vmem_limit_bytes=VMEM_LIMIT),
        name="mix_a",
    )(qa, ka, vat)


def _mix_b_kernel(q_ref, k0_ref, k1_ref, k2_ref, vt0_ref, vt1_ref, vt2_ref, bias_ref, o_ref,
                  qs_ref, m_ref, acc_ref):
    tq = B_TQ
    n_pairs = B_W // 128
    k_refs = (k0_ref, k1_ref, k2_ref)
    vt_refs = (vt0_ref, vt1_ref, vt2_ref)
    for c in range(n_pairs):
        qs_ref[:, 2 * c * tq:(2 * c + 2) * tq] = (
            _split_pair(q_ref[:, c * 128:(c + 1) * 128]).T)

    def scores(h, j):
        c = h // 2
        return jnp.dot(k_refs[j][:, c * 128:(c + 1) * 128], qs_ref[:, h * tq:(h + 1) * tq],
                       preferred_element_type=F32)

    items = [(h, j) for h in range(2 * n_pairs) for j in range(len(k_refs))]
    ahead = [scores(*item) for item in items[:B_AHEAD]]
    for t, (h, j) in enumerate(items):
        c, hh = divmod(h, 2)
        s = bias_ref[0, c, j * tq:(j + 1) * tq, hh * tq:(hh + 1) * tq] + ahead.pop(0)
        if t + B_AHEAD < len(items):
            ahead.append(scores(*items[t + B_AHEAD]))
        rows = slice(h * V_ROWS, (h + 1) * V_ROWS)
        m_new = jnp.max(s, axis=0, keepdims=True)
        if j > 0:
            m_old = m_ref[h]
            m_new = jnp.maximum(m_old, m_new)
        pv = jnp.dot(vt_refs[j][rows, :], jnp.exp2(s - m_new).astype(BF16), preferred_element_type=F32)
        acc_ref[h] = pv if j == 0 else jnp.exp2(m_old - m_new) * acc_ref[h] + pv
        m_ref[h] = m_new
        if j == len(k_refs) - 1 and hh == 1:
            o_ref[:, c * 128:(c + 1) * 128] = _finish_pair(acc_ref[h - 1], acc_ref[h])


def _mix_b(qb, kb, vbt, bias, bn, seq):
    n = qb.shape[0]
    nb = seq // B_TQ
    q_map = lambda b, i: (b * nb + i, 0)

    def kv_map(j):
        return lambda b, i: (b * nb + jnp.clip(i - 1, 0, nb - 3) + j, 0)

    def vt_map(j):
        return lambda b, i: (b * nb + jnp.clip(i - 1, 0, nb - 3) + j, 0, 0)

    def bias_map(b, i):
        return (jnp.where(i == 0, 0, jnp.where(i == nb - 1, 2, 1)), 0, 0, 0)

    blk = lambda m: pl.BlockSpec((B_TQ, B_W), m)
    vblk = lambda m: pl.BlockSpec((None, vbt.shape[1], B_TQ), m)
    return pl.pallas_call(
        _mix_b_kernel,
        grid=(bn, nb),
        in_specs=[blk(q_map), blk(kv_map(0)), blk(kv_map(1)), blk(kv_map(2)),
                  vblk(vt_map(0)), vblk(vt_map(1)), vblk(vt_map(2)),
                  pl.BlockSpec((1, B_W // 128, B_TK, 2 * B_TQ), bias_map)],
        out_specs=blk(q_map),
        out_shape=jax.ShapeDtypeStruct((n, B_W), BF16),
        scratch_shapes=[pltpu.VMEM((128, B_HEADS * B_TQ), BF16), pltpu.VMEM((B_HEADS, 1, B_TQ), F32),
                        pltpu.VMEM((B_HEADS, V_ROWS, B_TQ), F32)],
        compiler_params=pltpu.CompilerParams(dimension_semantics=("parallel", "parallel"),
                                             vmem_limit_bytes=VMEM_LIMIT),
        name="mix_b",
    )(qb, kb, kb, kb, vbt, vbt, vbt, bias)


def _neighbourhood_bias(rpb, rows):
    n_dc = 2 * NA_COLS - 1
    c = np.arange(GRID_W)
    cs = np.clip(c - NA_COLS // 2, 0, GRID_W - NA_COLS)
    col_ok = (c[:, None] >= cs[None, :]) & (c[:, None] < cs[None, :] + NA_COLS)
    dc = np.clip(c[:, None] - c[None, :] + NA_COLS - 1, 0, n_dc - 1)
    pick = (dc.reshape(-1)[None, :] == np.arange(n_dc)[:, None]).astype(np.float32)
    by_col = jnp.einsum("hrd,dx->hrx", rpb.astype(F32) * LOG2E, jnp.asarray(pick),
                        precision=lax.Precision.HIGHEST)
    by_col = by_col.reshape(B_HEADS, 2 * NA_ROWS - 1, GRID_W, GRID_W)
    by_col = jnp.where(col_ok[None, None], by_col, NEG)
    masked = jnp.full((B_HEADS, GRID_W, GRID_W), NEG, F32)
    tables = []
    for r0, ws in ((0, 0), (4, 0), (rows - 4, rows - 12)):
        key_rows = []
        for kr in range(B_TK // GRID_W):
            blocks = []
            for rr in range(B_TQ // GRID_W):
                r = r0 + rr
                rs = min(max(r - NA_ROWS // 2, 0), rows - NA_ROWS)
                ok = rs <= ws + kr < rs + NA_ROWS
                blocks.append(by_col[:, ws + kr - r + NA_ROWS - 1] if ok else masked)
            key_rows.append(jnp.concatenate(blocks, axis=-1))
        t = jnp.concatenate(key_rows, axis=-2)
        t = t.reshape(B_HEADS // 2, 2, B_TK, B_TQ).transpose(0, 2, 1, 3).reshape(B_HEADS // 2, B_TK, 2 * B_TQ)
        tables.append(t)
    return jnp.stack(tables, axis=0)


def _ffn(x1, gain, wg_ref, wu_ref, wd_ref):
    hn = _rms(x1, gain).astype(BF16)
    acc = None
    for c in range(D_FF // FF_CHUNK):
        cs = slice(c * FF_CHUNK, (c + 1) * FF_CHUNK)
        gate = jnp.dot(hn, wg_ref[:, cs], preferred_element_type=F32)
        up = jnp.dot(hn, wu_ref[:, cs], preferred_element_type=F32)
        act = (gate * (1.0 / (1.0 + jnp.exp(-gate))) * up).astype(BF16)
        d = jnp.dot(act, wd_ref[cs, :], preferred_element_type=F32)
        acc = d if acc is None else acc + d
    return x1 + acc


def _post_even_kernel(x_ref, oa_ref, ob_ref, woa_ref, wob_ref, g_ref, wg_ref, wu_ref, wd_ref, y_ref):
    mix = (jnp.dot(oa_ref[...], woa_ref[...], preferred_element_type=F32)
           + jnp.dot(ob_ref[...], wob_ref[...], preferred_element_type=F32))
    y_ref[...] = _ffn(x_ref[...] + mix, g_ref[...], wg_ref, wu_ref, wd_ref)


def _post_even(x2, oa, ob, woa, wob, g, wg, wu, wd):
    n = x2.shape[0]
    tm = ROW_TILE
    row = lambda i: (i, 0)
    return pl.pallas_call(
        _post_even_kernel,
        grid=(n // tm,),
        in_specs=[pl.BlockSpec((tm, D_MODEL), row), pl.BlockSpec((tm, A_Q), row), pl.BlockSpec((tm, B_W), row),
                  _resident(woa.shape), _resident(wob.shape), _resident((1, D_MODEL)),
                  _resident(wg.shape), _resident(wu.shape), _resident(wd.shape)],
        out_specs=pl.BlockSpec((tm, D_MODEL), row),
        out_shape=jax.ShapeDtypeStruct((n, D_MODEL), F32),
        compiler_params=pltpu.CompilerParams(dimension_semantics=("parallel",), vmem_limit_bytes=VMEM_LIMIT),
        name="post_even",
    )(x2, oa, ob, woa, wob, g, wg, wu, wd)


def _post_odd_kernel(x_ref, o1_ref, of2_ref, of3_ref, l1_ref, l2_ref, l3_ref, ex_ref, wo_ref, g_ref,
                     wg_ref, wu_ref, wd_ref, gf_ref, y_ref, o2_ref, o3_ref):
    tm = x_ref.shape[0]
    n_cb = C_W // 128
    for (_, dil), src, dst in zip(C_GROUPS[1:], (of2_ref, of3_ref), (o2_ref, o3_ref)):
        for rho in range(dil):
            for cb in range(n_cb):
                dst[cb, pl.ds(rho, tm // dil, stride=dil), :] = src[0, rho, :, cb * 128:(cb + 1) * 128].astype(F32)
    o2 = jnp.concatenate([o2_ref[cb] for cb in range(n_cb)], axis=1)
    o3 = jnp.concatenate([o3_ref[cb] for cb in range(n_cb)], axis=1)
    l1, l2, l3 = l1_ref[...], l2_ref[...], l3_ref[...]
    mx = jnp.maximum(jnp.maximum(l1, l2), l3)
    e1, e2, e3 = jnp.exp2(l1 - mx), jnp.exp2(l2 - mx), jnp.exp2(l3 - mx)
    den = e1 + e2 + e3
    ex = ex_ref[...]

    def widen(w):
        hi = w.astype(BF16)
        lo = (w - hi.astype(F32)).astype(BF16)
        return jnp.dot(hi, ex, preferred_element_type=F32) + jnp.dot(lo, ex, preferred_element_type=F32)

    o = o3 + widen(e1 / den) * (o1_ref[...].astype(F32) - o3) + widen(e2 / den) * (o2 - o3)
    mix = jnp.dot(o.astype(BF16), wo_ref[...], preferred_element_type=F32)
    y = _ffn(x_ref[...] + mix, g_ref[...], wg_ref, wu_ref, wd_ref)
    y_ref[...] = _rms(y, gf_ref[...])


def _post_odd(x2, seq, os_, ls_, ex, wo, g, wg, wu, wd, gf):
    n = x2.shape[0]
    tm = ROW_TILE
    nt = seq // tm
    row = lambda i: (i, 0)
    wide = pl.BlockSpec((tm, D_MODEL), row)
    narrow = pl.BlockSpec((tm, C_HEADS), row)
    folded = [pl.BlockSpec((1, dil, tm // dil, C_W), lambda i: (i // nt, 0, i % nt, 0)) for _, dil in C_GROUPS[1:]]
    os_ = [os_[0].reshape(n, C_W)] + [o.reshape(n // seq, dil, seq // dil, C_W)
                                      for o, (_, dil) in zip(os_[1:], C_GROUPS[1:])]
    return pl.pallas_call(
        _post_odd_kernel,
        grid=(n // tm,),
        in_specs=[wide, wide, *folded, narrow, narrow, narrow, _resident(ex.shape), _resident(wo.shape),
                  _resident((1, D_MODEL)), _resident(wg.shape), _resident(wu.shape), _resident(wd.shape),
                  _resident((1, D_MODEL))],
        out_specs=wide,
        out_shape=jax.ShapeDtypeStruct((n, D_MODEL), F32),
        scratch_shapes=[pltpu.VMEM((C_W // 128, tm, 128), F32), pltpu.VMEM((C_W // 128, tm, 128), F32)],
        compiler_params=pltpu.CompilerParams(dimension_semantics=("parallel",), vmem_limit_bytes=VMEM_LIMIT),
        name="post_odd",
    )(x2, *os_, *ls_, ex, wo, g, wg, wu, wd, gf)


def _in_odd_kernel(x_ref, g_ref, w_ref, *refs):
    n_slabs = 3 * C_W // IN_SLAB
    out_refs, p_refs, f_refs = refs[:-4], refs[-4:-2], refs[-2:]
    tm = x_ref.shape[0]
    assert [d for _, d in C_GROUPS] == [1, 4, 16]
    hn = _rms(x_ref[...], g_ref[...]).astype(BF16)

    def project(slab):
        cols = slice(slab * IN_SLAB, (slab + 1) * IN_SLAB)
        return jnp.dot(hn, w_ref[:, cols], preferred_element_type=F32)

    ahead = [project(s) for s in range(IN_AHEAD)]
    for slab in range(n_slabs):
        p = ahead.pop(0)
        if slab + IN_AHEAD < n_slabs:
            ahead.append(project(slab + IN_AHEAD))
        part, col0 = divmod(slab * IN_SLAB, C_W)
        if part == 0:
            p = p * (SCALE * LOG2E)
        p_ref, f_ref = p_refs[slab % 2], f_refs[slab % 2]
        dst1, dst4, dst16 = out_refs[part * 3:part * 3 + 3]
        dst1[0, 0, :, col0:col0 + IN_SLAB] = p.astype(BF16)
        for cb in range(IN_SLAB // 128):
            cols = slice(col0 + cb * 128, col0 + (cb + 1) * 128)
            p_ref[cb] = p[:, cb * 128:(cb + 1) * 128]
            for r4 in range(4):
                by4 = p_ref[cb, pl.ds(r4, tm // 4, stride=4), :]
                dst4[0, r4, :, cols] = by4.astype(BF16)
                f_ref[cb, r4] = by4
                for j in range(4):
                    by16 = f_ref[cb, r4, pl.ds(j, tm // 16, stride=4), :]
                    dst16[0, r4 + 4 * j, :, cols] = by16.astype(BF16)


def _in_odd(x2, bn, seq, g, w):
    n = x2.shape[0]
    tm = ROW_TILE
    nt = seq // tm
    specs, shapes = [], []
    for _ in range(3):
        for _, dil in C_GROUPS:
            specs.append(pl.BlockSpec((1, dil, tm // dil, C_W), lambda i: (i // nt, 0, i % nt, 0)))
            shapes.append(jax.ShapeDtypeStruct((bn, dil, seq // dil, C_W), BF16))
    outs = pl.pallas_call(
        _in_odd_kernel,
        grid=(n // tm,),
        in_specs=[pl.BlockSpec((tm, D_MODEL), lambda i: (i, 0)), _resident((1, D_MODEL)), _resident(w.shape)],
        out_specs=tuple(specs),
        out_shape=tuple(shapes),
        scratch_shapes=[pltpu.VMEM((IN_SLAB // 128, tm, 128), F32)] * 2
        + [pltpu.VMEM((IN_SLAB // 128, 4, tm // 4, 128), F32)] * 2,
        compiler_params=pltpu.CompilerParams(dimension_semantics=("parallel",), vmem_limit_bytes=VMEM_LIMIT),
        name="in_odd",
    )(x2, g, w)
    outs = [a.reshape(a.shape[0] * a.shape[1], a.shape[2], C_W) for a in outs]
    n_g = len(C_GROUPS)
    return outs[:n_g], outs[n_g:2 * n_g], outs[2 * n_g:]


def _band_kernel(sl_ref, q_ref, kp_ref, kc_ref, kn_ref, vp_ref, vc_ref, vn_ref, o_ref, lse_ref,
                 kwin_ref, vt_ref, pen_ref, *, length, tile):
    t = pl.program_id(1)
    n_pairs = C_W // 128
    win = C_TQ + 2 * C_HALF
    span = tile + 2 * C_HALF
    kwin_ref[0:C_HALF, :] = kp_ref[C_TQ - C_HALF:C_TQ, :]
    kwin_ref[C_HALF:C_HALF + tile, :] = kc_ref[...]
    kwin_ref[C_HALF + tile:span, :] = kn_ref[0:C_HALF, :]

    @pl.when((pl.program_id(0) == 0) & (t == 0))
    def _():
        key = lax.broadcasted_iota(jnp.int32, (win, 2 * C_TQ), 0)
        lane = lax.broadcasted_iota(jnp.int32, (win, 2 * C_TQ), 1)
        dist = jnp.abs(key - C_HALF - (lane & (C_TQ - 1)))
        distf = dist.astype(F32)
        for c in range(n_pairs):
            slope = jnp.where(lane < C_TQ, sl_ref[2 * c], sl_ref[2 * c + 1])
            pen = jnp.where(dist <= C_HALF, slope * distf, -NEG)
            pen_ref[0, c] = pen
            pen_ref[1, c] = jnp.where(key >= C_HALF, pen, -NEG)
            pen_ref[2, c] = jnp.where(key < win - C_HALF, pen, -NEG)

    ones = jnp.ones((V_ROWS - HEAD_DIM, span), BF16)
    for c in range(n_pairs):
        cs = slice(c * 128, (c + 1) * 128)
        vwin = jnp.concatenate([vp_ref[C_TQ - C_HALF:C_TQ, cs], vc_ref[:, cs], vn_ref[0:C_HALF, cs]], axis=0)
        vt = vwin.T
        for hh in range(2):
            vt_ref[c, hh * V_ROWS:hh * V_ROWS + HEAD_DIM, :] = vt[hh * HEAD_DIM:(hh + 1) * HEAD_DIM, :]
            vt_ref[c, hh * V_ROWS + HEAD_DIM:(hh + 1) * V_ROWS, :] = ones

    def scores(j, c):
        cs = slice(c * 128, (c + 1) * 128)
        qs = _split_pair(q_ref[j * C_TQ:(j + 1) * C_TQ, cs])
        k = kwin_ref[j * C_TQ:j * C_TQ + win, cs]
        return lax.dot_general(k, qs, _NT, preferred_element_type=F32)

    n_blocks = tile // C_TQ
    assert n_blocks >= 2
    table = {0: jnp.where(t == 0, 1, 0), n_blocks - 1: jnp.where(t == length // tile - 1, 2, 0)}
    items = [(j, c) for j in range(n_blocks) for c in range(n_pairs)]
    ahead = [scores(*item) for item in items[:C_AHEAD]]
    for n_item, (j, c) in enumerate(items):
        rows = slice(j * C_TQ, (j + 1) * C_TQ)
        cs = slice(c * 128, (c + 1) * 128)
        s = ahead.pop(0) - pen_ref[table.get(j, 0), c]
        if n_item + C_AHEAD < len(items):
            ahead.append(scores(*items[n_item + C_AHEAD]))
        m = jnp.max(s, axis=0, keepdims=True)
        p = jnp.exp2(s - m).astype(BF16)
        acc = jnp.dot(vt_ref[c, :, j * C_TQ:j * C_TQ + win], p, preferred_element_type=F32)
        a0, a1 = acc[:V_ROWS, :C_TQ], acc[V_ROWS:, C_TQ:]
        o_ref[rows, cs] = _finish_pair(a0, a1)
        lse_ref[2 * c:2 * c + 1, rows] = m[:, :C_TQ] + jnp.log2(a0[HEAD_DIM:HEAD_DIM + 1])
        lse_ref[2 * c + 1:2 * c + 2, rows] = m[:, C_TQ:] + jnp.log2(a1[HEAD_DIM:HEAD_DIM + 1])


def _band(slopes, q, k, v):
    nb_, length, _ = q.shape
    tile = 512 if length % 512 == 0 else 256
    per = tile // C_TQ
    last = length // C_TQ - 1
    cur = lambda n, t: (n, t, 0)
    prev = lambda n, t: (n, jnp.maximum(t * per - 1, 0), 0)
    nxt = lambda n, t: (n, jnp.minimum((t + 1) * per, last), 0)
    big = lambda m: pl.BlockSpec((None, tile, C_W), m)
    halo = lambda m: pl.BlockSpec((None, C_TQ, C_W), m)
    span = tile + 2 * C_HALF
    return pl.pallas_call(
        functools.partial(_band_kernel, length=length, tile=tile),
        grid=(nb_, length // tile),
        in_specs=[pl.BlockSpec(memory_space=pltpu.SMEM), big(cur),
                  halo(prev), big(cur), halo(nxt), halo(prev), big(cur), halo(nxt)],
        out_specs=(big(cur), pl.BlockSpec((None, C_HEADS, tile), lambda n, t: (n, 0, t))),
        out_shape=(jax.ShapeDtypeStruct((nb_, length, C_W), BF16),
                   jax.ShapeDtypeStruct((nb_, C_HEADS, length), F32)),
        scratch_shapes=[pltpu.VMEM((span, C_W), BF16), pltpu.VMEM((C_W // 128, 2 * V_ROWS, span), BF16),
                        pltpu.VMEM((3, C_W // 128, C_TQ + 2 * C_HALF, 2 * C_TQ), F32)],
        compiler_params=pltpu.CompilerParams(dimension_semantics=("arbitrary", "arbitrary"),
                                             vmem_limit_bytes=VMEM_LIMIT),
        name="band",
    )(slopes, q, k, k, k, v, v, v)


def _rope_tables(seq):
    t = jnp.arange(seq)
    n = HEAD_DIM // 4
    freqs = jnp.power(ROPE_THETA, -jnp.arange(n, dtype=F32) / n)
    ang_r = (t // GRID_W).astype(F32)[:, None] * freqs[None, :]
    ang_c = (t % GRID_W).astype(F32)[:, None] * freqs[None, :]
    zero = jnp.zeros_like(ang_r)
    cos = jnp.concatenate([jnp.cos(ang_r)] * 2 + [jnp.cos(ang_c)] * 2, axis=-1)
    sin_r, sin_c = jnp.sin(ang_r), jnp.sin(ang_c)
    sa = jnp.concatenate([-sin_r, zero, -sin_c, zero], axis=-1)
    sb = jnp.concatenate([zero, sin_r, zero, sin_c], axis=-1)
    two = lambda a: jnp.concatenate([a, a], axis=-1)
    return two(cos), two(sa), two(sb)


def _trunk(x, prm):
    bn, seq, _ = x.shape
    n = bn * seq
    x2 = x.reshape(n, D_MODEL)
    cos, sa, sb = _rope_tables(seq)
    qa, ka, vat, qb, kb, vbt = _in_even(x2, seq, prm["g_mix0"], prm["w_in_even"], prm["gq"], prm["gk"],
                                        cos, sa, sb, prm["bd"])
    oa = _mix_a(qa, ka, vat, bn, seq)
    ob = _mix_b(qb, kb, vbt, _neighbourhood_bias(prm["rpb"], seq // GRID_W), bn, seq)
    x2 = _post_even(x2, oa, ob, prm["wo_a"], prm["wo_b"], prm["g_ffn0"], prm["wg0"], prm["wu0"], prm["wd0"])

    qs, ks, vs = _in_odd(x2, bn, seq, prm["g_mix1"], prm["w_in_odd"])
    outs, lses = [], []
    for (_, dil), q, k, v in zip(C_GROUPS, qs, ks, vs):
        o, lse = _band(prm["slopes"] * dil * LOG2E, q, k, v)
        outs.append(o)
        lses.append(lse.reshape(bn, dil, C_HEADS, seq // dil).transpose(0, 3, 1, 2).reshape(n, C_HEADS))
    y = _post_odd(x2, seq, outs, lses, prm["expand"], prm["wo_odd"], prm["g_ffn1"], prm["wg1"], prm["wu1"],
                  prm["wd1"], prm["g_final"])
    return y.reshape(bn, seq, D_MODEL)


def kernel(x_prompt, x_sample, norm_mix, norm_ffn, norm_final, w_in_even, a_q_norm, a_k_norm, na_rpb,
           w_out_even, w_in_odd, w_out_odd, w_gate_up, w_down):
    head = np.arange(128) // HEAD_DIM
    order = np.array([0, 4, 1, 5, 2, 6, 3, 7])
    w_in0 = w_in_even[0]
    w_qa = w_in0[:, :A_Q].reshape(D_MODEL, A_Q // HEAD_DIM, HEAD_DIM)[:, order].reshape(D_MODEL, A_Q)
    w_in0 = jnp.concatenate([w_qa, w_in0[:, A_Q:]], axis=1)
    wo_a = w_out_even[0, :A_Q].reshape(A_Q // HEAD_DIM, HEAD_DIM, D_MODEL)[order].reshape(A_Q, D_MODEL)
    prm = {
        "g_mix0": norm_mix[0][None].astype(F32), "g_mix1": norm_mix[1][None].astype(F32),
        "g_ffn0": norm_ffn[0][None].astype(F32), "g_ffn1": norm_ffn[1][None].astype(F32),
        "g_final": norm_final[None].astype(F32),
        "w_in_even": w_in0.astype(BF16), "w_in_odd": w_in_odd[0].astype(BF16),
        "gq": jnp.tile(a_q_norm[0].astype(F32) * (SCALE * LOG2E), 2)[None], "gk": jnp.tile(a_k_norm[0].astype(F32), 2)[None],
        "bd": jnp.asarray((head[:, None] == head[None, :]) / HEAD_DIM, BF16),
        "rpb": na_rpb[0],
        "wo_a": wo_a.astype(BF16), "wo_b": w_out_even[0, A_Q:].astype(BF16),
        "wo_odd": w_out_odd[0].astype(BF16),
        "wg0": w_gate_up[0, :, :D_FF].astype(BF16), "wu0": w_gate_up[0, :, D_FF:].astype(BF16),
        "wg1": w_gate_up[1, :, :D_FF].astype(BF16), "wu1": w_gate_up[1, :, D_FF:].astype(BF16),
        "wd0": w_down[0].astype(BF16), "wd1": w_down[1].astype(BF16),
        "slopes": jnp.exp2(-8.0 * (jnp.arange(C_HEADS, dtype=F32) + 1.0) / C_HEADS),
        "expand": jnp.asarray(np.arange(C_HEADS)[:, None] == (np.arange(C_W) // HEAD_DIM)[None, :], BF16),
    }
    return (_trunk(x_prompt, prm), _trunk(x_sample, prm))
```

```python
import functools

import jax
import jax.numpy as jnp
import numpy as np
from jax import lax
from jax.experimental import pallas as pl
from jax.experimental.pallas import tpu as pltpu

F32 = jnp.float32
BF16 = jnp.bfloat16

D_MODEL = 1024
HEAD_DIM = 64
A_Q = 512
A_KV = 128
B_W = 512
C_W = 1024
C_HEADS = 16
B_HEADS = 8
GRID_W = 64
NA_ROWS = 8
NA_COLS = 16
C_GROUPS = ((128, 1), (512, 4), (2048, 16))
ROPE_THETA = 10000.0
D_FF = 2816
FF_CHUNK = 2816
EPS = 1e-6
NEG = -1e30
SCALE = HEAD_DIM ** -0.5
LOG2E = 1.4426950408889634

V_ROWS = HEAD_DIM + 16
ROW_TILE = 512
A_TQ = 256
A_TK = 256
A_UNROLL = 16
A_AHEAD = 5
B_TQ = 256
B_TK = 3 * B_TQ
B_AHEAD = 4
C_TQ = 128
C_HALF = 64
IN_SLAB = 256
IN_AHEAD = 2
C_AHEAD = 5
VMEM_LIMIT = 56 * 1024 * 1024

_NT = (((1,), (1,)), ((), ()))


def _resident(shape):
    zeros = (0,) * len(shape)
    return pl.BlockSpec(shape, lambda *_: zeros, pipeline_mode=pl.Buffered(1))


def _rms(x, gain):
    ms = jnp.mean(x * x, axis=-1, keepdims=True)
    return x * lax.rsqrt(ms + EPS) * gain


def _in_even_kernel(x_ref, g_ref, w_ref, gq_ref, gk_ref, cos_ref, sa_ref, sb_ref, bd_ref,
                    qa_ref, ka_ref, vat_ref, qb_ref, kb_ref, vbt_ref):
    hn = _rms(x_ref[...], g_ref[...]).astype(BF16)
    tm = hn.shape[0]
    cos, sa, sb, bd = cos_ref[...], sa_ref[...], sb_ref[...], bd_ref[...]

    def project(slab):
        cols = slice(slab * IN_SLAB, (slab + 1) * IN_SLAB)
        return jnp.dot(hn, w_ref[:, cols], preferred_element_type=F32)

    def headnorm_rope(c, gain):
        c2 = c * c
        hi = c2.astype(BF16)
        lo = (c2 - hi.astype(F32)).astype(BF16)
        ms = (jnp.dot(hi, bd, preferred_element_type=F32)
              + jnp.dot(lo, bd, preferred_element_type=F32))
        y = c * lax.rsqrt(ms + EPS) * gain
        return y * cos + pltpu.roll(y, 112, 1) * sa + pltpu.roll(y, 16, 1) * sb

    def put_transposed(dst_ref, chunk, c):
        ones = jnp.ones((V_ROWS - HEAD_DIM, A_TK), BF16)
        for j in range(tm // A_TK):
            t = chunk[j * A_TK:(j + 1) * A_TK, :].astype(BF16).T
            for hh in range(2):
                r0 = (2 * c + hh) * V_ROWS
                dst_ref[j, r0:r0 + HEAD_DIM, :] = t[hh * HEAD_DIM:(hh + 1) * HEAD_DIM, :]
                dst_ref[j, r0 + HEAD_DIM:r0 + V_ROWS, :] = ones

    def write_out(chunk, col):
        lanes = lambda base: slice(col - base, col - base + 128)
        o = A_Q + 2 * A_KV
        if col < A_Q:
            qa_ref[:, lanes(0)] = headnorm_rope(chunk, gq_ref[...]).astype(BF16)
        elif col < A_Q + A_KV:
            ka_ref[...] = headnorm_rope(chunk, gk_ref[...]).astype(BF16)
        elif col < o:
            put_transposed(vat_ref, chunk, 0)
        elif col < o + B_W:
            qb_ref[:, lanes(o)] = (chunk * (SCALE * LOG2E)).astype(BF16)
        elif col < o + 2 * B_W:
            kb_ref[:, lanes(o + B_W)] = chunk.astype(BF16)
        else:
            put_transposed(vbt_ref, chunk, (col - o - 2 * B_W) // 128)

    n_slabs = w_ref.shape[1] // IN_SLAB
    ahead = [project(s) for s in range(IN_AHEAD)]
    for slab in range(n_slabs):
        p = ahead.pop(0)
        if slab + IN_AHEAD < n_slabs:
            ahead.append(project(slab + IN_AHEAD))
        for i in range(IN_SLAB // 128):
            write_out(p[:, i * 128:(i + 1) * 128], slab * IN_SLAB + i * 128)


def _in_even(x2, seq, g, w, gq, gk, cos, sa, sb, bd):
    n = x2.shape[0]
    tm = ROW_TILE
    nt = seq // tm
    row = lambda i: (i, 0)
    pos = lambda i: (i % nt, 0)
    va_rows, vb_rows = (A_KV // HEAD_DIM) * V_ROWS, B_HEADS * V_ROWS
    out_shape = (
        jax.ShapeDtypeStruct((n, A_Q), BF16), jax.ShapeDtypeStruct((n, A_KV), BF16),
        jax.ShapeDtypeStruct((n // A_TK, va_rows, A_TK), BF16),
        jax.ShapeDtypeStruct((n, B_W), BF16), jax.ShapeDtypeStruct((n, B_W), BF16),
        jax.ShapeDtypeStruct((n // A_TK, vb_rows, A_TK), BF16))
    return pl.pallas_call(
        _in_even_kernel,
        grid=(n // tm,),
        in_specs=[pl.BlockSpec((tm, D_MODEL), row), _resident((1, D_MODEL)), _resident(w.shape),
                  _resident((1, 128)), _resident((1, 128)),
                  pl.BlockSpec((tm, 128), pos), pl.BlockSpec((tm, 128), pos), pl.BlockSpec((tm, 128), pos),
                  _resident((128, 128))],
        out_specs=(pl.BlockSpec((tm, A_Q), row), pl.BlockSpec((tm, A_KV), row),
                   pl.BlockSpec((tm // A_TK, va_rows, A_TK), lambda i: (i, 0, 0)),
                   pl.BlockSpec((tm, B_W), row), pl.BlockSpec((tm, B_W), row),
                   pl.BlockSpec((tm // A_TK, vb_rows, A_TK), lambda i: (i, 0, 0))),
        out_shape=out_shape,
        compiler_params=pltpu.CompilerParams(dimension_semantics=("parallel",), vmem_limit_bytes=VMEM_LIMIT),
        name="in_even",
    )(x2, g, w, gq, gk, cos, sa, sb, bd)


def _split_pair(qp):
    low = lax.broadcasted_iota(jnp.int32, qp.shape, 1) < HEAD_DIM
    zero = jnp.zeros(qp.shape, qp.dtype)
    return jnp.concatenate([jnp.where(low, qp, zero), jnp.where(low, zero, qp)], axis=0)


def _finish_pair(acc_lo, acc_hi):
    halves = [a[:HEAD_DIM] / a[HEAD_DIM:HEAD_DIM + 1] for a in (acc_lo, acc_hi)]
    return jnp.concatenate(halves, axis=0).astype(BF16).T


def _mix_a_kernel(q_ref, k_ref, vt_ref, o_ref, qs_ref, m_ref, acc_ref, *, n_chunks):
    tq = A_TQ
    n_pairs = A_Q // 128
    for c in range(n_pairs):
        qs_ref[:, 2 * c * tq:(2 * c + 2) * tq] = (
            _split_pair(q_ref[:, c * 128:(c + 1) * 128]).T)
    m_ref[...] = jnp.full(m_ref.shape, NEG, F32)
    acc_ref[...] = jnp.zeros(acc_ref.shape, F32)

    def scores(ci, h):
        k = k_ref[pl.ds(pl.multiple_of(ci * A_TK, A_TK), A_TK), :]
        return jnp.dot(k, qs_ref[:, h * tq:(h + 1) * tq], preferred_element_type=F32)

    def chunks(it, carry):
        items = [(it * A_UNROLL + u, h) for u in range(A_UNROLL) for h in range(2 * n_pairs)]
        ahead = [scores(*item) for item in items[:A_AHEAD]]
        for t, (ci, h) in enumerate(items):
            s = ahead.pop(0)
            if t + A_AHEAD < len(items):
                ahead.append(scores(*items[t + A_AHEAD]))
            cols = slice(h * tq, (h + 1) * tq)
            g = h % 2
            m_old = m_ref[:, cols]
            m_new = jnp.maximum(m_old, jnp.max(s, axis=0, keepdims=True))
            alpha = jnp.exp2(m_old - m_new)
            p = jnp.exp2(s - m_new).astype(BF16)
            m_ref[:, cols] = m_new
            vt = vt_ref[ci, g * V_ROWS:(g + 1) * V_ROWS, :]
            acc_ref[h] = alpha * acc_ref[h] + jnp.dot(vt, p, preferred_element_type=F32)
        return carry

    lax.fori_loop(0, n_chunks // A_UNROLL, chunks, 0)

    for c in range(n_pairs):
        o_ref[:, c * 128:(c + 1) * 128] = _finish_pair(acc_ref[2 * c], acc_ref[2 * c + 1])


def _mix_a(qa, ka, vat, bn, seq):
    n = qa.shape[0]
    nq = seq // A_TQ
    nc = seq // A_TK
    return pl.pallas_call(
        functools.partial(_mix_a_kernel, n_chunks=nc),
        grid=(bn, nq),
        in_specs=[pl.BlockSpec((A_TQ, A_Q), lambda b, i: (b * nq + i, 0)),
                  pl.BlockSpec((seq, A_KV), lambda b, i: (b, 0)),
                  pl.BlockSpec((nc, vat.shape[1], A_TK), lambda b, i: (b, 0, 0))],
        out_specs=pl.BlockSpec((A_TQ, A_Q), lambda b, i: (b * nq + i, 0)),
        out_shape=jax.ShapeDtypeStruct((n, A_Q), BF16),
        scratch_shapes=[pltpu.VMEM((128, 8 * A_TQ), BF16), pltpu.VMEM((1, 8 * A_TQ), F32),
                        pltpu.VMEM((8, V_ROWS, A_TQ), F32)],
        compiler_params=pltpu.CompilerParams(dimension_semantics=("parallel", "parallel"),
                                             vmem_limit_bytes=VMEM_LIMIT),
        name="mix_a",
    )(qa, ka, vat)


def _mix_b_kernel(q_ref, k0_ref, k1_ref, k2_ref, vt0_ref, vt1_ref, vt2_ref, bias_ref, o_ref,
                  qs_ref, m_ref, acc_ref):
    tq = B_TQ
    n_pairs = B_W // 128
    k_refs = (k0_ref, k1_ref, k2_ref)
    vt_refs = (vt0_ref, vt1_ref, vt2_ref)
    for c in range(n_pairs):
        qs_ref[:, 2 * c * tq:(2 * c + 2) * tq] = (
            _split_pair(q_ref[:, c * 128:(c + 1) * 128]).T)

    def scores(h, j):
        c = h // 2
        return jnp.dot(k_refs[j][:, c * 128:(c + 1) * 128], qs_ref[:, h * tq:(h + 1) * tq],
                       preferred_element_type=F32)

    items = [(h, j) for h in range(2 * n_pairs) for j in range(len(k_refs))]
    ahead = [scores(*item) for item in items[:B_AHEAD]]
    for t, (h, j) in enumerate(items):
        c, hh = divmod(h, 2)
        s = bias_ref[0, c, j * tq:(j + 1) * tq, hh * tq:(hh + 1) * tq] + ahead.pop(0)
        if t + B_AHEAD < len(items):
            ahead.append(scores(*items[t + B_AHEAD]))
        rows = slice(h * V_ROWS, (h + 1) * V_ROWS)
        m_new = jnp.max(s, axis=0, keepdims=True)
        if j > 0:
            m_old = m_ref[h]
            m_new = jnp.maximum(m_old, m_new)
        pv = jnp.dot(vt_refs[j][rows, :], jnp.exp2(s - m_new).astype(BF16), preferred_element_type=F32)
        acc_ref[h] = pv if j == 0 else jnp.exp2(m_old - m_new) * acc_ref[h] + pv
        m_ref[h] = m_new
        if j == len(k_refs) - 1 and hh == 1:
            o_ref[:, c * 128:(c + 1) * 128] = _finish_pair(acc_ref[h - 1], acc_ref[h])


def _mix_b(qb, kb, vbt, bias, bn, seq):
    n = qb.shape[0]
    nb = seq // B_TQ
    q_map = lambda b, i: (b * nb + i, 0)

    def kv_map(j):
        return lambda b, i: (b * nb + jnp.clip(i - 1, 0, nb - 3) + j, 0)

    def vt_map(j):
        return lambda b, i: (b * nb + jnp.clip(i - 1, 0, nb - 3) + j, 0, 0)

    def bias_map(b, i):
        return (jnp.where(i == 0, 0, jnp.where(i == nb - 1, 2, 1)), 0, 0, 0)

    blk = lambda m: pl.BlockSpec((B_TQ, B_W), m)
    vblk = lambda m: pl.BlockSpec((None, vbt.shape[1], B_TQ), m)
    return pl.pallas_call(
        _mix_b_kernel,
        grid=(bn, nb),
        in_specs=[blk(q_map), blk(kv_map(0)), blk(kv_map(1)), blk(kv_map(2)),
                  vblk(vt_map(0)), vblk(vt_map(1)), vblk(vt_map(2)),
                  pl.BlockSpec((1, B_W // 128, B_TK, 2 * B_TQ), bias_map)],
        out_specs=blk(q_map),
        out_shape=jax.ShapeDtypeStruct((n, B_W), BF16),
        scratch_shapes=[pltpu.VMEM((128, B_HEADS * B_TQ), BF16), pltpu.VMEM((B_HEADS, 1, B_TQ), F32),
                        pltpu.VMEM((B_HEADS, V_ROWS, B_TQ), F32)],
        compiler_params=pltpu.CompilerParams(dimension_semantics=("parallel", "parallel"),
                                             vmem_limit_bytes=VMEM_LIMIT),
        name="mix_b",
    )(qb, kb, kb, kb, vbt, vbt, vbt, bias)


def _neighbourhood_bias(rpb, rows):
    n_dc = 2 * NA_COLS - 1
    c = np.arange(GRID_W)
    cs = np.clip(c - NA_COLS // 2, 0, GRID_W - NA_COLS)
    col_ok = (c[:, None] >= cs[None, :]) & (c[:, None] < cs[None, :] + NA_COLS)
    dc = np.clip(c[:, None] - c[None, :] + NA_COLS - 1, 0, n_dc - 1)
    pick = (dc.reshape(-1)[None, :] == np.arange(n_dc)[:, None]).astype(np.float32)
    by_col = jnp.einsum("hrd,dx->hrx", rpb.astype(F32) * LOG2E, jnp.asarray(pick),
                        precision=lax.Precision.HIGHEST)
    by_col = by_col.reshape(B_HEADS, 2 * NA_ROWS - 1, GRID_W, GRID_W)
    by_col = jnp.where(col_ok[None, None], by_col, NEG)
    masked = jnp.full((B_HEADS, GRID_W, GRID_W), NEG, F32)
    tables = []
    for r0, ws in ((0, 0), (4, 0), (rows - 4, rows - 12)):
        key_rows = []
        for kr in range(B_TK // GRID_W):
            blocks = []
            for rr in range(B_TQ // GRID_W):
                r = r0 + rr
                rs = min(max(r - NA_ROWS // 2, 0), rows - NA_ROWS)
                ok = rs <= ws + kr < rs + NA_ROWS
                blocks.append(by_col[:, ws + kr - r + NA_ROWS - 1] if ok else masked)
            key_rows.append(jnp.concatenate(blocks, axis=-1))
        t = jnp.concatenate(key_rows, axis=-2)
        t = t.reshape(B_HEADS // 2, 2, B_TK, B_TQ).transpose(0, 2, 1, 3).reshape(B_HEADS // 2, B_TK, 2 * B_TQ)
        tables.append(t)
    return jnp.stack(tables, axis=0)


def _ffn(x1, gain, wg_ref, wu_ref, wd_ref):
    hn = _rms(x1, gain).astype(BF16)
    acc = None
    for c in range(D_FF // FF_CHUNK):
        cs = slice(c * FF_CHUNK, (c + 1) * FF_CHUNK)
        gate = jnp.dot(hn, wg_ref[:, cs], preferred_element_type=F32)
        up = jnp.dot(hn, wu_ref[:, cs], preferred_element_type=F32)
        act = (gate * (1.0 / (1.0 + jnp.exp(-gate))) * up).astype(BF16)
        d = jnp.dot(act, wd_ref[cs, :], preferred_element_type=F32)
        acc = d if acc is None else acc + d
    return x1 + acc


def _post_even_kernel(x_ref, oa_ref, ob_ref, woa_ref, wob_ref, g_ref, wg_ref, wu_ref, wd_ref, y_ref):
    mix = (jnp.dot(oa_ref[...], woa_ref[...], preferred_element_type=F32)
           + jnp.dot(ob_ref[...], wob_ref[...], preferred_element_type=F32))
    y_ref[...] = _ffn(x_ref[...] + mix, g_ref[...], wg_ref, wu_ref, wd_ref)


def _post_even(x2, oa, ob, woa, wob, g, wg, wu, wd):
    n = x2.shape[0]
    tm = ROW_TILE
    row = lambda i: (i, 0)
    return pl.pallas_call(
        _post_even_kernel,
        grid=(n // tm,),
        in_specs=[pl.BlockSpec((tm, D_MODEL), row), pl.BlockSpec((tm, A_Q), row), pl.BlockSpec((tm, B_W), row),
                  _resident(woa.shape), _resident(wob.shape), _resident((1, D_MODEL)),
                  _resident(wg.shape), _resident(wu.shape), _resident(wd.shape)],
        out_specs=pl.BlockSpec((tm, D_MODEL), row),
        out_shape=jax.ShapeDtypeStruct((n, D_MODEL), F32),
        compiler_params=pltpu.CompilerParams(dimension_semantics=("parallel",), vmem_limit_bytes=VMEM_LIMIT),
        name="post_even",
    )(x2, oa, ob, woa, wob, g, wg, wu, wd)


def _post_odd_kernel(x_ref, o1_ref, of2_ref, of3_ref, l1_ref, l2_ref, l3_ref, ex_ref, wo_ref, g_ref,
                     wg_ref, wu_ref, wd_ref, gf_ref, y_ref, o2_ref, o3_ref):
    tm = x_ref.shape[0]
    n_cb = C_W // 128
    for (_, dil), src, dst in zip(C_GROUPS[1:], (of2_ref, of3_ref), (o2_ref, o3_ref)):
        for rho in range(dil):
            for cb in range(n_cb):
                dst[cb, pl.ds(rho, tm // dil, stride=dil), :] = src[0, rho, :, cb * 128:(cb + 1) * 128].astype(F32)
    o2 = jnp.concatenate([o2_ref[cb] for cb in range(n_cb)], axis=1)
    o3 = jnp.concatenate([o3_ref[cb] for cb in range(n_cb)], axis=1)
    l1, l2, l3 = l1_ref[...], l2_ref[...], l3_ref[...]
    mx = jnp.maximum(jnp.maximum(l1, l2), l3)
    e1, e2, e3 = jnp.exp2(l1 - mx), jnp.exp2(l2 - mx), jnp.exp2(l3 - mx)
    den = e1 + e2 + e3
    ex = ex_ref[...]

    def widen(w):
        hi = w.astype(BF16)
        lo = (w - hi.astype(F32)).astype(BF16)
        return jnp.dot(hi, ex, preferred_element_type=F32) + jnp.dot(lo, ex, preferred_element_type=F32)

    o = o3 + widen(e1 / den) * (o1_ref[...].astype(F32) - o3) + widen(e2 / den) * (o2 - o3)
    mix = jnp.dot(o.astype(BF16), wo_ref[...], preferred_element_type=F32)
    y = _ffn(x_ref[...] + mix, g_ref[...], wg_ref, wu_ref, wd_ref)
    y_ref[...] = _rms(y, gf_ref[...])


def _post_odd(x2, seq, os_, ls_, ex, wo, g, wg, wu, wd, gf):
    n = x2.shape[0]
    tm = ROW_TILE
    nt = seq // tm
    row = lambda i: (i, 0)
    wide = pl.BlockSpec((tm, D_MODEL), row)
    narrow = pl.BlockSpec((tm, C_HEADS), row)
    folded = [pl.BlockSpec((1, dil, tm // dil, C_W), lambda i: (i // nt, 0, i % nt, 0)) for _, dil in C_GROUPS[1:]]
    os_ = [os_[0].reshape(n, C_W)] + [o.reshape(n // seq, dil, seq // dil, C_W)
                                      for o, (_, dil) in zip(os_[1:], C_GROUPS[1:])]
    return pl.pallas_call(
        _post_odd_kernel,
        grid=(n // tm,),
        in_specs=[wide, wide, *folded, narrow, narrow, narrow, _resident(ex.shape), _resident(wo.shape),
                  _resident((1, D_MODEL)), _resident(wg.shape), _resident(wu.shape), _resident(wd.shape),
                  _resident((1, D_MODEL))],
        out_specs=wide,
        out_shape=jax.ShapeDtypeStruct((n, D_MODEL), F32),
        scratch_shapes=[pltpu.VMEM((C_W // 128, tm, 128), F32), pltpu.VMEM((C_W // 128, tm, 128), F32)],
        compiler_params=pltpu.CompilerParams(dimension_semantics=("parallel",), vmem_limit_bytes=VMEM_LIMIT),
        name="post_odd",
    )(x2, *os_, *ls_, ex, wo, g, wg, wu, wd, gf)


def _in_odd_kernel(x_ref, g_ref, w_ref, *refs):
    n_slabs = 3 * C_W // IN_SLAB
    out_refs, p_refs, f_refs = refs[:-4], refs[-4:-2], refs[-2:]
    tm = x_ref.shape[0]
    assert [d for _, d in C_GROUPS] == [1, 4, 16]
    hn = _rms(x_ref[...], g_ref[...]).astype(BF16)

    def project(slab):
        cols = slice(slab * IN_SLAB, (slab + 1) * IN_SLAB)
        return jnp.dot(hn, w_ref[:, cols], preferred_element_type=F32)

    ahead = [project(s) for s in range(IN_AHEAD)]
    for slab in range(n_slabs):
        p = ahead.pop(0)
        if slab + IN_AHEAD < n_slabs:
            ahead.append(project(slab + IN_AHEAD))
        part, col0 = divmod(slab * IN_SLAB, C_W)
        if part == 0:
            p = p * (SCALE * LOG2E)
        p_ref, f_ref = p_refs[slab % 2], f_refs[slab % 2]
        dst1, dst4, dst16 = out_refs[part * 3:part * 3 + 3]
        dst1[0, 0, :, col0:col0 + IN_SLAB] = p.astype(BF16)
        for cb in range(IN_SLAB // 128):
            cols = slice(col0 + cb * 128, col0 + (cb + 1) * 128)
            p_ref[cb] = p[:, cb * 128:(cb + 1) * 128]
            for r4 in range(4):
                by4 = p_ref[cb, pl.ds(r4, tm // 4, stride=4), :]
                dst4[0, r4, :, cols] = by4.astype(BF16)
                f_ref[cb, r4] = by4
                for j in range(4):
                    by16 = f_ref[cb, r4, pl.ds(j, tm // 16, stride=4), :]
                    dst16[0, r4 + 4 * j, :, cols] = by16.astype(BF16)


def _in_odd(x2, bn, seq, g, w):
    n = x2.shape[0]
    tm = ROW_TILE
    nt = seq // tm
    specs, shapes = [], []
    for _ in range(3):
        for _, dil in C_GROUPS:
            specs.append(pl.BlockSpec((1, dil, tm // dil, C_W), lambda i: (i // nt, 0, i % nt, 0)))
            shapes.append(jax.ShapeDtypeStruct((bn, dil, seq // dil, C_W), BF16))
    outs = pl.pallas_call(
        _in_odd_kernel,
        grid=(n // tm,),
        in_specs=[pl.BlockSpec((tm, D_MODEL), lambda i: (i, 0)), _resident((1, D_MODEL)), _resident(w.shape)],
        out_specs=tuple(specs),
        out_shape=tuple(shapes),
        scratch_shapes=[pltpu.VMEM((IN_SLAB // 128, tm, 128), F32)] * 2
        + [pltpu.VMEM((IN_SLAB // 128, 4, tm // 4, 128), F32)] * 2,
        compiler_params=pltpu.CompilerParams(dimension_semantics=("parallel",), vmem_limit_bytes=VMEM_LIMIT),
        name="in_odd",
    )(x2, g, w)
    outs = [a.reshape(a.shape[0] * a.shape[1], a.shape[2], C_W) for a in outs]
    n_g = len(C_GROUPS)
    return outs[:n_g], outs[n_g:2 * n_g], outs[2 * n_g:]


def _band_kernel(sl_ref, q_ref, kp_ref, kc_ref, kn_ref, vp_ref, vc_ref, vn_ref, o_ref, lse_ref,
                 kwin_ref, vt_ref, pen_ref, *, length, tile):
    t = pl.program_id(1)
    n_pairs = C_W // 128
    win = C_TQ + 2 * C_HALF
    span = tile + 2 * C_HALF
    kwin_ref[0:C_HALF, :] = kp_ref[C_TQ - C_HALF:C_TQ, :]
    kwin_ref[C_HALF:C_HALF + tile, :] = kc_ref[...]
    kwin_ref[C_HALF + tile:span, :] = kn_ref[0:C_HALF, :]

    @pl.when((pl.program_id(0) == 0) & (t == 0))
    def _():
        key = lax.broadcasted_iota(jnp.int32, (win, 2 * C_TQ), 0)
        lane = lax.broadcasted_iota(jnp.int32, (win, 2 * C_TQ), 1)
        dist = jnp.abs(key - C_HALF - (lane & (C_TQ - 1)))
        distf = dist.astype(F32)
        for c in range(n_pairs):
            slope = jnp.where(lane < C_TQ, sl_ref[2 * c], sl_ref[2 * c + 1])
            pen = jnp.where(dist <= C_HALF, slope * distf, -NEG)
            pen_ref[0, c] = pen
            pen_ref[1, c] = jnp.where(key >= C_HALF, pen, -NEG)
            pen_ref[2, c] = jnp.where(key < win - C_HALF, pen, -NEG)

    ones = jnp.ones((V_ROWS - HEAD_DIM, span), BF16)
    for c in range(n_pairs):
        cs = slice(c * 128, (c + 1) * 128)
        vwin = jnp.concatenate([vp_ref[C_TQ - C_HALF:C_TQ, cs], vc_ref[:, cs], vn_ref[0:C_HALF, cs]], axis=0)
        vt = vwin.T
        for hh in range(2):
            vt_ref[c, hh * V_ROWS:hh * V_ROWS + HEAD_DIM, :] = vt[hh * HEAD_DIM:(hh + 1) * HEAD_DIM, :]
            vt_ref[c, hh * V_ROWS + HEAD_DIM:(hh + 1) * V_ROWS, :] = ones

    def scores(j, c):
        cs = slice(c * 128, (c + 1) * 128)
        qs = _split_pair(q_ref[j * C_TQ:(j + 1) * C_TQ, cs])
        k = kwin_ref[j * C_TQ:j * C_TQ + win, cs]
        return lax.dot_general(k, qs, _NT, preferred_element_type=F32)

    n_blocks = tile // C_TQ
    assert n_blocks >= 2
    table = {0: jnp.where(t == 0, 1, 0), n_blocks - 1: jnp.where(t == length // tile - 1, 2, 0)}
    items = [(j, c) for j in range(n_blocks) for c in range(n_pairs)]
    ahead = [scores(*item) for item in items[:C_AHEAD]]
    for n_item, (j, c) in enumerate(items):
        rows = slice(j * C_TQ, (j + 1) * C_TQ)
        cs = slice(c * 128, (c + 1) * 128)
        s = ahead.pop(0) - pen_ref[table.get(j, 0), c]
        if n_item + C_AHEAD < len(items):
            ahead.append(scores(*items[n_item + C_AHEAD]))
        m = jnp.max(s, axis=0, keepdims=True)
        p = jnp.exp2(s - m).astype(BF16)
        acc = jnp.dot(vt_ref[c, :, j * C_TQ:j * C_TQ + win], p, preferred_element_type=F32)
        a0, a1 = acc[:V_ROWS, :C_TQ], acc[V_ROWS:, C_TQ:]
        o_ref[rows, cs] = _finish_pair(a0, a1)
        lse_ref[2 * c:2 * c + 1, rows] = m[:, :C_TQ] + jnp.log2(a0[HEAD_DIM:HEAD_DIM + 1])
        lse_ref[2 * c + 1:2 * c + 2, rows] = m[:, C_TQ:] + jnp.log2(a1[HEAD_DIM:HEAD_DIM + 1])


def _band(slopes, q, k, v):
    nb_, length, _ = q.shape
    tile = 512 if length % 512 == 0 else 256
    per = tile // C_TQ
    last = length // C_TQ - 1
    cur = lambda n, t: (n, t, 0)
    prev = lambda n, t: (n, jnp.maximum(t * per - 1, 0), 0)
    nxt = lambda n, t: (n, jnp.minimum((t + 1) * per, last), 0)
    big = lambda m: pl.BlockSpec((None, tile, C_W), m)
    halo = lambda m: pl.BlockSpec((None, C_TQ, C_W), m)
    span = tile + 2 * C_HALF
    return pl.pallas_call(
        functools.partial(_band_kernel, length=length, tile=tile),
        grid=(nb_, length // tile),
        in_specs=[pl.BlockSpec(memory_space=pltpu.SMEM), big(cur),
                  halo(prev), big(cur), halo(nxt), halo(prev), big(cur), halo(nxt)],
        out_specs=(big(cur), pl.BlockSpec((None, C_HEADS, tile), lambda n, t: (n, 0, t))),
        out_shape=(jax.ShapeDtypeStruct((nb_, length, C_W), BF16),
                   jax.ShapeDtypeStruct((nb_, C_HEADS, length), F32)),
        scratch_shapes=[pltpu.VMEM((span, C_W), BF16), pltpu.VMEM((C_W // 128, 2 * V_ROWS, span), BF16),
                        pltpu.VMEM((3, C_W // 128, C_TQ + 2 * C_HALF, 2 * C_TQ), F32)],
        compiler_params=pltpu.CompilerParams(dimension_semantics=("arbitrary", "arbitrary"),
                                             vmem_limit_bytes=VMEM_LIMIT),
        name="band",
    )(slopes, q, k, k, k, v, v, v)


def _rope_tables(seq):
    t = jnp.arange(seq)
    n = HEAD_DIM // 4
    freqs = jnp.power(ROPE_THETA, -jnp.arange(n, dtype=F32) / n)
    ang_r = (t // GRID_W).astype(F32)[:, None] * freqs[None, :]
    ang_c = (t % GRID_W).astype(F32)[:, None] * freqs[None, :]
    zero = jnp.zeros_like(ang_r)
    cos = jnp.concatenate([jnp.cos(ang_r)] * 2 + [jnp.cos(ang_c)] * 2, axis=-1)
    sin_r, sin_c = jnp.sin(ang_r), jnp.sin(ang_c)
    sa = jnp.concatenate([-sin_r, zero, -sin_c, zero], axis=-1)
    sb = jnp.concatenate([zero, sin_r, zero, sin_c], axis=-1)
    two = lambda a: jnp.concatenate([a, a], axis=-1)
    return two(cos), two(sa), two(sb)


def _trunk(x, prm):
    bn, seq, _ = x.shape
    n = bn * seq
    x2 = x.reshape(n, D_MODEL)
    cos, sa, sb = _rope_tables(seq)
    qa, ka, vat, qb, kb, vbt = _in_even(x2, seq, prm["g_mix0"], prm["w_in_even"], prm["gq"], prm["gk"],
                                        cos, sa, sb, prm["bd"])
    oa = _mix_a(qa, ka, vat, bn, seq)
    ob = _mix_b(qb, kb, vbt, _neighbourhood_bias(prm["rpb"], seq // GRID_W), bn, seq)
    x2 = _post_even(x2, oa, ob, prm["wo_a"], prm["wo_b"], prm["g_ffn0"], prm["wg0"], prm["wu0"], prm["wd0"])

    qs, ks, vs = _in_odd(x2, bn, seq, prm["g_mix1"], prm["w_in_odd"])
    outs, lses = [], []
    for (_, dil), q, k, v in zip(C_GROUPS, qs, ks, vs):
        o, lse = _band(prm["slopes"] * dil * LOG2E, q, k, v)
        outs.append(o)
        lses.append(lse.reshape(bn, dil, C_HEADS, seq // dil).transpose(0, 3, 1, 2).reshape(n, C_HEADS))
    y = _post_odd(x2, seq, outs, lses, prm["expand"], prm["wo_odd"], prm["g_ffn1"], prm["wg1"], prm["wu1"],
                  prm["wd1"], prm["g_final"])
    return y.reshape(bn, seq, D_MODEL)


def kernel(x_prompt, x_sample, norm_mix, norm_ffn, norm_final, w_in_even, a_q_norm, a_k_norm, na_rpb,
           w_out_even, w_in_odd, w_out_odd, w_gate_up, w_down):
    head = np.arange(128) // HEAD_DIM
    order = np.array([0, 4, 1, 5, 2, 6, 3, 7])
    w_in0 = w_in_even[0]
    w_qa = w_in0[:, :A_Q].reshape(D_MODEL, A_Q // HEAD_DIM, HEAD_DIM)[:, order].reshape(D_MODEL, A_Q)
    w_in0 = jnp.concatenate([w_qa, w_in0[:, A_Q:]], axis=1)
    wo_a = w_out_even[0, :A_Q].reshape(A_Q // HEAD_DIM, HEAD_DIM, D_MODEL)[order].reshape(A_Q, D_MODEL)
    prm = {
        "g_mix0": norm_mix[0][None].astype(F32), "g_mix1": norm_mix[1][None].astype(F32),
        "g_ffn0": norm_ffn[0][None].astype(F32), "g_ffn1": norm_ffn[1][None].astype(F32),
        "g_final": norm_final[None].astype(F32),
        "w_in_even": w_in0.astype(BF16), "w_in_odd": w_in_odd[0].astype(BF16),
        "gq": jnp.tile(a_q_norm[0].astype(F32) * (SCALE * LOG2E), 2)[None], "gk": jnp.tile(a_k_norm[0].astype(F32), 2)[None],
        "bd": jnp.asarray((head[:, None] == head[None, :]) / HEAD_DIM, BF16),
        "rpb": na_rpb[0],
        "wo_a": wo_a.astype(BF16), "wo_b": w_out_even[0, A_Q:].astype(BF16),
        "wo_odd": w_out_odd[0].astype(BF16),
        "wg0": w_gate_up[0, :, :D_FF].astype(BF16), "wu0": w_gate_up[0, :, D_FF:].astype(BF16),
        "wg1": w_gate_up[1, :, :D_FF].astype(BF16), "wu1": w_gate_up[1, :, D_FF:].astype(BF16),
        "wd0": w_down[0].astype(BF16), "wd1": w_down[1].astype(BF16),
        "slopes": jnp.exp2(-8.0 * (jnp.arange(C_HEADS, dtype=F32) + 1.0) / C_HEADS),
        "expand": jnp.asarray(np.arange(C_HEADS)[:, None] == (np.arange(C_W) // HEAD_DIM)[None, :], BF16),
    }
    return (_trunk(x_prompt, prm), _trunk(x_sample, prm))
```

```python
import functools

import jax
import jax.numpy as jnp
import numpy as np
from jax import lax
from jax.experimental import pallas as pl
from jax.experimental.pallas import tpu as pltpu

F32 = jnp.float32
BF16 = jnp.bfloat16

D_MODEL = 1024
HEAD_DIM = 64
A_Q = 512
A_KV = 128
B_W = 512
C_W = 1024
C_HEADS = 16
B_HEADS = 8
GRID_W = 64
NA_ROWS = 8
NA_COLS = 16
C_GROUPS = ((128, 1), (512, 4), (2048, 16))
ROPE_THETA = 10000.0
D_FF = 2816
EPS = 1e-6
NEG = -1e30
SCALE = HEAD_DIM ** -0.5
LOG2E = 1.4426950408889634

V_ROWS = HEAD_DIM + 16
ROW_TILE = 512
A_TQ = 256
A_TK = 256
A_UNROLL = 16
A_AHEAD = 5
B_TQ = 256
B_TK = 3 * B_TQ
B_AHEAD = 4
C_TQ = 128
C_HALF = 64
IN_SLAB = 256
IN_AHEAD = 2
C_AHEAD = 5
VMEM_LIMIT = 56 * 1024 * 1024

_NT = (((1,), (1,)), ((), ()))


def _resident(shape):
    zeros = (0,) * len(shape)
    return pl.BlockSpec(shape, lambda *_: zeros, pipeline_mode=pl.Buffered(1))


def _rms(x, gain):
    ms = jnp.mean(x * x, axis=-1, keepdims=True)
    return x * lax.rsqrt(ms + EPS) * gain


def _in_even_kernel(x_ref, g_ref, w_ref, gq_ref, gk_ref, cos_ref, sa_ref, sb_ref, bd_ref,
                    qa_ref, ka_ref, vat_ref, qb_ref, kb_ref, vbt_ref):
    hn = _rms(x_ref[...], g_ref[...]).astype(BF16)
    tm = hn.shape[0]
    cos, sa, sb, bd = cos_ref[...], sa_ref[...], sb_ref[...], bd_ref[...]

    def project(slab):
        cols = slice(slab * IN_SLAB, (slab + 1) * IN_SLAB)
        return jnp.dot(hn, w_ref[:, cols], preferred_element_type=F32)

    def headnorm_rope(c, gain):
        c2 = c * c
        hi = c2.astype(BF16)
        lo = (c2 - hi.astype(F32)).astype(BF16)
        ms = (jnp.dot(hi, bd, preferred_element_type=F32)
              + jnp.dot(lo, bd, preferred_element_type=F32))
        y = c * lax.rsqrt(ms + EPS) * gain
        return y * cos + pltpu.roll(y, 112, 1) * sa + pltpu.roll(y, 16, 1) * sb

    def put_transposed(dst_ref, chunk, c):
        ones = jnp.ones((V_ROWS - HEAD_DIM, A_TK), BF16)
        for j in range(tm // A_TK):
            t = chunk[j * A_TK:(j + 1) * A_TK, :].astype(BF16).T
            for hh in range(2):
                r0 = (2 * c + hh) * V_ROWS
                dst_ref[j, r0:r0 + HEAD_DIM, :] = t[hh * HEAD_DIM:(hh + 1) * HEAD_DIM, :]
                dst_ref[j, r0 + HEAD_DIM:r0 + V_ROWS, :] = ones

    def write_out(chunk, col):
        lanes = lambda base: slice(col - base, col - base + 128)
        o = A_Q + 2 * A_KV
        if col < A_Q:
            qa_ref[:, lanes(0)] = headnorm_rope(chunk, gq_ref[...]).astype(BF16)
        elif col < A_Q + A_KV:
            ka_ref[...] = headnorm_rope(chunk, gk_ref[...]).astype(BF16)
        elif col < o:
            put_transposed(vat_ref, chunk, 0)
        elif col < o + B_W:
            qb_ref[:, lanes(o)] = (chunk * (SCALE * LOG2E)).astype(BF16)
        elif col < o + 2 * B_W:
            kb_ref[:, lanes(o + B_W)] = chunk.astype(BF16)
        else:
            put_transposed(vbt_ref, chunk, (col - o - 2 * B_W) // 128)

    n_slabs = w_ref.shape[1] // IN_SLAB
    ahead = [project(s) for s in range(IN_AHEAD)]
    for slab in range(n_slabs):
        p = ahead.pop(0)
        if slab + IN_AHEAD < n_slabs:
            ahead.append(project(slab + IN_AHEAD))
        for i in range(IN_SLAB // 128):
            write_out(p[:, i * 128:(i + 1) * 128], slab * IN_SLAB + i * 128)


def _in_even(x2, seq, g, w, gq, gk, cos, sa, sb, bd):
    n = x2.shape[0]
    tm = ROW_TILE
    nt = seq // tm
    row = lambda i: (i, 0)
    pos = lambda i: (i % nt, 0)
    va_rows, vb_rows = (A_KV // HEAD_DIM) * V_ROWS, B_HEADS * V_ROWS
    out_shape = (
        jax.ShapeDtypeStruct((n, A_Q), BF16), jax.ShapeDtypeStruct((n, A_KV), BF16),
        jax.ShapeDtypeStruct((n // A_TK, va_rows, A_TK), BF16),
        jax.ShapeDtypeStruct((n, B_W), BF16), jax.ShapeDtypeStruct((n, B_W), BF16),
        jax.ShapeDtypeStruct((n // A_TK, vb_rows, A_TK), BF16))
    return pl.pallas_call(
        _in_even_kernel,
        grid=(n // tm,),
        in_specs=[pl.BlockSpec((tm, D_MODEL), row), _resident((1, D_MODEL)), _resident(w.shape),
                  _resident((1, 128)), _resident((1, 128)),
                  pl.BlockSpec((tm, 128), pos), pl.BlockSpec((tm, 128), pos), pl.BlockSpec((tm, 128), pos),
                  _resident((128, 128))],
        out_specs=(pl.BlockSpec((tm, A_Q), row), pl.BlockSpec((tm, A_KV), row),
                   pl.BlockSpec((tm // A_TK, va_rows, A_TK), lambda i: (i, 0, 0)),
                   pl.BlockSpec((tm, B_W), row), pl.BlockSpec((tm, B_W), row),
                   pl.BlockSpec((tm // A_TK, vb_rows, A_TK), lambda i: (i, 0, 0))),
        out_shape=out_shape,
        compiler_params=pltpu.CompilerParams(dimension_semantics=("parallel",), vmem_limit_bytes=VMEM_LIMIT),
        name="in_even",
    )(x2, g, w, gq, gk, cos, sa, sb, bd)


def _split_pair(qp):
    low = lax.broadcasted_iota(jnp.int32, qp.shape, 1) < HEAD_DIM
    zero = jnp.zeros(qp.shape, qp.dtype)
    return jnp.concatenate([jnp.where(low, qp, zero), jnp.where(low, zero, qp)], axis=0)


def _finish_pair(acc_lo, acc_hi):
    halves = [a[:HEAD_DIM] / a[HEAD_DIM:HEAD_DIM + 1] for a in (acc_lo, acc_hi)]
    return jnp.concatenate(halves, axis=0).astype(BF16).T


def _mix_a_kernel(q_ref, k_ref, vt_ref, o_ref, qs_ref, m_ref, acc_ref, *, n_chunks):
    tq = A_TQ
    n_pairs = A_Q // 128
    for c in range(n_pairs):
        qs_ref[:, 2 * c * tq:(2 * c + 2) * tq] = (
            _split_pair(q_ref[:, c * 128:(c + 1) * 128]).T)
    m_ref[...] = jnp.full(m_ref.shape, NEG, F32)
    acc_ref[...] = jnp.zeros(acc_ref.shape, F32)

    def scores(ci, h):
        k = k_ref[pl.ds(pl.multiple_of(ci * A_TK, A_TK), A_TK), :]
        return jnp.dot(k, qs_ref[:, h * tq:(h + 1) * tq], preferred_element_type=F32)

    def chunks(it, carry):
        items = [(it * A_UNROLL + u, h) for u in range(A_UNROLL) for h in range(2 * n_pairs)]
        ahead = [scores(*item) for item in items[:A_AHEAD]]
        for t, (ci, h) in enumerate(items):
            s = ahead.pop(0)
            if t + A_AHEAD < len(items):
                ahead.append(scores(*items[t + A_AHEAD]))
            cols = slice(h * tq, (h + 1) * tq)
            g = h % 2
            m_old = m_ref[:, cols]
            m_new = jnp.maximum(m_old, jnp.max(s, axis=0, keepdims=True))
            alpha = jnp.exp2(m_old - m_new)
            p = jnp.exp2(s - m_new).astype(BF16)
            m_ref[:, cols] = m_new
            vt = vt_ref[ci, g * V_ROWS:(g + 1) * V_ROWS, :]
            acc_ref[h] = alpha * acc_ref[h] + jnp.dot(vt, p, preferred_element_type=F32)
        return carry

    lax.fori_loop(0, n_chunks // A_UNROLL, chunks, 0)

    for c in range(n_pairs):
        o_ref[:, c * 128:(c + 1) * 128] = _finish_pair(acc_ref[2 * c], acc_ref[2 * c + 1])


def _mix_a(qa, ka, vat, bn, seq):
    n = qa.shape[0]
    nq = seq // A_TQ
    nc = seq // A_TK
    assert seq % A_TQ == 0 and nc % A_UNROLL == 0, seq
    return pl.pallas_call(
        functools.partial(_mix_a_kernel, n_chunks=nc),
        grid=(bn, nq),
        in_specs=[pl.BlockSpec((A_TQ, A_Q), lambda b, i: (b * nq + i, 0)),
                  pl.BlockSpec((seq, A_KV), lambda b, i: (b, 0)),
                  pl.BlockSpec((nc, vat.shape[1], A_TK), lambda b, i: (b, 0, 0))],
        out_specs=pl.BlockSpec((A_TQ, A_Q), lambda b, i: (b * nq + i, 0)),
        out_shape=jax.ShapeDtypeStruct((n, A_Q), BF16),
        scratch_shapes=[pltpu.VMEM((128, 8 * A_TQ), BF16), pltpu.VMEM((1, 8 * A_TQ), F32),
                        pltpu.VMEM((8, V_ROWS, A_TQ), F32)],
        compiler_params=pltpu.CompilerParams(dimension_semantics=("parallel", "parallel"),
                                             vmem_limit_bytes=VMEM_LIMIT),
        name="mix_a",
    )(qa, ka, vat)


def _mix_b_kernel(q_ref, k0_ref, k1_ref, k2_ref, vt0_ref, vt1_ref, vt2_ref, bias_ref, o_ref,
                  qs_ref, m_ref, acc_ref):
    tq = B_TQ
    n_pairs = B_W // 128
    k_refs = (k0_ref, k1_ref, k2_ref)
    vt_refs = (vt0_ref, vt1_ref, vt2_ref)
    for c in range(n_pairs):
        qs_ref[:, 2 * c * tq:(2 * c + 2) * tq] = (
            _split_pair(q_ref[:, c * 128:(c + 1) * 128]).T)

    def scores(h, j):
        c = h // 2
        return jnp.dot(k_refs[j][:, c * 128:(c + 1) * 128], qs_ref[:, h * tq:(h + 1) * tq],
                       preferred_element_type=F32)

    items = [(h, j) for h in range(2 * n_pairs) for j in range(len(k_refs))]
    ahead = [scores(*item) for item in items[:B_AHEAD]]
    for t, (h, j) in enumerate(items):
        c, hh = divmod(h, 2)
        s = bias_ref[0, c, j * tq:(j + 1) * tq, hh * tq:(hh + 1) * tq] + ahead.pop(0)
        if t + B_AHEAD < len(items):
            ahead.append(scores(*items[t + B_AHEAD]))
        rows = slice(h * V_ROWS, (h + 1) * V_ROWS)
        m_new = jnp.max(s, axis=0, keepdims=True)
        if j > 0:
            m_old = m_ref[h]
            m_new = jnp.maximum(m_old, m_new)
        pv = jnp.dot(vt_refs[j][rows, :], jnp.exp2(s - m_new).astype(BF16), preferred_element_type=F32)
        acc_ref[h] = pv if j == 0 else jnp.exp2(m_old - m_new) * acc_ref[h] + pv
        m_ref[h] = m_new
        if j == len(k_refs) - 1 and hh == 1:
            o_ref[:, c * 128:(c + 1) * 128] = _finish_pair(acc_ref[h - 1], acc_ref[h])


def _mix_b(qb, kb, vbt, bias, bn, seq):
    n = qb.shape[0]
    nb = seq // B_TQ
    q_map = lambda b, i: (b * nb + i, 0)

    def kv_map(j):
        return lambda b, i: (b * nb + jnp.clip(i - 1, 0, nb - 3) + j, 0)

    def vt_map(j):
        return lambda b, i: (b * nb + jnp.clip(i - 1, 0, nb - 3) + j, 0, 0)

    def bias_map(b, i):
        return (jnp.where(i == 0, 0, jnp.where(i == nb - 1, 2, 1)), 0, 0, 0)

    blk = lambda m: pl.BlockSpec((B_TQ, B_W), m)
    vblk = lambda m: pl.BlockSpec((None, vbt.shape[1], B_TQ), m)
    return pl.pallas_call(
        _mix_b_kernel,
        grid=(bn, nb),
        in_specs=[blk(q_map), blk(kv_map(0)), blk(kv_map(1)), blk(kv_map(2)),
                  vblk(vt_map(0)), vblk(vt_map(1)), vblk(vt_map(2)),
                  pl.BlockSpec((1, B_W // 128, B_TK, 2 * B_TQ), bias_map)],
        out_specs=blk(q_map),
        out_shape=jax.ShapeDtypeStruct((n, B_W), BF16),
        scratch_shapes=[pltpu.VMEM((128, B_HEADS * B_TQ), BF16), pltpu.VMEM((B_HEADS, 1, B_TQ), F32),
                        pltpu.VMEM((B_HEADS, V_ROWS, B_TQ), F32)],
        compiler_params=pltpu.CompilerParams(dimension_semantics=("parallel", "parallel"),
                                             vmem_limit_bytes=VMEM_LIMIT),
        name="mix_b",
    )(qb, kb, kb, kb, vbt, vbt, vbt, bias)


def _neighbourhood_bias(rpb, rows):
    n_dc = 2 * NA_COLS - 1
    c = np.arange(GRID_W)
    cs = np.clip(c - NA_COLS // 2, 0, GRID_W - NA_COLS)
    col_ok = (c[:, None] >= cs[None, :]) & (c[:, None] < cs[None, :] + NA_COLS)
    dc = np.clip(c[:, None] - c[None, :] + NA_COLS - 1, 0, n_dc - 1)
    pick = (dc.reshape(-1)[None, :] == np.arange(n_dc)[:, None]).astype(np.float32)
    by_col = jnp.einsum("hrd,dx->hrx", rpb.astype(F32) * LOG2E, jnp.asarray(pick),
                        precision=lax.Precision.HIGHEST)
    by_col = by_col.reshape(B_HEADS, 2 * NA_ROWS - 1, GRID_W, GRID_W)
    by_col = jnp.where(col_ok[None, None], by_col, NEG)
    masked = jnp.full((B_HEADS, GRID_W, GRID_W), NEG, F32)
    tables = []
    for r0, ws in ((0, 0), (4, 0), (rows - 4, rows - 12)):
        key_rows = []
        for kr in range(B_TK // GRID_W):
            blocks = []
            for rr in range(B_TQ // GRID_W):
                r = r0 + rr
                rs = min(max(r - NA_ROWS // 2, 0), rows - NA_ROWS)
                ok = rs <= ws + kr < rs + NA_ROWS
                blocks.append(by_col[:, ws + kr - r + NA_ROWS - 1] if ok else masked)
            key_rows.append(jnp.concatenate(blocks, axis=-1))
        t = jnp.concatenate(key_rows, axis=-2)
        t = t.reshape(B_HEADS // 2, 2, B_TK, B_TQ).transpose(0, 2, 1, 3).reshape(B_HEADS // 2, B_TK, 2 * B_TQ)
        tables.append(t)
    return jnp.stack(tables, axis=0)


def _ffn(x1, gain, wg_ref, wu_ref, wd_ref):
    hn = _rms(x1, gain).astype(BF16)
    gate = jnp.dot(hn, wg_ref[...], preferred_element_type=F32)
    up = jnp.dot(hn, wu_ref[...], preferred_element_type=F32)
    act = (gate * (1.0 / (1.0 + jnp.exp(-gate))) * up).astype(BF16)
    return x1 + jnp.dot(act, wd_ref[...], preferred_element_type=F32)


def _post_even_kernel(x_ref, oa_ref, ob_ref, woa_ref, wob_ref, g_ref, wg_ref, wu_ref, wd_ref, y_ref):
    mix = (jnp.dot(oa_ref[...], woa_ref[...], preferred_element_type=F32)
           + jnp.dot(ob_ref[...], wob_ref[...], preferred_element_type=F32))
    y_ref[...] = _ffn(x_ref[...] + mix, g_ref[...], wg_ref, wu_ref, wd_ref)


def _post_even(x2, oa, ob, woa, wob, g, wg, wu, wd):
    n = x2.shape[0]
    tm = ROW_TILE
    row = lambda i: (i, 0)
    return pl.pallas_call(
        _post_even_kernel,
        grid=(n // tm,),
        in_specs=[pl.BlockSpec((tm, D_MODEL), row), pl.BlockSpec((tm, A_Q), row), pl.BlockSpec((tm, B_W), row),
                  _resident(woa.shape), _resident(wob.shape), _resident((1, D_MODEL)),
                  _resident(wg.shape), _resident(wu.shape), _resident(wd.shape)],
        out_specs=pl.BlockSpec((tm, D_MODEL), row),
        out_shape=jax.ShapeDtypeStruct((n, D_MODEL), F32),
        compiler_params=pltpu.CompilerParams(dimension_semantics=("parallel",), vmem_limit_bytes=VMEM_LIMIT),
        name="post_even",
    )(x2, oa, ob, woa, wob, g, wg, wu, wd)


def _post_odd_kernel(x_ref, o1_ref, of2_ref, of3_ref, l1_ref, l2_ref, l3_ref, ex_ref, wo_ref, g_ref,
                     wg_ref, wu_ref, wd_ref, gf_ref, y_ref, o2_ref, o3_ref):
    tm = x_ref.shape[0]
    n_cb = C_W // 128
    for (_, dil), src, dst in zip(C_GROUPS[1:], (of2_ref, of3_ref), (o2_ref, o3_ref)):
        for rho in range(dil):
            for cb in range(n_cb):
                dst[cb, pl.ds(rho, tm // dil, stride=dil), :] = src[0, rho, :, cb * 128:(cb + 1) * 128].astype(F32)
    o2 = jnp.concatenate([o2_ref[cb] for cb in range(n_cb)], axis=1)
    o3 = jnp.concatenate([o3_ref[cb] for cb in range(n_cb)], axis=1)
    l1, l2, l3 = l1_ref[...], l2_ref[...], l3_ref[...]
    mx = jnp.maximum(jnp.maximum(l1, l2), l3)
    e1, e2, e3 = jnp.exp2(l1 - mx), jnp.exp2(l2 - mx), jnp.exp2(l3 - mx)
    den = e1 + e2 + e3
    ex = ex_ref[...]

    def widen(w):
        hi = w.astype(BF16).astype(F32)
        return jnp.dot(jnp.concatenate([hi, w - hi], axis=1).astype(BF16), ex, preferred_element_type=F32)

    o = o3 + widen(e1 / den) * (o1_ref[...].astype(F32) - o3) + widen(e2 / den) * (o2 - o3)
    mix = jnp.dot(o.astype(BF16), wo_ref[...], preferred_element_type=F32)
    y = _ffn(x_ref[...] + mix, g_ref[...], wg_ref, wu_ref, wd_ref)
    y_ref[...] = _rms(y, gf_ref[...])


def _post_odd(x2, seq, os_, ls_, ex, wo, g, wg, wu, wd, gf):
    n = x2.shape[0]
    tm = ROW_TILE
    nt = seq // tm
    row = lambda i: (i, 0)
    wide = pl.BlockSpec((tm, D_MODEL), row)
    narrow = pl.BlockSpec((tm, C_HEADS), row)
    folded = [pl.BlockSpec((1, dil, tm // dil, C_W), lambda i: (i // nt, 0, i % nt, 0)) for _, dil in C_GROUPS[1:]]
    os_ = [os_[0].reshape(n, C_W)] + [o.reshape(n // seq, dil, seq // dil, C_W)
                                      for o, (_, dil) in zip(os_[1:], C_GROUPS[1:])]
    return pl.pallas_call(
        _post_odd_kernel,
        grid=(n // tm,),
        in_specs=[wide, wide, *folded, narrow, narrow, narrow, _resident(ex.shape), _resident(wo.shape),
                  _resident((1, D_MODEL)), _resident(wg.shape), _resident(wu.shape), _resident(wd.shape),
                  _resident((1, D_MODEL))],
        out_specs=wide,
        out_shape=jax.ShapeDtypeStruct((n, D_MODEL), F32),
        scratch_shapes=[pltpu.VMEM((C_W // 128, tm, 128), F32), pltpu.VMEM((C_W // 128, tm, 128), F32)],
        compiler_params=pltpu.CompilerParams(dimension_semantics=("parallel",), vmem_limit_bytes=VMEM_LIMIT),
        name="post_odd",
    )(x2, *os_, *ls_, ex, wo, g, wg, wu, wd, gf)


def _in_odd_kernel(x_ref, g_ref, w_ref, *refs):
    n_slabs = 3 * C_W // IN_SLAB
    out_refs, p_refs, f_refs = refs[:-4], refs[-4:-2], refs[-2:]
    tm = x_ref.shape[0]
    assert [d for _, d in C_GROUPS] == [1, 4, 16]
    hn = _rms(x_ref[...], g_ref[...]).astype(BF16)

    def project(slab):
        cols = slice(slab * IN_SLAB, (slab + 1) * IN_SLAB)
        return jnp.dot(hn, w_ref[:, cols], preferred_element_type=F32)

    ahead = [project(s) for s in range(IN_AHEAD)]
    for slab in range(n_slabs):
        p = ahead.pop(0)
        if slab + IN_AHEAD < n_slabs:
            ahead.append(project(slab + IN_AHEAD))
        part, col0 = divmod(slab * IN_SLAB, C_W)
        if part == 0:
            p = p * (SCALE * LOG2E)
        p_ref, f_ref = p_refs[slab % 2], f_refs[slab % 2]
        dst1, dst4, dst16 = out_refs[part * 3:part * 3 + 3]
        dst1[0, 0, :, col0:col0 + IN_SLAB] = p.astype(BF16)
        for cb in range(IN_SLAB // 128):
            cols = slice(col0 + cb * 128, col0 + (cb + 1) * 128)
            p_ref[cb] = p[:, cb * 128:(cb + 1) * 128]
            for r4 in range(4):
                by4 = p_ref[cb, pl.ds(r4, tm // 4, stride=4), :]
                dst4[0, r4, :, cols] = by4.astype(BF16)
                f_ref[cb, r4] = by4
                for j in range(4):
                    by16 = f_ref[cb, r4, pl.ds(j, tm // 16, stride=4), :]
                    dst16[0, r4 + 4 * j, :, cols] = by16.astype(BF16)


def _in_odd(x2, bn, seq, g, w):
    n = x2.shape[0]
    tm = ROW_TILE
    nt = seq // tm
    specs, shapes = [], []
    for _ in range(3):
        for _, dil in C_GROUPS:
            specs.append(pl.BlockSpec((1, dil, tm // dil, C_W), lambda i: (i // nt, 0, i % nt, 0)))
            shapes.append(jax.ShapeDtypeStruct((bn, dil, seq // dil, C_W), BF16))
    outs = pl.pallas_call(
        _in_odd_kernel,
        grid=(n // tm,),
        in_specs=[pl.BlockSpec((tm, D_MODEL), lambda i: (i, 0)), _resident((1, D_MODEL)), _resident(w.shape)],
        out_specs=tuple(specs),
        out_shape=tuple(shapes),
        scratch_shapes=[pltpu.VMEM((IN_SLAB // 128, tm, 128), F32)] * 2
        + [pltpu.VMEM((IN_SLAB // 128, 4, tm // 4, 128), F32)] * 2,
        compiler_params=pltpu.CompilerParams(dimension_semantics=("parallel",), vmem_limit_bytes=VMEM_LIMIT),
        name="in_odd",
    )(x2, g, w)
    outs = [a.reshape(a.shape[0] * a.shape[1], a.shape[2], C_W) for a in outs]
    n_g = len(C_GROUPS)
    return outs[:n_g], outs[n_g:2 * n_g], outs[2 * n_g:]


def _band_kernel(sl_ref, q_ref, kp_ref, kc_ref, kn_ref, vp_ref, vc_ref, vn_ref, o_ref, lse_ref,
                 kwin_ref, vt_ref, pen_ref, *, length, tile):
    t = pl.program_id(1)
    n_pairs = C_W // 128
    win = C_TQ + 2 * C_HALF
    span = tile + 2 * C_HALF
    kwin_ref[0:C_HALF, :] = kp_ref[C_TQ - C_HALF:C_TQ, :]
    kwin_ref[C_HALF:C_HALF + tile, :] = kc_ref[...]
    kwin_ref[C_HALF + tile:span, :] = kn_ref[0:C_HALF, :]

    @pl.when((pl.program_id(0) == 0) & (t == 0))
    def _():
        key = lax.broadcasted_iota(jnp.int32, (win, 2 * C_TQ), 0)
        lane = lax.broadcasted_iota(jnp.int32, (win, 2 * C_TQ), 1)
        dist = jnp.abs(key - C_HALF - (lane & (C_TQ - 1)))
        distf = dist.astype(F32)
        for c in range(n_pairs):
            slope = jnp.where(lane < C_TQ, sl_ref[2 * c], sl_ref[2 * c + 1])
            pen = jnp.where(dist <= C_HALF, slope * distf, -NEG)
            pen_ref[0, c] = pen
            pen_ref[1, c] = jnp.where(key >= C_HALF, pen, -NEG)
            pen_ref[2, c] = jnp.where(key < win - C_HALF, pen, -NEG)

    ones = jnp.ones((V_ROWS - HEAD_DIM, span), BF16)
    for c in range(n_pairs):
        cs = slice(c * 128, (c + 1) * 128)
        vwin = jnp.concatenate([vp_ref[C_TQ - C_HALF:C_TQ, cs], vc_ref[:, cs], vn_ref[0:C_HALF, cs]], axis=0)
        vt = vwin.T
        for hh in range(2):
            vt_ref[c, hh * V_ROWS:hh * V_ROWS + HEAD_DIM, :] = vt[hh * HEAD_DIM:(hh + 1) * HEAD_DIM, :]
            vt_ref[c, hh * V_ROWS + HEAD_DIM:(hh + 1) * V_ROWS, :] = ones

    def scores(j, c):
        cs = slice(c * 128, (c + 1) * 128)
        qs = _split_pair(q_ref[j * C_TQ:(j + 1) * C_TQ, cs])
        k = kwin_ref[j * C_TQ:j * C_TQ + win, cs]
        return lax.dot_general(k, qs, _NT, preferred_element_type=F32)

    n_blocks = tile // C_TQ
    assert n_blocks >= 2
    table = {0: jnp.where(t == 0, 1, 0), n_blocks - 1: jnp.where(t == length // tile - 1, 2, 0)}
    items = [(j, c) for j in range(n_blocks) for c in range(n_pairs)]
    ahead = [scores(*item) for item in items[:C_AHEAD]]
    for n_item, (j, c) in enumerate(items):
        rows = slice(j * C_TQ, (j + 1) * C_TQ)
        cs = slice(c * 128, (c + 1) * 128)
        s = ahead.pop(0) - pen_ref[table.get(j, 0), c]
        if n_item + C_AHEAD < len(items):
            ahead.append(scores(*items[n_item + C_AHEAD]))
        m = jnp.max(s, axis=0, keepdims=True)
        p = jnp.exp2(s - m).astype(BF16)
        acc = jnp.dot(vt_ref[c, :, j * C_TQ:j * C_TQ + win], p, preferred_element_type=F32)
        a0, a1 = acc[:V_ROWS, :C_TQ], acc[V_ROWS:, C_TQ:]
        o_ref[rows, cs] = _finish_pair(a0, a1)
        lse_ref[2 * c:2 * c + 1, rows] = m[:, :C_TQ] + jnp.log2(a0[HEAD_DIM:HEAD_DIM + 1])
        lse_ref[2 * c + 1:2 * c + 2, rows] = m[:, C_TQ:] + jnp.log2(a1[HEAD_DIM:HEAD_DIM + 1])


def _band(slopes, q, k, v):
    nb_, length, _ = q.shape
    tile = 512 if length % 512 == 0 else 256
    per = tile // C_TQ
    last = length // C_TQ - 1
    cur = lambda n, t: (n, t, 0)
    prev = lambda n, t: (n, jnp.maximum(t * per - 1, 0), 0)
    nxt = lambda n, t: (n, jnp.minimum((t + 1) * per, last), 0)
    big = lambda m: pl.BlockSpec((None, tile, C_W), m)
    halo = lambda m: pl.BlockSpec((None, C_TQ, C_W), m)
    span = tile + 2 * C_HALF
    return pl.pallas_call(
        functools.partial(_band_kernel, length=length, tile=tile),
        grid=(nb_, length // tile),
        in_specs=[pl.BlockSpec(memory_space=pltpu.SMEM), big(cur),
                  halo(prev), big(cur), halo(nxt), halo(prev), big(cur), halo(nxt)],
        out_specs=(big(cur), pl.BlockSpec((None, C_HEADS, tile), lambda n, t: (n, 0, t))),
        out_shape=(jax.ShapeDtypeStruct((nb_, length, C_W), BF16),
                   jax.ShapeDtypeStruct((nb_, C_HEADS, length), F32)),
        scratch_shapes=[pltpu.VMEM((span, C_W), BF16), pltpu.VMEM((C_W // 128, 2 * V_ROWS, span), BF16),
                        pltpu.VMEM((3, C_W // 128, C_TQ + 2 * C_HALF, 2 * C_TQ), F32)],
        compiler_params=pltpu.CompilerParams(dimension_semantics=("arbitrary", "arbitrary"),
                                             vmem_limit_bytes=VMEM_LIMIT),
        name="band",
    )(slopes, q, k, k, k, v, v, v)


def _rope_tables(seq):
    t = jnp.arange(seq)
    n = HEAD_DIM // 4
    freqs = jnp.power(ROPE_THETA, -jnp.arange(n, dtype=F32) / n)
    ang_r = (t // GRID_W).astype(F32)[:, None] * freqs[None, :]
    ang_c = (t % GRID_W).astype(F32)[:, None] * freqs[None, :]
    zero = jnp.zeros_like(ang_r)
    cos = jnp.concatenate([jnp.cos(ang_r)] * 2 + [jnp.cos(ang_c)] * 2, axis=-1)
    sin_r, sin_c = jnp.sin(ang_r), jnp.sin(ang_c)
    sa = jnp.concatenate([-sin_r, zero, -sin_c, zero], axis=-1)
    sb = jnp.concatenate([zero, sin_r, zero, sin_c], axis=-1)
    two = lambda a: jnp.concatenate([a, a], axis=-1)
    return two(cos), two(sa), two(sb)


def _trunk(x, prm):
    bn, seq, _ = x.shape
    n = bn * seq
    assert x.shape[2] == D_MODEL and seq % (16 * 256) == 0, x.shape
    x2 = x.reshape(n, D_MODEL)
    cos, sa, sb = prm["rope"]
    qa, ka, vat, qb, kb, vbt = _in_even(x2, seq, prm["g_mix0"], prm["w_in_even"], prm["gq"], prm["gk"],
                                        cos, sa, sb, prm["bd"])
    oa = _mix_a(qa, ka, vat, bn, seq)
    ob = _mix_b(qb, kb, vbt, _neighbourhood_bias(prm["rpb"], seq // GRID_W), bn, seq)
    x2 = _post_even(x2, oa, ob, prm["wo_a"], prm["wo_b"], prm["g_ffn0"], prm["wg0"], prm["wu0"], prm["wd0"])

    qs, ks, vs = _in_odd(x2, bn, seq, prm["g_mix1"], prm["w_in_odd"])
    outs, lses = [], []
    for (_, dil), q, k, v in zip(C_GROUPS, qs, ks, vs):
        o, lse = _band(prm["slopes"] * dil * LOG2E, q, k, v)
        outs.append(o)
        lses.append(lse.reshape(bn, dil, C_HEADS, seq // dil).transpose(0, 3, 1, 2).reshape(n, C_HEADS))
    y = _post_odd(x2, seq, outs, lses, prm["expand"], prm["wo_odd"], prm["g_ffn1"], prm["wg1"], prm["wu1"],
                  prm["wd1"], prm["g_final"])
    return y.reshape(bn, seq, D_MODEL)


def kernel(x_prompt, x_sample, norm_mix, norm_ffn, norm_final, w_in_even, a_q_norm, a_k_norm, na_rpb,
           w_out_even, w_in_odd, w_out_odd, w_gate_up, w_down):
    head = np.arange(128) // HEAD_DIM
    order = np.array([0, 4, 1, 5, 2, 6, 3, 7])
    w_in0 = w_in_even[0]
    w_qa = w_in0[:, :A_Q].reshape(D_MODEL, A_Q // HEAD_DIM, HEAD_DIM)[:, order].reshape(D_MODEL, A_Q)
    w_in0 = jnp.concatenate([w_qa, w_in0[:, A_Q:]], axis=1)
    wo_a = w_out_even[0, :A_Q].reshape(A_Q // HEAD_DIM, HEAD_DIM, D_MODEL)[order].reshape(A_Q, D_MODEL)
    prm = {
        "g_mix0": norm_mix[0][None].astype(F32), "g_mix1": norm_mix[1][None].astype(F32),
        "g_ffn0": norm_ffn[0][None].astype(F32), "g_ffn1": norm_ffn[1][None].astype(F32),
        "g_final": norm_final[None].astype(F32),
        "w_in_even": w_in0.astype(BF16), "w_in_odd": w_in_odd[0].astype(BF16),
        "gq": jnp.tile(a_q_norm[0].astype(F32) * (SCALE * LOG2E), 2)[None], "gk": jnp.tile(a_k_norm[0].astype(F32), 2)[None],
        "bd": jnp.asarray((head[:, None] == head[None, :]) / HEAD_DIM, BF16),
        "rpb": na_rpb[0],
        "rope": _rope_tables(max(x_prompt.shape[1], x_sample.shape[1])),
        "wo_a": wo_a.astype(BF16), "wo_b": w_out_even[0, A_Q:].astype(BF16),
        "wo_odd": w_out_odd[0].astype(BF16),
        "wg0": w_gate_up[0, :, :D_FF].astype(BF16), "wu0": w_gate_up[0, :, D_FF:].astype(BF16),
        "wg1": w_gate_up[1, :, :D_FF].astype(BF16), "wu1": w_gate_up[1, :, D_FF:].astype(BF16),
        "wd0": w_down[0].astype(BF16), "wd1": w_down[1].astype(BF16),
        "slopes": jnp.exp2(-8.0 * (jnp.arange(C_HEADS, dtype=F32) + 1.0) / C_HEADS),
        "expand": jnp.asarray(np.tile(np.arange(C_HEADS), 2)[:, None] == (np.arange(C_W) // HEAD_DIM)[None, :], BF16),
    }
    return (_trunk(x_prompt, prm), _trunk(x_sample, prm))
```

```python
import functools

import jax
import jax.numpy as jnp
import numpy as np
from jax import lax
from jax.experimental import pallas as pl
from jax.experimental.pallas import tpu as pltpu

F32 = jnp.float32
BF16 = jnp.bfloat16

D_MODEL = 1024
HEAD_DIM = 64
A_Q = 512
A_KV = 128
B_W = 512
C_W = 1024
C_HEADS = 16
B_HEADS = 8
GRID_W = 64
NA_ROWS = 8
NA_COLS = 16
C_GROUPS = ((128, 1), (512, 4), (2048, 16))
ROPE_THETA = 10000.0
D_FF = 2816
EPS = 1e-6
NEG = -1e30
SCALE = HEAD_DIM ** -0.5
LOG2E = 1.4426950408889634

LANES = 128
BF16_ROWS = 16
MXU_TILE = 256
VMEM_BYTES = 64 * 1024 * 1024

V_ROWS = HEAD_DIM + BF16_ROWS
ROW_TILE = 512
A_TQ = 256
A_TK = 256
A_UNROLL = 16
A_AHEAD = 5
B_TQ = 256
B_TK = 3 * B_TQ
B_AHEAD = 4
C_TQ = 128
C_HALF = 64
IN_SLAB = MXU_TILE
IN_AHEAD = 2
C_AHEAD = 5
VMEM_LIMIT = VMEM_BYTES * 7 // 8

_NT = (((1,), (1,)), ((), ()))


def _resident(shape):
    zeros = (0,) * len(shape)
    return pl.BlockSpec(shape, lambda *_: zeros, pipeline_mode=pl.Buffered(1))


def _rms(x, gain):
    ms = jnp.mean(x * x, axis=-1, keepdims=True)
    return x * lax.rsqrt(ms + EPS) * gain


def _in_even_kernel(x_ref, g_ref, w_ref, gq_ref, gk_ref, cos_ref, sa_ref, sb_ref, bd_ref,
                    qa_ref, ka_ref, vat_ref, qb_ref, kb_ref, vbt_ref):
    hn = _rms(x_ref[...], g_ref[...]).astype(BF16)
    tm = hn.shape[0]
    cos, sa, sb, bd = cos_ref[...], sa_ref[...], sb_ref[...], bd_ref[...]

    def project(slab):
        cols = slice(slab * IN_SLAB, (slab + 1) * IN_SLAB)
        return jnp.dot(hn, w_ref[:, cols], preferred_element_type=F32)

    def headnorm_rope(c, gain):
        c2 = c * c
        hi = c2.astype(BF16)
        lo = (c2 - hi.astype(F32)).astype(BF16)
        ms = (jnp.dot(hi, bd, preferred_element_type=F32)
              + jnp.dot(lo, bd, preferred_element_type=F32))
        y = c * lax.rsqrt(ms + EPS) * gain
        return y * cos + pltpu.roll(y, 112, 1) * sa + pltpu.roll(y, 16, 1) * sb

    def put_transposed(dst_ref, chunk, c):
        ones = jnp.ones((V_ROWS - HEAD_DIM, A_TK), BF16)
        for j in range(tm // A_TK):
            t = chunk[j * A_TK:(j + 1) * A_TK, :].astype(BF16).T
            for hh in range(2):
                r0 = (2 * c + hh) * V_ROWS
                dst_ref[j, r0:r0 + HEAD_DIM, :] = t[hh * HEAD_DIM:(hh + 1) * HEAD_DIM, :]
                dst_ref[j, r0 + HEAD_DIM:r0 + V_ROWS, :] = ones

    def write_out(chunk, col):
        lanes = lambda base: slice(col - base, col - base + LANES)
        o = A_Q + 2 * A_KV
        if col < A_Q:
            qa_ref[:, lanes(0)] = headnorm_rope(chunk, gq_ref[...]).astype(BF16)
        elif col < A_Q + A_KV:
            ka_ref[...] = headnorm_rope(chunk, gk_ref[...]).astype(BF16)
        elif col < o:
            put_transposed(vat_ref, chunk, 0)
        elif col < o + B_W:
            qb_ref[:, lanes(o)] = (chunk * (SCALE * LOG2E)).astype(BF16)
        elif col < o + 2 * B_W:
            kb_ref[:, lanes(o + B_W)] = chunk.astype(BF16)
        else:
            put_transposed(vbt_ref, chunk, (col - o - 2 * B_W) // LANES)

    n_slabs = w_ref.shape[1] // IN_SLAB
    ahead = [project(s) for s in range(IN_AHEAD)]
    for slab in range(n_slabs):
        p = ahead.pop(0)
        if slab + IN_AHEAD < n_slabs:
            ahead.append(project(slab + IN_AHEAD))
        for i in range(IN_SLAB // LANES):
            write_out(p[:, i * LANES:(i + 1) * LANES], slab * IN_SLAB + i * LANES)


def _in_even(x2, seq, g, w, gq, gk, cos, sa, sb, bd):
    n = x2.shape[0]
    tm = ROW_TILE
    nt = seq // tm
    row = lambda i: (i, 0)
    pos = lambda i: (i % nt, 0)
    va_rows, vb_rows = (A_KV // HEAD_DIM) * V_ROWS, B_HEADS * V_ROWS
    out_shape = (
        jax.ShapeDtypeStruct((n, A_Q), BF16), jax.ShapeDtypeStruct((n, A_KV), BF16),
        jax.ShapeDtypeStruct((n // A_TK, va_rows, A_TK), BF16),
        jax.ShapeDtypeStruct((n, B_W), BF16), jax.ShapeDtypeStruct((n, B_W), BF16),
        jax.ShapeDtypeStruct((n // A_TK, vb_rows, A_TK), BF16))
    return pl.pallas_call(
        _in_even_kernel,
        grid=(n // tm,),
        in_specs=[pl.BlockSpec((tm, D_MODEL), row), _resident((1, D_MODEL)), _resident(w.shape),
                  _resident((1, LANES)), _resident((1, LANES)),
                  pl.BlockSpec((tm, LANES), pos), pl.BlockSpec((tm, LANES), pos), pl.BlockSpec((tm, LANES), pos),
                  _resident((LANES, LANES))],
        out_specs=(pl.BlockSpec((tm, A_Q), row), pl.BlockSpec((tm, A_KV), row),
                   pl.BlockSpec((tm // A_TK, va_rows, A_TK), lambda i: (i, 0, 0)),
                   pl.BlockSpec((tm, B_W), row), pl.BlockSpec((tm, B_W), row),
                   pl.BlockSpec((tm // A_TK, vb_rows, A_TK), lambda i: (i, 0, 0))),
        out_shape=out_shape,
        compiler_params=pltpu.CompilerParams(dimension_semantics=("parallel",), vmem_limit_bytes=VMEM_LIMIT),
        name="in_even",
    )(x2, g, w, gq, gk, cos, sa, sb, bd)


def _split_pair(qp):
    low = lax.broadcasted_iota(jnp.int32, qp.shape, 1) < HEAD_DIM
    zero = jnp.zeros(qp.shape, qp.dtype)
    return jnp.concatenate([jnp.where(low, qp, zero), jnp.where(low, zero, qp)], axis=0)


def _finish_pair(acc_lo, acc_hi):
    halves = [a[:HEAD_DIM] / a[HEAD_DIM:HEAD_DIM + 1] for a in (acc_lo, acc_hi)]
    return jnp.concatenate(halves, axis=0).astype(BF16).T


def _mix_a_kernel(q_ref, k_ref, vt_ref, o_ref, qs_ref, m_ref, acc_ref, *, n_chunks):
    tq = A_TQ
    n_pairs = A_Q // LANES
    for c in range(n_pairs):
        qs_ref[:, 2 * c * tq:(2 * c + 2) * tq] = (
            _split_pair(q_ref[:, c * LANES:(c + 1) * LANES]).T)
    m_ref[...] = jnp.full(m_ref.shape, NEG, F32)
    acc_ref[...] = jnp.zeros(acc_ref.shape, F32)

    def scores(ci, h):
        k = k_ref[pl.ds(pl.multiple_of(ci * A_TK, A_TK), A_TK), :]
        return jnp.dot(k, qs_ref[:, h * tq:(h + 1) * tq], preferred_element_type=F32)

    def chunks(it, carry):
        items = [(it * A_UNROLL + u, h) for u in range(A_UNROLL) for h in range(2 * n_pairs)]
        ahead = [scores(*item) for item in items[:A_AHEAD]]
        for t, (ci, h) in enumerate(items):
            s = ahead.pop(0)
            if t + A_AHEAD < len(items):
                ahead.append(scores(*items[t + A_AHEAD]))
            cols = slice(h * tq, (h + 1) * tq)
            g = h % 2
            m_old = m_ref[:, cols]
            m_new = jnp.maximum(m_old, jnp.max(s, axis=0, keepdims=True))
            alpha = jnp.exp2(m_old - m_new)
            p = jnp.exp2(s - m_new).astype(BF16)
            m_ref[:, cols] = m_new
            vt = vt_ref[ci, g * V_ROWS:(g + 1) * V_ROWS, :]
            acc_ref[h] = alpha * acc_ref[h] + jnp.dot(vt, p, preferred_element_type=F32)
        return carry

    lax.fori_loop(0, n_chunks // A_UNROLL, chunks, 0)

    for c in range(n_pairs):
        o_ref[:, c * LANES:(c + 1) * LANES] = _finish_pair(acc_ref[2 * c], acc_ref[2 * c + 1])


def _mix_a(qa, ka, vat, bn, seq):
    n = qa.shape[0]
    nq = seq // A_TQ
    nc = seq // A_TK
    assert seq % A_TQ == 0 and nc % A_UNROLL == 0, seq
    return pl.pallas_call(
        functools.partial(_mix_a_kernel, n_chunks=nc),
        grid=(bn, nq),
        in_specs=[pl.BlockSpec((A_TQ, A_Q), lambda b, i: (b * nq + i, 0)),
                  pl.BlockSpec((seq, A_KV), lambda b, i: (b, 0)),
                  pl.BlockSpec((nc, vat.shape[1], A_TK), lambda b, i: (b, 0, 0))],
        out_specs=pl.BlockSpec((A_TQ, A_Q), lambda b, i: (b * nq + i, 0)),
        out_shape=jax.ShapeDtypeStruct((n, A_Q), BF16),
        scratch_shapes=[pltpu.VMEM((LANES,8 * A_TQ), BF16), pltpu.VMEM((1, 8 * A_TQ), F32),
                        pltpu.VMEM((8, V_ROWS, A_TQ), F32)],
        compiler_params=pltpu.CompilerParams(dimension_semantics=("parallel", "parallel"),
                                             vmem_limit_bytes=VMEM_LIMIT),
        name="mix_a",
    )(qa, ka, vat)


def _mix_b_kernel(q_ref, k0_ref, k1_ref, k2_ref, vt0_ref, vt1_ref, vt2_ref, bias_ref, o_ref,
                  qs_ref, m_ref, acc_ref):
    tq = B_TQ
    n_pairs = B_W // LANES
    k_refs = (k0_ref, k1_ref, k2_ref)
    vt_refs = (vt0_ref, vt1_ref, vt2_ref)
    for c in range(n_pairs):
        qs_ref[:, 2 * c * tq:(2 * c + 2) * tq] = (
            _split_pair(q_ref[:, c * LANES:(c + 1) * LANES]).T)

    def scores(h, j):
        c = h // 2
        return jnp.dot(k_refs[j][:, c * LANES:(c + 1) * LANES], qs_ref[:, h * tq:(h + 1) * tq],
                       preferred_element_type=F32)

    items = [(h, j) for h in range(2 * n_pairs) for j in range(len(k_refs))]
    ahead = [scores(*item) for item in items[:B_AHEAD]]
    for t, (h, j) in enumerate(items):
        c, hh = divmod(h, 2)
        s = bias_ref[0, c, j * tq:(j + 1) * tq, hh * tq:(hh + 1) * tq] + ahead.pop(0)
        if t + B_AHEAD < len(items):
            ahead.append(scores(*items[t + B_AHEAD]))
        rows = slice(h * V_ROWS, (h + 1) * V_ROWS)
        m_new = jnp.max(s, axis=0, keepdims=True)
        if j > 0:
            m_old = m_ref[h]
            m_new = jnp.maximum(m_old, m_new)
        pv = jnp.dot(vt_refs[j][rows, :], jnp.exp2(s - m_new).astype(BF16), preferred_element_type=F32)
        acc_ref[h] = pv if j == 0 else jnp.exp2(m_old - m_new) * acc_ref[h] + pv
        m_ref[h] = m_new
        if j == len(k_refs) - 1 and hh == 1:
            o_ref[:, c * LANES:(c + 1) * LANES] = _finish_pair(acc_ref[h - 1], acc_ref[h])


def _mix_b(qb, kb, vbt, bias, bn, seq):
    n = qb.shape[0]
    nb = seq // B_TQ
    q_map = lambda b, i: (b * nb + i, 0)

    def kv_map(j):
        return lambda b, i: (b * nb + jnp.clip(i - 1, 0, nb - 3) + j, 0)

    def vt_map(j):
        return lambda b, i: (b * nb + jnp.clip(i - 1, 0, nb - 3) + j, 0, 0)

    def bias_map(b, i):
        return (jnp.where(i == 0, 0, jnp.where(i == nb - 1, 2, 1)), 0, 0, 0)

    blk = lambda m: pl.BlockSpec((B_TQ, B_W), m)
    vblk = lambda m: pl.BlockSpec((None, vbt.shape[1], B_TQ), m)
    return pl.pallas_call(
        _mix_b_kernel,
        grid=(bn, nb),
        in_specs=[blk(q_map), blk(kv_map(0)), blk(kv_map(1)), blk(kv_map(2)),
                  vblk(vt_map(0)), vblk(vt_map(1)), vblk(vt_map(2)),
                  pl.BlockSpec((1, B_W // LANES, B_TK, 2 * B_TQ), bias_map)],
        out_specs=blk(q_map),
        out_shape=jax.ShapeDtypeStruct((n, B_W), BF16),
        scratch_shapes=[pltpu.VMEM((LANES,B_HEADS * B_TQ), BF16), pltpu.VMEM((B_HEADS, 1, B_TQ), F32),
                        pltpu.VMEM((B_HEADS, V_ROWS, B_TQ), F32)],
        compiler_params=pltpu.CompilerParams(dimension_semantics=("parallel", "parallel"),
                                             vmem_limit_bytes=VMEM_LIMIT),
        name="mix_b",
    )(qb, kb, kb, kb, vbt, vbt, vbt, bias)


def _neighbourhood_bias(rpb, rows):
    n_dc = 2 * NA_COLS - 1
    c = np.arange(GRID_W)
    cs = np.clip(c - NA_COLS // 2, 0, GRID_W - NA_COLS)
    col_ok = (c[:, None] >= cs[None, :]) & (c[:, None] < cs[None, :] + NA_COLS)
    dc = np.clip(c[:, None] - c[None, :] + NA_COLS - 1, 0, n_dc - 1)
    pick = (dc.reshape(-1)[None, :] == np.arange(n_dc)[:, None]).astype(np.float32)
    by_col = jnp.einsum("hrd,dx->hrx", rpb.astype(F32) * LOG2E, jnp.asarray(pick),
                        precision=lax.Precision.HIGHEST)
    by_col = by_col.reshape(B_HEADS, 2 * NA_ROWS - 1, GRID_W, GRID_W)
    by_col = jnp.where(col_ok[None, None], by_col, NEG)
    masked = jnp.full((B_HEADS, GRID_W, GRID_W), NEG, F32)
    tables = []
    for r0, ws in ((0, 0), (4, 0), (rows - 4, rows - 12)):
        key_rows = []
        for kr in range(B_TK // GRID_W):
            blocks = []
            for rr in range(B_TQ // GRID_W):
                r = r0 + rr
                rs = min(max(r - NA_ROWS // 2, 0), rows - NA_ROWS)
                ok = rs <= ws + kr < rs + NA_ROWS
                blocks.append(by_col[:, ws + kr - r + NA_ROWS - 1] if ok else masked)
            key_rows.append(jnp.concatenate(blocks, axis=-1))
        t = jnp.concatenate(key_rows, axis=-2)
        t = t.reshape(B_HEADS // 2, 2, B_TK, B_TQ).transpose(0, 2, 1, 3).reshape(B_HEADS // 2, B_TK, 2 * B_TQ)
        tables.append(t)
    return jnp.stack(tables, axis=0)


def _ffn(x1, gain, wg_ref, wu_ref, wd_ref):
    hn = _rms(x1, gain).astype(BF16)
    gate = jnp.dot(hn, wg_ref[...], preferred_element_type=F32)
    up = jnp.dot(hn, wu_ref[...], preferred_element_type=F32)
    act = (gate * (1.0 / (1.0 + jnp.exp(-gate))) * up).astype(BF16)
    return x1 + jnp.dot(act, wd_ref[...], preferred_element_type=F32)


def _post_even_kernel(x_ref, oa_ref, ob_ref, woa_ref, wob_ref, g_ref, wg_ref, wu_ref, wd_ref, y_ref):
    mix = (jnp.dot(oa_ref[...], woa_ref[...], preferred_element_type=F32)
           + jnp.dot(ob_ref[...], wob_ref[...], preferred_element_type=F32))
    y_ref[...] = _ffn(x_ref[...] + mix, g_ref[...], wg_ref, wu_ref, wd_ref)


def _post_even(x2, oa, ob, woa, wob, g, wg, wu, wd):
    n = x2.shape[0]
    tm = ROW_TILE
    row = lambda i: (i, 0)
    return pl.pallas_call(
        _post_even_kernel,
        grid=(n // tm,),
        in_specs=[pl.BlockSpec((tm, D_MODEL), row), pl.BlockSpec((tm, A_Q), row), pl.BlockSpec((tm, B_W), row),
                  _resident(woa.shape), _resident(wob.shape), _resident((1, D_MODEL)),
                  _resident(wg.shape), _resident(wu.shape), _resident(wd.shape)],
        out_specs=pl.BlockSpec((tm, D_MODEL), row),
        out_shape=jax.ShapeDtypeStruct((n, D_MODEL), F32),
        compiler_params=pltpu.CompilerParams(dimension_semantics=("parallel",), vmem_limit_bytes=VMEM_LIMIT),
        name="post_even",
    )(x2, oa, ob, woa, wob, g, wg, wu, wd)


def _post_odd_kernel(x_ref, o1_ref, of2_ref, of3_ref, l1_ref, l2_ref, l3_ref, ex_ref, wo_ref, g_ref,
                     wg_ref, wu_ref, wd_ref, gf_ref, y_ref, o2_ref, o3_ref):
    tm = x_ref.shape[0]
    n_cb = C_W // LANES
    for (_, dil), src, dst in zip(C_GROUPS[1:], (of2_ref, of3_ref), (o2_ref, o3_ref)):
        for rho in range(dil):
            for cb in range(n_cb):
                dst[cb, pl.ds(rho, tm // dil, stride=dil), :] = src[0, rho, :, cb * LANES:(cb + 1) * LANES].astype(F32)
    o2 = jnp.concatenate([o2_ref[cb] for cb in range(n_cb)], axis=1)
    o3 = jnp.concatenate([o3_ref[cb] for cb in range(n_cb)], axis=1)
    l1, l2, l3 = l1_ref[...], l2_ref[...], l3_ref[...]
    mx = jnp.maximum(jnp.maximum(l1, l2), l3)
    e1, e2, e3 = jnp.exp2(l1 - mx), jnp.exp2(l2 - mx), jnp.exp2(l3 - mx)
    den = e1 + e2 + e3
    ex = ex_ref[...]

    def widen(w):
        hi = w.astype(BF16).astype(F32)
        return jnp.dot(jnp.concatenate([hi, w - hi], axis=1).astype(BF16), ex, preferred_element_type=F32)

    o = o3 + widen(e1 / den) * (o1_ref[...].astype(F32) - o3) + widen(e2 / den) * (o2 - o3)
    mix = jnp.dot(o.astype(BF16), wo_ref[...], preferred_element_type=F32)
    y = _ffn(x_ref[...] + mix, g_ref[...], wg_ref, wu_ref, wd_ref)
    y_ref[...] = _rms(y, gf_ref[...])


def _post_odd(x2, seq, os_, ls_, ex, wo, g, wg, wu, wd, gf):
    n = x2.shape[0]
    tm = ROW_TILE
    nt = seq // tm
    row = lambda i: (i, 0)
    wide = pl.BlockSpec((tm, D_MODEL), row)
    narrow = pl.BlockSpec((tm, C_HEADS), row)
    folded = [pl.BlockSpec((1, dil, tm // dil, C_W), lambda i: (i // nt, 0, i % nt, 0)) for _, dil in C_GROUPS[1:]]
    os_ = [os_[0].reshape(n, C_W)] + [o.reshape(n // seq, dil, seq // dil, C_W)
                                      for o, (_, dil) in zip(os_[1:], C_GROUPS[1:])]
    return pl.pallas_call(
        _post_odd_kernel,
        grid=(n // tm,),
        in_specs=[wide, wide, *folded, narrow, narrow, narrow, _resident(ex.shape), _resident(wo.shape),
                  _resident((1, D_MODEL)), _resident(wg.shape), _resident(wu.shape), _resident(wd.shape),
                  _resident((1, D_MODEL))],
        out_specs=wide,
        out_shape=jax.ShapeDtypeStruct((n, D_MODEL), F32),
        scratch_shapes=[pltpu.VMEM((C_W // LANES, tm, LANES), F32), pltpu.VMEM((C_W // LANES, tm, LANES), F32)],
        compiler_params=pltpu.CompilerParams(dimension_semantics=("parallel",), vmem_limit_bytes=VMEM_LIMIT),
        name="post_odd",
    )(x2, *os_, *ls_, ex, wo, g, wg, wu, wd, gf)


def _in_odd_kernel(x_ref, g_ref, w_ref, *refs):
    n_slabs = 3 * C_W // IN_SLAB
    out_refs, p_refs, f_refs = refs[:-4], refs[-4:-2], refs[-2:]
    tm = x_ref.shape[0]
    assert [d for _, d in C_GROUPS] == [1, 4, 16]
    hn = _rms(x_ref[...], g_ref[...]).astype(BF16)

    def project(slab):
        cols = slice(slab * IN_SLAB, (slab + 1) * IN_SLAB)
        return jnp.dot(hn, w_ref[:, cols], preferred_element_type=F32)

    ahead = [project(s) for s in range(IN_AHEAD)]
    for slab in range(n_slabs):
        p = ahead.pop(0)
        if slab + IN_AHEAD < n_slabs:
            ahead.append(project(slab + IN_AHEAD))
        part, col0 = divmod(slab * IN_SLAB, C_W)
        if part == 0:
            p = p * (SCALE * LOG2E)
        p_ref, f_ref = p_refs[slab % 2], f_refs[slab % 2]
        dst1, dst4, dst16 = out_refs[part * 3:part * 3 + 3]
        dst1[0, 0, :, col0:col0 + IN_SLAB] = p.astype(BF16)
        for cb in range(IN_SLAB // LANES):
            cols = slice(col0 + cb * LANES, col0 + (cb + 1) * LANES)
            p_ref[cb] = p[:, cb * LANES:(cb + 1) * LANES]
            for r4 in range(4):
                by4 = p_ref[cb, pl.ds(r4, tm // 4, stride=4), :]
                dst4[0, r4, :, cols] = by4.astype(BF16)
                f_ref[cb, r4] = by4
                for j in range(4):
                    by16 = f_ref[cb, r4, pl.ds(j, tm // 16, stride=4), :]
                    dst16[0, r4 + 4 * j, :, cols] = by16.astype(BF16)


def _in_odd(x2, bn, seq, g, w):
    n = x2.shape[0]
    tm = ROW_TILE
    nt = seq // tm
    specs, shapes = [], []
    for _ in range(3):
        for _, dil in C_GROUPS:
            specs.append(pl.BlockSpec((1, dil, tm // dil, C_W), lambda i: (i // nt, 0, i % nt, 0)))
            shapes.append(jax.ShapeDtypeStruct((bn, dil, seq // dil, C_W), BF16))
    outs = pl.pallas_call(
        _in_odd_kernel,
        grid=(n // tm,),
        in_specs=[pl.BlockSpec((tm, D_MODEL), lambda i: (i, 0)), _resident((1, D_MODEL)), _resident(w.shape)],
        out_specs=tuple(specs),
        out_shape=tuple(shapes),
        scratch_shapes=[pltpu.VMEM((IN_SLAB // LANES, tm, LANES), F32)] * 2
        + [pltpu.VMEM((IN_SLAB // LANES, 4, tm // 4, LANES), F32)] * 2,
        compiler_params=pltpu.CompilerParams(dimension_semantics=("parallel",), vmem_limit_bytes=VMEM_LIMIT),
        name="in_odd",
    )(x2, g, w)
    outs = [a.reshape(a.shape[0] * a.shape[1], a.shape[2], C_W) for a in outs]
    n_g = len(C_GROUPS)
    return outs[:n_g], outs[n_g:2 * n_g], outs[2 * n_g:]


def _band_kernel(sl_ref, q_ref, kp_ref, kc_ref, kn_ref, vp_ref, vc_ref, vn_ref, o_ref, lse_ref,
                 kwin_ref, vt_ref, pen_ref, *, length, tile):
    t = pl.program_id(1)
    n_pairs = C_W // LANES
    win = C_TQ + 2 * C_HALF
    span = tile + 2 * C_HALF
    kwin_ref[0:C_HALF, :] = kp_ref[C_TQ - C_HALF:C_TQ, :]
    kwin_ref[C_HALF:C_HALF + tile, :] = kc_ref[...]
    kwin_ref[C_HALF + tile:span, :] = kn_ref[0:C_HALF, :]

    @pl.when((pl.program_id(0) == 0) & (t == 0))
    def _():
        key = lax.broadcasted_iota(jnp.int32, (win, 2 * C_TQ), 0)
        lane = lax.broadcasted_iota(jnp.int32, (win, 2 * C_TQ), 1)
        dist = jnp.abs(key - C_HALF - (lane & (C_TQ - 1)))
        distf = dist.astype(F32)
        for c in range(n_pairs):
            slope = jnp.where(lane < C_TQ, sl_ref[2 * c], sl_ref[2 * c + 1])
            pen = jnp.where(dist <= C_HALF, slope * distf, -NEG)
            pen_ref[0, c] = pen
            pen_ref[1, c] = jnp.where(key >= C_HALF, pen, -NEG)
            pen_ref[2, c] = jnp.where(key < win - C_HALF, pen, -NEG)

    ones = jnp.ones((V_ROWS - HEAD_DIM, span), BF16)
    for c in range(n_pairs):
        cs = slice(c * LANES, (c + 1) * LANES)
        vwin = jnp.concatenate([vp_ref[C_TQ - C_HALF:C_TQ, cs], vc_ref[:, cs], vn_ref[0:C_HALF, cs]], axis=0)
        vt = vwin.T
        for hh in range(2):
            vt_ref[c, hh * V_ROWS:hh * V_ROWS + HEAD_DIM, :] = vt[hh * HEAD_DIM:(hh + 1) * HEAD_DIM, :]
            vt_ref[c, hh * V_ROWS + HEAD_DIM:(hh + 1) * V_ROWS, :] = ones

    def scores(j, c):
        cs = slice(c * LANES, (c + 1) * LANES)
        qs = _split_pair(q_ref[j * C_TQ:(j + 1) * C_TQ, cs])
        k = kwin_ref[j * C_TQ:j * C_TQ + win, cs]
        return lax.dot_general(k, qs, _NT, preferred_element_type=F32)

    n_blocks = tile // C_TQ
    assert n_blocks >= 2
    table = {0: jnp.where(t == 0, 1, 0), n_blocks - 1: jnp.where(t == length // tile - 1, 2, 0)}
    items = [(j, c) for j in range(n_blocks) for c in range(n_pairs)]
    ahead = [scores(*item) for item in items[:C_AHEAD]]
    for n_item, (j, c) in enumerate(items):
        rows = slice(j * C_TQ, (j + 1) * C_TQ)
        cs = slice(c * LANES, (c + 1) * LANES)
        s = ahead.pop(0) - pen_ref[table.get(j, 0), c]
        if n_item + C_AHEAD < len(items):
            ahead.append(scores(*items[n_item + C_AHEAD]))
        m = jnp.max(s, axis=0, keepdims=True)
        p = jnp.exp2(s - m).astype(BF16)
        acc = jnp.dot(vt_ref[c, :, j * C_TQ:j * C_TQ + win], p, preferred_element_type=F32)
        a0, a1 = acc[:V_ROWS, :C_TQ], acc[V_ROWS:, C_TQ:]
        o_ref[rows, cs] = _finish_pair(a0, a1)
        lse_ref[2 * c:2 * c + 1, rows] = m[:, :C_TQ] + jnp.log2(a0[HEAD_DIM:HEAD_DIM + 1])
        lse_ref[2 * c + 1:2 * c + 2, rows] = m[:, C_TQ:] + jnp.log2(a1[HEAD_DIM:HEAD_DIM + 1])


def _band(slopes, q, k, v):
    nb_, length, _ = q.shape
    tile = 512 if length % 512 == 0 else 256
    per = tile // C_TQ
    last = length // C_TQ - 1
    cur = lambda n, t: (n, t, 0)
    prev = lambda n, t: (n, jnp.maximum(t * per - 1, 0), 0)
    nxt = lambda n, t: (n, jnp.minimum((t + 1) * per, last), 0)
    big = lambda m: pl.BlockSpec((None, tile, C_W), m)
    halo = lambda m: pl.BlockSpec((None, C_TQ, C_W), m)
    span = tile + 2 * C_HALF
    return pl.pallas_call(
        functools.partial(_band_kernel, length=length, tile=tile),
        grid=(nb_, length // tile),
        in_specs=[pl.BlockSpec(memory_space=pltpu.SMEM), big(cur),
                  halo(prev), big(cur), halo(nxt), halo(prev), big(cur), halo(nxt)],
        out_specs=(big(cur), pl.BlockSpec((None, C_HEADS, tile), lambda n, t: (n, 0, t))),
        out_shape=(jax.ShapeDtypeStruct((nb_, length, C_W), BF16),
                   jax.ShapeDtypeStruct((nb_, C_HEADS, length), F32)),
        scratch_shapes=[pltpu.VMEM((span, C_W), BF16), pltpu.VMEM((C_W // LANES, 2 * V_ROWS, span), BF16),
                        pltpu.VMEM((3, C_W // LANES, C_TQ + 2 * C_HALF, 2 * C_TQ), F32)],
        compiler_params=pltpu.CompilerParams(dimension_semantics=("arbitrary", "arbitrary"),
                                             vmem_limit_bytes=VMEM_LIMIT),
        name="band",
    )(slopes, q, k, k, k, v, v, v)


def _rope_tables(seq):
    t = jnp.arange(seq)
    n = HEAD_DIM // 4
    freqs = jnp.power(ROPE_THETA, -jnp.arange(n, dtype=F32) / n)
    ang_r = (t // GRID_W).astype(F32)[:, None] * freqs[None, :]
    ang_c = (t % GRID_W).astype(F32)[:, None] * freqs[None, :]
    zero = jnp.zeros_like(ang_r)
    cos = jnp.concatenate([jnp.cos(ang_r)] * 2 + [jnp.cos(ang_c)] * 2, axis=-1)
    sin_r, sin_c = jnp.sin(ang_r), jnp.sin(ang_c)
    sa = jnp.concatenate([-sin_r, zero, -sin_c, zero], axis=-1)
    sb = jnp.concatenate([zero, sin_r, zero, sin_c], axis=-1)
    two = lambda a: jnp.concatenate([a, a], axis=-1)
    return two(cos), two(sa), two(sb)


def _trunk(x, prm):
    bn, seq, _ = x.shape
    n = bn * seq
    assert x.shape[2] == D_MODEL and seq % (16 * 256) == 0, x.shape
    x2 = x.reshape(n, D_MODEL)
    cos, sa, sb = prm["rope"]
    qa, ka, vat, qb, kb, vbt = _in_even(x2, seq, prm["g_mix0"], prm["w_in_even"], prm["gq"], prm["gk"],
                                        cos, sa, sb, prm["bd"])
    oa = _mix_a(qa, ka, vat, bn, seq)
    ob = _mix_b(qb, kb, vbt, _neighbourhood_bias(prm["rpb"], seq // GRID_W), bn, seq)
    x2 = _post_even(x2, oa, ob, prm["wo_a"], prm["wo_b"], prm["g_ffn0"], prm["wg0"], prm["wu0"], prm["wd0"])

    qs, ks, vs = _in_odd(x2, bn, seq, prm["g_mix1"], prm["w_in_odd"])
    outs, lses = [], []
    for (_, dil), q, k, v in zip(C_GROUPS, qs, ks, vs):
        o, lse = _band(prm["slopes"] * dil * LOG2E, q, k, v)
        outs.append(o)
        lses.append(lse.reshape(bn, dil, C_HEADS, seq // dil).transpose(0, 3, 1, 2).reshape(n, C_HEADS))
    y = _post_odd(x2, seq, outs, lses, prm["expand"], prm["wo_odd"], prm["g_ffn1"], prm["wg1"], prm["wu1"],
                  prm["wd1"], prm["g_final"])
    return y.reshape(bn, seq, D_MODEL)


def kernel(x_prompt, x_sample, norm_mix, norm_ffn, norm_final, w_in_even, a_q_norm, a_k_norm, na_rpb,
           w_out_even, w_in_odd, w_out_odd, w_gate_up, w_down):
    head = np.arange(LANES) // HEAD_DIM
    order = np.array([0, 4, 1, 5, 2, 6, 3, 7])
    w_in0 = w_in_even[0]
    w_qa = w_in0[:, :A_Q].reshape(D_MODEL, A_Q // HEAD_DIM, HEAD_DIM)[:, order].reshape(D_MODEL, A_Q)
    w_in0 = jnp.concatenate([w_qa, w_in0[:, A_Q:]], axis=1)
    wo_a = w_out_even[0, :A_Q].reshape(A_Q // HEAD_DIM, HEAD_DIM, D_MODEL)[order].reshape(A_Q, D_MODEL)
    prm = {
        "g_mix0": norm_mix[0][None].astype(F32), "g_mix1": norm_mix[1][None].astype(F32),
        "g_ffn0": norm_ffn[0][None].astype(F32), "g_ffn1": norm_ffn[1][None].astype(F32),
        "g_final": norm_final[None].astype(F32),
        "w_in_even": w_in0.astype(BF16), "w_in_odd": w_in_odd[0].astype(BF16),
        "gq": jnp.tile(a_q_norm[0].astype(F32) * (SCALE * LOG2E), 2)[None], "gk": jnp.tile(a_k_norm[0].astype(F32), 2)[None],
        "bd": jnp.asarray((head[:, None] == head[None, :]) / HEAD_DIM, BF16),
        "rpb": na_rpb[0],
        "rope": _rope_tables(max(x_prompt.shape[1], x_sample.shape[1])),
        "wo_a": wo_a.astype(BF16), "wo_b": w_out_even[0, A_Q:].astype(BF16),
        "wo_odd": w_out_odd[0].astype(BF16),
        "wg0": w_gate_up[0, :, :D_FF].astype(BF16), "wu0": w_gate_up[0, :, D_FF:].astype(BF16),
        "wg1": w_gate_up[1, :, :D_FF].astype(BF16), "wu1": w_gate_up[1, :, D_FF:].astype(BF16),
        "wd0": w_down[0].astype(BF16), "wd1": w_down[1].astype(BF16),
        "slopes": jnp.exp2(-8.0 * (jnp.arange(C_HEADS, dtype=F32) + 1.0) / C_HEADS),
        "expand": jnp.asarray(np.tile(np.arange(C_HEADS), 2)[:, None] == (np.arange(C_W) // HEAD_DIM)[None, :], BF16),
    }
    return (_trunk(x_prompt, prm), _trunk(x_sample, prm))
```

```python
import functools

import jax
import jax.numpy as jnp
import numpy as np
from jax import lax
from jax.experimental import pallas as pl
from jax.experimental.pallas import tpu as pltpu

F32 = jnp.float32
BF16 = jnp.bfloat16

D_MODEL = 1024
HEAD_DIM = 64
A_Q = 512
A_KV = 128
B_W = 512
C_W = 1024
C_HEADS = 16
B_HEADS = 8
A_HEADS = A_Q // HEAD_DIM
GRID_W = 64
NA_ROWS = 8
NA_COLS = 16
C_GROUPS = ((128, 1), (512, 4), (2048, 16))
ROPE_THETA = 10000.0
D_FF = 2816
EPS = 1e-6
NEG = -1e30
SCALE = HEAD_DIM ** -0.5
LOG2E = 1.4426950408889634
ROPE_HALF = HEAD_DIM // 4

LANES = 128
BF16_ROWS = 16
MXU_TILE = 256
VMEM_BYTES = 64 * 1024 * 1024

V_ROWS = HEAD_DIM + BF16_ROWS
ROW_TILE = 512
A_TQ = 256
A_TK = 256
A_UNROLL = 16
A_AHEAD = 5
B_TQ = 256
B_TK = 3 * B_TQ
B_AHEAD = 4
C_TQ = 128
C_HALF = 64
IN_SLAB = MXU_TILE
IN_AHEAD = 2
C_AHEAD = 5
VMEM_LIMIT = VMEM_BYTES * 7 // 8

_NT = (((1,), (1,)), ((), ()))


def _resident(shape):
    zeros = (0,) * len(shape)
    return pl.BlockSpec(shape, lambda *_: zeros, pipeline_mode=pl.Buffered(1))


def _rms(x, gain):
    ms = jnp.mean(x * x, axis=-1, keepdims=True)
    return x * lax.rsqrt(ms + EPS) * gain


def _in_even_kernel(x_ref, g_ref, w_ref, gq_ref, gk_ref, cos_ref, sa_ref, sb_ref, bd_ref,
                    qa_ref, ka_ref, vat_ref, qb_ref, kb_ref, vbt_ref):
    hn = _rms(x_ref[...], g_ref[...]).astype(BF16)
    tm = hn.shape[0]
    cos, sa, sb, bd = cos_ref[...], sa_ref[...], sb_ref[...], bd_ref[...]

    def project(slab):
        cols = slice(slab * IN_SLAB, (slab + 1) * IN_SLAB)
        return jnp.dot(hn, w_ref[:, cols], preferred_element_type=F32)

    def headnorm_rope(c, gain):
        c2 = c * c
        hi = c2.astype(BF16)
        lo = (c2 - hi.astype(F32)).astype(BF16)
        ms = (jnp.dot(hi, bd, preferred_element_type=F32)
              + jnp.dot(lo, bd, preferred_element_type=F32))
        y = c * lax.rsqrt(ms + EPS) * gain
        return y * cos + pltpu.roll(y, LANES - ROPE_HALF, 1) * sa + pltpu.roll(y, ROPE_HALF, 1) * sb

    def put_transposed(dst_ref, chunk, c):
        ones = jnp.ones((V_ROWS - HEAD_DIM, A_TK), BF16)
        for j in range(tm // A_TK):
            t = chunk[j * A_TK:(j + 1) * A_TK, :].astype(BF16).T
            for hh in range(2):
                r0 = (2 * c + hh) * V_ROWS
                dst_ref[j, r0:r0 + HEAD_DIM, :] = t[hh * HEAD_DIM:(hh + 1) * HEAD_DIM, :]
                dst_ref[j, r0 + HEAD_DIM:r0 + V_ROWS, :] = ones

    def write_out(chunk, col):
        lanes = lambda base: slice(col - base, col - base + LANES)
        o = A_Q + 2 * A_KV
        if col < A_Q:
            qa_ref[:, lanes(0)] = headnorm_rope(chunk, gq_ref[...]).astype(BF16)
        elif col < A_Q + A_KV:
            ka_ref[...] = headnorm_rope(chunk, gk_ref[...]).astype(BF16)
        elif col < o:
            put_transposed(vat_ref, chunk, 0)
        elif col < o + B_W:
            qb_ref[:, lanes(o)] = (chunk * (SCALE * LOG2E)).astype(BF16)
        elif col < o + 2 * B_W:
            kb_ref[:, lanes(o + B_W)] = chunk.astype(BF16)
        else:
            put_transposed(vbt_ref, chunk, (col - o - 2 * B_W) // LANES)

    n_slabs = w_ref.shape[1] // IN_SLAB
    ahead = [project(s) for s in range(IN_AHEAD)]
    for slab in range(n_slabs):
        p = ahead.pop(0)
        if slab + IN_AHEAD < n_slabs:
            ahead.append(project(slab + IN_AHEAD))
        for i in range(IN_SLAB // LANES):
            write_out(p[:, i * LANES:(i + 1) * LANES], slab * IN_SLAB + i * LANES)


def _in_even(x2, seq, g, w, gq, gk, cos, sa, sb, bd):
    n = x2.shape[0]
    tm = ROW_TILE
    nt = seq // tm
    row = lambda i: (i, 0)
    pos = lambda i: (i % nt, 0)
    va_rows, vb_rows = (A_KV // HEAD_DIM) * V_ROWS, B_HEADS * V_ROWS
    out_shape = (
        jax.ShapeDtypeStruct((n, A_Q), BF16), jax.ShapeDtypeStruct((n, A_KV), BF16),
        jax.ShapeDtypeStruct((n // A_TK, va_rows, A_TK), BF16),
        jax.ShapeDtypeStruct((n, B_W), BF16), jax.ShapeDtypeStruct((n, B_W), BF16),
        jax.ShapeDtypeStruct((n // A_TK, vb_rows, A_TK), BF16))
    return pl.pallas_call(
        _in_even_kernel,
        grid=(n // tm,),
        in_specs=[pl.BlockSpec((tm, D_MODEL), row), _resident((1, D_MODEL)), _resident(w.shape),
                  _resident((1, LANES)), _resident((1, LANES)),
                  pl.BlockSpec((tm, LANES), pos), pl.BlockSpec((tm, LANES), pos), pl.BlockSpec((tm, LANES), pos),
                  _resident((LANES, LANES))],
        out_specs=(pl.BlockSpec((tm, A_Q), row), pl.BlockSpec((tm, A_KV), row),
                   pl.BlockSpec((tm // A_TK, va_rows, A_TK), lambda i: (i, 0, 0)),
                   pl.BlockSpec((tm, B_W), row), pl.BlockSpec((tm, B_W), row),
                   pl.BlockSpec((tm // A_TK, vb_rows, A_TK), lambda i: (i, 0, 0))),
        out_shape=out_shape,
        compiler_params=pltpu.CompilerParams(dimension_semantics=("parallel",), vmem_limit_bytes=VMEM_LIMIT),
        name="in_even",
    )(x2, g, w, gq, gk, cos, sa, sb, bd)


def _split_pair(qp):
    low = lax.broadcasted_iota(jnp.int32, qp.shape, 1) < HEAD_DIM
    zero = jnp.zeros(qp.shape, qp.dtype)
    return jnp.concatenate([jnp.where(low, qp, zero), jnp.where(low, zero, qp)], axis=0)


def _finish_pair(acc_lo, acc_hi):
    halves = [a[:HEAD_DIM] / a[HEAD_DIM:HEAD_DIM + 1] for a in (acc_lo, acc_hi)]
    return jnp.concatenate(halves, axis=0).astype(BF16).T


def _mix_a_kernel(q_ref, k_ref, vt_ref, o_ref, qs_ref, m_ref, acc_ref, *, n_chunks):
    tq = A_TQ
    n_pairs = A_Q // LANES
    for c in range(n_pairs):
        qs_ref[:, 2 * c * tq:(2 * c + 2) * tq] = (
            _split_pair(q_ref[:, c * LANES:(c + 1) * LANES]).T)
    m_ref[...] = jnp.full(m_ref.shape, NEG, F32)
    acc_ref[...] = jnp.zeros(acc_ref.shape, F32)

    def scores(ci, h):
        k = k_ref[pl.ds(pl.multiple_of(ci * A_TK, A_TK), A_TK), :]
        return jnp.dot(k, qs_ref[:, h * tq:(h + 1) * tq], preferred_element_type=F32)

    def chunks(it, carry):
        items = [(it * A_UNROLL + u, h) for u in range(A_UNROLL) for h in range(2 * n_pairs)]
        ahead = [scores(*item) for item in items[:A_AHEAD]]
        for t, (ci, h) in enumerate(items):
            s = ahead.pop(0)
            if t + A_AHEAD < len(items):
                ahead.append(scores(*items[t + A_AHEAD]))
            cols = slice(h * tq, (h + 1) * tq)
            g = h % 2
            m_old = m_ref[:, cols]
            m_new = jnp.maximum(m_old, jnp.max(s, axis=0, keepdims=True))
            alpha = jnp.exp2(m_old - m_new)
            p = jnp.exp2(s - m_new).astype(BF16)
            m_ref[:, cols] = m_new
            vt = vt_ref[ci, g * V_ROWS:(g + 1) * V_ROWS, :]
            acc_ref[h] = alpha * acc_ref[h] + jnp.dot(vt, p, preferred_element_type=F32)
        return carry

    lax.fori_loop(0, n_chunks // A_UNROLL, chunks, 0)

    for c in range(n_pairs):
        o_ref[:, c * LANES:(c + 1) * LANES] = _finish_pair(acc_ref[2 * c], acc_ref[2 * c + 1])


def _mix_a(qa, ka, vat, bn, seq):
    n = qa.shape[0]
    nq = seq // A_TQ
    nc = seq // A_TK
    assert seq % A_TQ == 0 and nc % A_UNROLL == 0, seq
    return pl.pallas_call(
        functools.partial(_mix_a_kernel, n_chunks=nc),
        grid=(bn, nq),
        in_specs=[pl.BlockSpec((A_TQ, A_Q), lambda b, i: (b * nq + i, 0)),
                  pl.BlockSpec((seq, A_KV), lambda b, i: (b, 0)),
                  pl.BlockSpec((nc, vat.shape[1], A_TK), lambda b, i: (b, 0, 0))],
        out_specs=pl.BlockSpec((A_TQ, A_Q), lambda b, i: (b * nq + i, 0)),
        out_shape=jax.ShapeDtypeStruct((n, A_Q), BF16),
        scratch_shapes=[pltpu.VMEM((LANES, A_HEADS * A_TQ), BF16), pltpu.VMEM((1, A_HEADS * A_TQ), F32),
                        pltpu.VMEM((A_HEADS, V_ROWS, A_TQ), F32)],
        compiler_params=pltpu.CompilerParams(dimension_semantics=("parallel", "parallel"),
                                             vmem_limit_bytes=VMEM_LIMIT),
        name="mix_a",
    )(qa, ka, vat)


def _mix_b_kernel(q_ref, k0_ref, k1_ref, k2_ref, vt0_ref, vt1_ref, vt2_ref, bias_ref, o_ref,
                  qs_ref, m_ref, acc_ref):
    tq = B_TQ
    n_pairs = B_W // LANES
    k_refs = (k0_ref, k1_ref, k2_ref)
    vt_refs = (vt0_ref, vt1_ref, vt2_ref)
    for c in range(n_pairs):
        qs_ref[:, 2 * c * tq:(2 * c + 2) * tq] = (
            _split_pair(q_ref[:, c * LANES:(c + 1) * LANES]).T)

    def scores(h, j):
        c = h // 2
        return jnp.dot(k_refs[j][:, c * LANES:(c + 1) * LANES], qs_ref[:, h * tq:(h + 1) * tq],
                       preferred_element_type=F32)

    items = [(h, j) for h in range(2 * n_pairs) for j in range(len(k_refs))]
    ahead = [scores(*item) for item in items[:B_AHEAD]]
    for t, (h, j) in enumerate(items):
        c, hh = divmod(h, 2)
        s = bias_ref[0, c, j * tq:(j + 1) * tq, hh * tq:(hh + 1) * tq] + ahead.pop(0)
        if t + B_AHEAD < len(items):
            ahead.append(scores(*items[t + B_AHEAD]))
        rows = slice(h * V_ROWS, (h + 1) * V_ROWS)
        m_new = jnp.max(s, axis=0, keepdims=True)
        if j > 0:
            m_old = m_ref[h]
            m_new = jnp.maximum(m_old, m_new)
        pv = jnp.dot(vt_refs[j][rows, :], jnp.exp2(s - m_new).astype(BF16), preferred_element_type=F32)
        acc_ref[h] = pv if j == 0 else jnp.exp2(m_old - m_new) * acc_ref[h] + pv
        m_ref[h] = m_new
        if j == len(k_refs) - 1 and hh == 1:
            o_ref[:, c * LANES:(c + 1) * LANES] = _finish_pair(acc_ref[h - 1], acc_ref[h])


def _mix_b(qb, kb, vbt, bias, bn, seq):
    n = qb.shape[0]
    nb = seq // B_TQ
    q_map = lambda b, i: (b * nb + i, 0)

    def kv_map(j):
        return lambda b, i: (b * nb + jnp.clip(i - 1, 0, nb - 3) + j, 0)

    def vt_map(j):
        return lambda b, i: (b * nb + jnp.clip(i - 1, 0, nb - 3) + j, 0, 0)

    def bias_map(b, i):
        return (jnp.where(i == 0, 0, jnp.where(i == nb - 1, 2, 1)), 0, 0, 0)

    blk = lambda m: pl.BlockSpec((B_TQ, B_W), m)
    vblk = lambda m: pl.BlockSpec((None, vbt.shape[1], B_TQ), m)
    return pl.pallas_call(
        _mix_b_kernel,
        grid=(bn, nb),
        in_specs=[blk(q_map), blk(kv_map(0)), blk(kv_map(1)), blk(kv_map(2)),
                  vblk(vt_map(0)), vblk(vt_map(1)), vblk(vt_map(2)),
                  pl.BlockSpec((1, B_W // LANES, B_TK, 2 * B_TQ), bias_map)],
        out_specs=blk(q_map),
        out_shape=jax.ShapeDtypeStruct((n, B_W), BF16),
        scratch_shapes=[pltpu.VMEM((LANES, B_HEADS * B_TQ), BF16), pltpu.VMEM((B_HEADS, 1, B_TQ), F32),
                        pltpu.VMEM((B_HEADS, V_ROWS, B_TQ), F32)],
        compiler_params=pltpu.CompilerParams(dimension_semantics=("parallel", "parallel"),
                                             vmem_limit_bytes=VMEM_LIMIT),
        name="mix_b",
    )(qb, kb, kb, kb, vbt, vbt, vbt, bias)


def _neighbourhood_bias(rpb, rows):
    n_dc = 2 * NA_COLS - 1
    c = np.arange(GRID_W)
    cs = np.clip(c - NA_COLS // 2, 0, GRID_W - NA_COLS)
    col_ok = (c[:, None] >= cs[None, :]) & (c[:, None] < cs[None, :] + NA_COLS)
    dc = np.clip(c[:, None] - c[None, :] + NA_COLS - 1, 0, n_dc - 1)
    pick = (dc.reshape(-1)[None, :] == np.arange(n_dc)[:, None]).astype(np.float32)
    by_col = jnp.einsum("hrd,dx->hrx", rpb.astype(F32) * LOG2E, jnp.asarray(pick),
                        precision=lax.Precision.HIGHEST)
    by_col = by_col.reshape(B_HEADS, 2 * NA_ROWS - 1, GRID_W, GRID_W)
    by_col = jnp.where(col_ok[None, None], by_col, NEG)
    masked = jnp.full((B_HEADS, GRID_W, GRID_W), NEG, F32)
    tables = []
    for r0, ws in ((0, 0), (4, 0), (rows - 4, rows - 12)):
        key_rows = []
        for kr in range(B_TK // GRID_W):
            blocks = []
            for rr in range(B_TQ // GRID_W):
                r = r0 + rr
                rs = min(max(r - NA_ROWS // 2, 0), rows - NA_ROWS)
                ok = rs <= ws + kr < rs + NA_ROWS
                blocks.append(by_col[:, ws + kr - r + NA_ROWS - 1] if ok else masked)
            key_rows.append(jnp.concatenate(blocks, axis=-1))
        t = jnp.concatenate(key_rows, axis=-2)
        t = t.reshape(B_HEADS // 2, 2, B_TK, B_TQ).transpose(0, 2, 1, 3).reshape(B_HEADS // 2, B_TK, 2 * B_TQ)
        tables.append(t)
    return jnp.stack(tables, axis=0)


def _ffn(x1, gain, wg_ref, wu_ref, wd_ref):
    hn = _rms(x1, gain).astype(BF16)
    gate = jnp.dot(hn, wg_ref[...], preferred_element_type=F32)
    up = jnp.dot(hn, wu_ref[...], preferred_element_type=F32)
    act = (gate * (1.0 / (1.0 + jnp.exp(-gate))) * up).astype(BF16)
    return x1 + jnp.dot(act, wd_ref[...], preferred_element_type=F32)


def _post_even_kernel(x_ref, oa_ref, ob_ref, woa_ref, wob_ref, g_ref, wg_ref, wu_ref, wd_ref, y_ref):
    mix = (jnp.dot(oa_ref[...], woa_ref[...], preferred_element_type=F32)
           + jnp.dot(ob_ref[...], wob_ref[...], preferred_element_type=F32))
    y_ref[...] = _ffn(x_ref[...] + mix, g_ref[...], wg_ref, wu_ref, wd_ref)


def _post_even(x2, oa, ob, woa, wob, g, wg, wu, wd):
    n = x2.shape[0]
    tm = ROW_TILE
    row = lambda i: (i, 0)
    return pl.pallas_call(
        _post_even_kernel,
        grid=(n // tm,),
        in_specs=[pl.BlockSpec((tm, D_MODEL), row), pl.BlockSpec((tm, A_Q), row), pl.BlockSpec((tm, B_W), row),
                  _resident(woa.shape), _resident(wob.shape), _resident((1, D_MODEL)),
                  _resident(wg.shape), _resident(wu.shape), _resident(wd.shape)],
        out_specs=pl.BlockSpec((tm, D_MODEL), row),
        out_shape=jax.ShapeDtypeStruct((n, D_MODEL), F32),
        compiler_params=pltpu.CompilerParams(dimension_semantics=("parallel",), vmem_limit_bytes=VMEM_LIMIT),
        name="post_even",
    )(x2, oa, ob, woa, wob, g, wg, wu, wd)


def _post_odd_kernel(x_ref, o1_ref, of2_ref, of3_ref, l1_ref, l2_ref, l3_ref, ex_ref, wo_ref, g_ref,
                     wg_ref, wu_ref, wd_ref, gf_ref, y_ref, o2_ref, o3_ref):
    tm = x_ref.shape[0]
    n_cb = C_W // LANES
    for (_, dil), src, dst in zip(C_GROUPS[1:], (of2_ref, of3_ref), (o2_ref, o3_ref)):
        for rho in range(dil):
            for cb in range(n_cb):
                dst[cb, pl.ds(rho, tm // dil, stride=dil), :] = src[0, rho, :, cb * LANES:(cb + 1) * LANES].astype(F32)
    o2 = jnp.concatenate([o2_ref[cb] for cb in range(n_cb)], axis=1)
    o3 = jnp.concatenate([o3_ref[cb] for cb in range(n_cb)], axis=1)
    l1, l2, l3 = l1_ref[...], l2_ref[...], l3_ref[...]
    mx = jnp.maximum(jnp.maximum(l1, l2), l3)
    e1, e2, e3 = jnp.exp2(l1 - mx), jnp.exp2(l2 - mx), jnp.exp2(l3 - mx)
    den = e1 + e2 + e3
    ex = ex_ref[...]

    def widen(w):
        hi = w.astype(BF16).astype(F32)
        return jnp.dot(jnp.concatenate([hi, w - hi], axis=1).astype(BF16), ex, preferred_element_type=F32)

    o = o3 + widen(e1 / den) * (o1_ref[...].astype(F32) - o3) + widen(e2 / den) * (o2 - o3)
    mix = jnp.dot(o.astype(BF16), wo_ref[...], preferred_element_type=F32)
    y = _ffn(x_ref[...] + mix, g_ref[...], wg_ref, wu_ref, wd_ref)
    y_ref[...] = _rms(y, gf_ref[...])


def _post_odd(x2, seq, os_, ls_, ex, wo, g, wg, wu, wd, gf):
    n = x2.shape[0]
    tm = ROW_TILE
    nt = seq // tm
    row = lambda i: (i, 0)
    wide = pl.BlockSpec((tm, D_MODEL), row)
    narrow = pl.BlockSpec((tm, C_HEADS), row)
    folded = [pl.BlockSpec((1, dil, tm // dil, C_W), lambda i: (i // nt, 0, i % nt, 0)) for _, dil in C_GROUPS[1:]]
    os_ = [os_[0].reshape(n, C_W)] + [o.reshape(n // seq, dil, seq // dil, C_W)
                                      for o, (_, dil) in zip(os_[1:], C_GROUPS[1:])]
    return pl.pallas_call(
        _post_odd_kernel,
        grid=(n // tm,),
        in_specs=[wide, wide, *folded, narrow, narrow, narrow, _resident(ex.shape), _resident(wo.shape),
                  _resident((1, D_MODEL)), _resident(wg.shape), _resident(wu.shape), _resident(wd.shape),
                  _resident((1, D_MODEL))],
        out_specs=wide,
        out_shape=jax.ShapeDtypeStruct((n, D_MODEL), F32),
        scratch_shapes=[pltpu.VMEM((C_W // LANES, tm, LANES), F32), pltpu.VMEM((C_W // LANES, tm, LANES), F32)],
        compiler_params=pltpu.CompilerParams(dimension_semantics=("parallel",), vmem_limit_bytes=VMEM_LIMIT),
        name="post_odd",
    )(x2, *os_, *ls_, ex, wo, g, wg, wu, wd, gf)


def _in_odd_kernel(x_ref, g_ref, w_ref, *refs):
    n_slabs = 3 * C_W // IN_SLAB
    out_refs, p_refs, f_refs = refs[:-4], refs[-4:-2], refs[-2:]
    tm = x_ref.shape[0]
    assert [d for _, d in C_GROUPS] == [1, 4, 16]
    hn = _rms(x_ref[...], g_ref[...]).astype(BF16)

    def project(slab):
        cols = slice(slab * IN_SLAB, (slab + 1) * IN_SLAB)
        return jnp.dot(hn, w_ref[:, cols], preferred_element_type=F32)

    ahead = [project(s) for s in range(IN_AHEAD)]
    for slab in range(n_slabs):
        p = ahead.pop(0)
        if slab + IN_AHEAD < n_slabs:
            ahead.append(project(slab + IN_AHEAD))
        part, col0 = divmod(slab * IN_SLAB, C_W)
        if part == 0:
            p = p * (SCALE * LOG2E)
        p_ref, f_ref = p_refs[slab % 2], f_refs[slab % 2]
        dst1, dst4, dst16 = out_refs[part * 3:part * 3 + 3]
        dst1[0, 0, :, col0:col0 + IN_SLAB] = p.astype(BF16)
        for cb in range(IN_SLAB // LANES):
            cols = slice(col0 + cb * LANES, col0 + (cb + 1) * LANES)
            p_ref[cb] = p[:, cb * LANES:(cb + 1) * LANES]
            for r4 in range(4):
                by4 = p_ref[cb, pl.ds(r4, tm // 4, stride=4), :]
                dst4[0, r4, :, cols] = by4.astype(BF16)
                f_ref[cb, r4] = by4
                for j in range(4):
                    by16 = f_ref[cb, r4, pl.ds(j, tm // 16, stride=4), :]
                    dst16[0, r4 + 4 * j, :, cols] = by16.astype(BF16)


def _in_odd(x2, bn, seq, g, w):
    n = x2.shape[0]
    tm = ROW_TILE
    nt = seq // tm
    specs, shapes = [], []
    for _ in range(3):
        for _, dil in C_GROUPS:
            specs.append(pl.BlockSpec((1, dil, tm // dil, C_W), lambda i: (i // nt, 0, i % nt, 0)))
            shapes.append(jax.ShapeDtypeStruct((bn, dil, seq // dil, C_W), BF16))
    outs = pl.pallas_call(
        _in_odd_kernel,
        grid=(n // tm,),
        in_specs=[pl.BlockSpec((tm, D_MODEL), lambda i: (i, 0)), _resident((1, D_MODEL)), _resident(w.shape)],
        out_specs=tuple(specs),
        out_shape=tuple(shapes),
        scratch_shapes=[pltpu.VMEM((IN_SLAB // LANES, tm, LANES), F32)] * 2
        + [pltpu.VMEM((IN_SLAB // LANES, 4, tm // 4, LANES), F32)] * 2,
        compiler_params=pltpu.CompilerParams(dimension_semantics=("parallel",), vmem_limit_bytes=VMEM_LIMIT),
        name="in_odd",
    )(x2, g, w)
    outs = [a.reshape(a.shape[0] * a.shape[1], a.shape[2], C_W) for a in outs]
    n_g = len(C_GROUPS)
    return outs[:n_g], outs[n_g:2 * n_g], outs[2 * n_g:]


def _band_kernel(sl_ref, q_ref, kp_ref, kc_ref, kn_ref, vp_ref, vc_ref, vn_ref, o_ref, lse_ref,
                 kwin_ref, vt_ref, pen_ref, *, length, tile):
    t = pl.program_id(1)
    n_pairs = C_W // LANES
    win = C_TQ + 2 * C_HALF
    span = tile + 2 * C_HALF
    kwin_ref[0:C_HALF, :] = kp_ref[C_TQ - C_HALF:C_TQ, :]
    kwin_ref[C_HALF:C_HALF + tile, :] = kc_ref[...]
    kwin_ref[C_HALF + tile:span, :] = kn_ref[0:C_HALF, :]

    @pl.when((pl.program_id(0) == 0) & (t == 0))
    def _():
        key = lax.broadcasted_iota(jnp.int32, (win, 2 * C_TQ), 0)
        lane = lax.broadcasted_iota(jnp.int32, (win, 2 * C_TQ), 1)
        dist = jnp.abs(key - C_HALF - (lane & (C_TQ - 1)))
        distf = dist.astype(F32)
        for c in range(n_pairs):
            slope = jnp.where(lane < C_TQ, sl_ref[2 * c], sl_ref[2 * c + 1])
            pen = jnp.where(dist <= C_HALF, slope * distf, -NEG)
            pen_ref[0, c] = pen
            pen_ref[1, c] = jnp.where(key >= C_HALF, pen, -NEG)
            pen_ref[2, c] = jnp.where(key < win - C_HALF, pen, -NEG)

    ones = jnp.ones((V_ROWS - HEAD_DIM, span), BF16)
    for c in range(n_pairs):
        cs = slice(c * LANES, (c + 1) * LANES)
        vwin = jnp.concatenate([vp_ref[C_TQ - C_HALF:C_TQ, cs], vc_ref[:, cs], vn_ref[0:C_HALF, cs]], axis=0)
        vt = vwin.T
        for hh in range(2):
            vt_ref[c, hh * V_ROWS:hh * V_ROWS + HEAD_DIM, :] = vt[hh * HEAD_DIM:(hh + 1) * HEAD_DIM, :]
            vt_ref[c, hh * V_ROWS + HEAD_DIM:(hh + 1) * V_ROWS, :] = ones

    def scores(j, c):
        cs = slice(c * LANES, (c + 1) * LANES)
        qs = _split_pair(q_ref[j * C_TQ:(j + 1) * C_TQ, cs])
        k = kwin_ref[j * C_TQ:j * C_TQ + win, cs]
        return lax.dot_general(k, qs, _NT, preferred_element_type=F32)

    n_blocks = tile // C_TQ
    assert n_blocks >= 2
    table = {0: jnp.where(t == 0, 1, 0), n_blocks - 1: jnp.where(t == length // tile - 1, 2, 0)}
    items = [(j, c) for j in range(n_blocks) for c in range(n_pairs)]
    ahead = [scores(*item) for item in items[:C_AHEAD]]
    for n_item, (j, c) in enumerate(items):
        rows = slice(j * C_TQ, (j + 1) * C_TQ)
        cs = slice(c * LANES, (c + 1) * LANES)
        s = ahead.pop(0) - pen_ref[table.get(j, 0), c]
        if n_item + C_AHEAD < len(items):
            ahead.append(scores(*items[n_item + C_AHEAD]))
        m = jnp.max(s, axis=0, keepdims=True)
        p = jnp.exp2(s - m).astype(BF16)
        acc = jnp.dot(vt_ref[c, :, j * C_TQ:j * C_TQ + win], p, preferred_element_type=F32)
        a0, a1 = acc[:V_ROWS, :C_TQ], acc[V_ROWS:, C_TQ:]
        o_ref[rows, cs] = _finish_pair(a0, a1)
        lse_ref[2 * c:2 * c + 1, rows] = m[:, :C_TQ] + jnp.log2(a0[HEAD_DIM:HEAD_DIM + 1])
        lse_ref[2 * c + 1:2 * c + 2, rows] = m[:, C_TQ:] + jnp.log2(a1[HEAD_DIM:HEAD_DIM + 1])


def _band(slopes, q, k, v):
    nb_, length, _ = q.shape
    tile = next(t for t in (1024, 512, 256) if length % t == 0)
    per = tile // C_TQ
    last = length // C_TQ - 1
    cur = lambda n, t: (n, t, 0)
    prev = lambda n, t: (n, jnp.maximum(t * per - 1, 0), 0)
    nxt = lambda n, t: (n, jnp.minimum((t + 1) * per, last), 0)
    big = lambda m: pl.BlockSpec((None, tile, C_W), m)
    halo = lambda m: pl.BlockSpec((None, C_TQ, C_W), m)
    span = tile + 2 * C_HALF
    return pl.pallas_call(
        functools.partial(_band_kernel, length=length, tile=tile),
        grid=(nb_, length // tile),
        in_specs=[pl.BlockSpec(memory_space=pltpu.SMEM), big(cur),
                  halo(prev), big(cur), halo(nxt), halo(prev), big(cur), halo(nxt)],
        out_specs=(big(cur), pl.BlockSpec((None, C_HEADS, tile), lambda n, t: (n, 0, t))),
        out_shape=(jax.ShapeDtypeStruct((nb_, length, C_W), BF16),
                   jax.ShapeDtypeStruct((nb_, C_HEADS, length), F32)),
        scratch_shapes=[pltpu.VMEM((span, C_W), BF16), pltpu.VMEM((C_W // LANES, 2 * V_ROWS, span), BF16),
                        pltpu.VMEM((3, C_W // LANES, C_TQ + 2 * C_HALF, 2 * C_TQ), F32)],
        compiler_params=pltpu.CompilerParams(dimension_semantics=("arbitrary", "arbitrary"),
                                             vmem_limit_bytes=VMEM_LIMIT),
        name="band",
    )(slopes, q, k, k, k, v, v, v)


def _rope_tables(seq):
    t = jnp.arange(seq)
    n = HEAD_DIM // 4
    freqs = jnp.power(ROPE_THETA, -jnp.arange(n, dtype=F32) / n)
    ang_r = (t // GRID_W).astype(F32)[:, None] * freqs[None, :]
    ang_c = (t % GRID_W).astype(F32)[:, None] * freqs[None, :]
    zero = jnp.zeros_like(ang_r)
    cos = jnp.concatenate([jnp.cos(ang_r)] * 2 + [jnp.cos(ang_c)] * 2, axis=-1)
    sin_r, sin_c = jnp.sin(ang_r), jnp.sin(ang_c)
    sa = jnp.concatenate([-sin_r, zero, -sin_c, zero], axis=-1)
    sb = jnp.concatenate([zero, sin_r, zero, sin_c], axis=-1)
    two = lambda a: jnp.concatenate([a, a], axis=-1)
    return two(cos), two(sa), two(sb)


def _trunk(x, prm):
    bn, seq, _ = x.shape
    n = bn * seq
    assert x.shape[2] == D_MODEL and seq % (16 * 256) == 0, x.shape
    x2 = x.reshape(n, D_MODEL)
    cos, sa, sb = prm["rope"]
    qa, ka, vat, qb, kb, vbt = _in_even(x2, seq, prm["g_mix0"], prm["w_in_even"], prm["gq"], prm["gk"],
                                        cos, sa, sb, prm["bd"])
    oa = _mix_a(qa, ka, vat, bn, seq)
    ob = _mix_b(qb, kb, vbt, _neighbourhood_bias(prm["rpb"], seq // GRID_W), bn, seq)
    x2 = _post_even(x2, oa, ob, prm["wo_a"], prm["wo_b"], prm["g_ffn0"], prm["wg0"], prm["wu0"], prm["wd0"])

    qs, ks, vs = _in_odd(x2, bn, seq, prm["g_mix1"], prm["w_in_odd"])
    outs, lses = [], []
    for (_, dil), q, k, v in zip(C_GROUPS, qs, ks, vs):
        o, lse = _band(prm["slopes"] * dil * LOG2E, q, k, v)
        outs.append(o)
        lses.append(lse.reshape(bn, dil, C_HEADS, seq // dil).transpose(0, 3, 1, 2).reshape(n, C_HEADS))
    y = _post_odd(x2, seq, outs, lses, prm["expand"], prm["wo_odd"], prm["g_ffn1"], prm["wg1"], prm["wu1"],
                  prm["wd1"], prm["g_final"])
    return y.reshape(bn, seq, D_MODEL)


def kernel(x_prompt, x_sample, norm_mix, norm_ffn, norm_final, w_in_even, a_q_norm, a_k_norm, na_rpb,
           w_out_even, w_in_odd, w_out_odd, w_gate_up, w_down):
    head = np.arange(LANES) // HEAD_DIM
    order = np.array([0, 4, 1, 5, 2, 6, 3, 7])
    w_in0 = w_in_even[0]
    w_qa = w_in0[:, :A_Q].reshape(D_MODEL, A_Q // HEAD_DIM, HEAD_DIM)[:, order].reshape(D_MODEL, A_Q)
    w_in0 = jnp.concatenate([w_qa, w_in0[:, A_Q:]], axis=1)
    wo_a = w_out_even[0, :A_Q].reshape(A_Q // HEAD_DIM, HEAD_DIM, D_MODEL)[order].reshape(A_Q, D_MODEL)
    prm = {
        "g_mix0": norm_mix[0][None].astype(F32), "g_mix1": norm_mix[1][None].astype(F32),
        "g_ffn0": norm_ffn[0][None].astype(F32), "g_ffn1": norm_ffn[1][None].astype(F32),
        "g_final": norm_final[None].astype(F32),
        "w_in_even": w_in0.astype(BF16), "w_in_odd": w_in_odd[0].astype(BF16),
        "gq": jnp.tile(a_q_norm[0].astype(F32) * (SCALE * LOG2E), 2)[None], "gk": jnp.tile(a_k_norm[0].astype(F32), 2)[None],
        "bd": jnp.asarray((head[:, None] == head[None, :]) / HEAD_DIM, BF16),
        "rpb": na_rpb[0],
        "rope": _rope_tables(max(x_prompt.shape[1], x_sample.shape[1])),
        "wo_a": wo_a.astype(BF16), "wo_b": w_out_even[0, A_Q:].astype(BF16),
        "wo_odd": w_out_odd[0].astype(BF16),
        "wg0": w_gate_up[0, :, :D_FF].astype(BF16), "wu0": w_gate_up[0, :, D_FF:].astype(BF16),
        "wg1": w_gate_up[1, :, :D_FF].astype(BF16), "wu1": w_gate_up[1, :, D_FF:].astype(BF16),
        "wd0": w_down[0].astype(BF16), "wd1": w_down[1].astype(BF16),
        "slopes": jnp.exp2(-8.0 * (jnp.arange(C_HEADS, dtype=F32) + 1.0) / C_HEADS),
        "expand": jnp.asarray(np.tile(np.arange(C_HEADS), 2)[:, None] == (np.arange(C_W) // HEAD_DIM)[None, :], BF16),
    }
    return (_trunk(x_prompt, prm), _trunk(x_sample, prm))
```

```python
import functools

import jax
import jax.numpy as jnp
import numpy as np
from jax import lax
from jax.experimental import pallas as pl
from jax.experimental.pallas import tpu as pltpu

F32 = jnp.float32
BF16 = jnp.bfloat16

D_MODEL = 1024
HEAD_DIM = 64
A_Q = 512
A_KV = 128
B_W = 512
C_W = 1024
C_HEADS = 16
B_HEADS = 8
A_HEADS = A_Q // HEAD_DIM
GRID_W = 64
NA_ROWS = 8
NA_COLS = 16
C_GROUPS = ((128, 1), (512, 4), (2048, 16))
ROPE_THETA = 10000.0
D_FF = 2816
EPS = 1e-6
NEG = -1e30
SCALE = HEAD_DIM ** -0.5
LOG2E = 1.4426950408889634
ROPE_HALF = HEAD_DIM // 4

LANES = 128
BF16_ROWS = 16
MXU_TILE = 256
VMEM_BYTES = 64 * 1024 * 1024

V_ROWS = HEAD_DIM + BF16_ROWS
ROW_TILE = 512
A_TQ = 256
A_TK = 256
A_UNROLL = 16
A_AHEAD = 5
B_TQ = 256
B_TK = 3 * B_TQ
B_AHEAD = 4
C_TQ = 128
C_HALF = 64
IN_SLAB = MXU_TILE
IN_AHEAD = 2
C_TILES = (1024, 512, 256)
C_AHEAD = 5
C_AHEAD_LONG = 4
VMEM_LIMIT = VMEM_BYTES * 7 // 8

_NT = (((1,), (1,)), ((), ()))


def _resident(shape):
    zeros = (0,) * len(shape)
    return pl.BlockSpec(shape, lambda *_: zeros, pipeline_mode=pl.Buffered(1))


def _rms(x, gain):
    ms = jnp.mean(x * x, axis=-1, keepdims=True)
    return x * lax.rsqrt(ms + EPS) * gain


def _in_even_kernel(x_ref, g_ref, w_ref, gq_ref, gk_ref, cos_ref, sa_ref, sb_ref, bd_ref,
                    qa_ref, ka_ref, vat_ref, qb_ref, kb_ref, vbt_ref):
    hn = _rms(x_ref[...], g_ref[...]).astype(BF16)
    tm = hn.shape[0]
    cos, sa, sb, bd = cos_ref[...], sa_ref[...], sb_ref[...], bd_ref[...]

    def project(slab):
        cols = slice(slab * IN_SLAB, (slab + 1) * IN_SLAB)
        return jnp.dot(hn, w_ref[:, cols], preferred_element_type=F32)

    def headnorm_rope(c, gain):
        c2 = c * c
        hi = c2.astype(BF16)
        lo = (c2 - hi.astype(F32)).astype(BF16)
        ms = (jnp.dot(hi, bd, preferred_element_type=F32)
              + jnp.dot(lo, bd, preferred_element_type=F32))
        y = c * lax.rsqrt(ms + EPS) * gain
        return y * cos + pltpu.roll(y, LANES - ROPE_HALF, 1) * sa + pltpu.roll(y, ROPE_HALF, 1) * sb

    def put_transposed(dst_ref, chunk, c):
        ones = jnp.ones((V_ROWS - HEAD_DIM, A_TK), BF16)
        for j in range(tm // A_TK):
            t = chunk[j * A_TK:(j + 1) * A_TK, :].astype(BF16).T
            for hh in range(2):
                r0 = (2 * c + hh) * V_ROWS
                dst_ref[j, r0:r0 + HEAD_DIM, :] = t[hh * HEAD_DIM:(hh + 1) * HEAD_DIM, :]
                dst_ref[j, r0 + HEAD_DIM:r0 + V_ROWS, :] = ones

    def write_out(chunk, col):
        lanes = lambda base: slice(col - base, col - base + LANES)
        o = A_Q + 2 * A_KV
        if col < A_Q:
            qa_ref[:, lanes(0)] = headnorm_rope(chunk, gq_ref[...]).astype(BF16)
        elif col < A_Q + A_KV:
            ka_ref[...] = headnorm_rope(chunk, gk_ref[...]).astype(BF16)
        elif col < o:
            put_transposed(vat_ref, chunk, 0)
        elif col < o + B_W:
            qb_ref[:, lanes(o)] = (chunk * (SCALE * LOG2E)).astype(BF16)
        elif col < o + 2 * B_W:
            kb_ref[:, lanes(o + B_W)] = chunk.astype(BF16)
        else:
            put_transposed(vbt_ref, chunk, (col - o - 2 * B_W) // LANES)

    n_slabs = w_ref.shape[1] // IN_SLAB
    ahead = [project(s) for s in range(IN_AHEAD)]
    for slab in range(n_slabs):
        p = ahead.pop(0)
        if slab + IN_AHEAD < n_slabs:
            ahead.append(project(slab + IN_AHEAD))
        for i in range(IN_SLAB // LANES):
            write_out(p[:, i * LANES:(i + 1) * LANES], slab * IN_SLAB + i * LANES)


def _in_even(x2, seq, g, w, gq, gk, cos, sa, sb, bd):
    n = x2.shape[0]
    tm = ROW_TILE
    nt = seq // tm
    row = lambda i: (i, 0)
    pos = lambda i: (i % nt, 0)
    va_rows, vb_rows = (A_KV // HEAD_DIM) * V_ROWS, B_HEADS * V_ROWS
    out_shape = (
        jax.ShapeDtypeStruct((n, A_Q), BF16), jax.ShapeDtypeStruct((n, A_KV), BF16),
        jax.ShapeDtypeStruct((n // A_TK, va_rows, A_TK), BF16),
        jax.ShapeDtypeStruct((n, B_W), BF16), jax.ShapeDtypeStruct((n, B_W), BF16),
        jax.ShapeDtypeStruct((n // A_TK, vb_rows, A_TK), BF16))
    return pl.pallas_call(
        _in_even_kernel,
        grid=(n // tm,),
        in_specs=[pl.BlockSpec((tm, D_MODEL), row), _resident((1, D_MODEL)), _resident(w.shape),
                  _resident((1, LANES)), _resident((1, LANES)),
                  pl.BlockSpec((tm, LANES), pos), pl.BlockSpec((tm, LANES), pos), pl.BlockSpec((tm, LANES), pos),
                  _resident((LANES, LANES))],
        out_specs=(pl.BlockSpec((tm, A_Q), row), pl.BlockSpec((tm, A_KV), row),
                   pl.BlockSpec((tm // A_TK, va_rows, A_TK), lambda i: (i, 0, 0)),
                   pl.BlockSpec((tm, B_W), row), pl.BlockSpec((tm, B_W), row),
                   pl.BlockSpec((tm // A_TK, vb_rows, A_TK), lambda i: (i, 0, 0))),
        out_shape=out_shape,
        compiler_params=pltpu.CompilerParams(dimension_semantics=("parallel",), vmem_limit_bytes=VMEM_LIMIT),
        name="in_even",
    )(x2, g, w, gq, gk, cos, sa, sb, bd)


def _split_pair(qp):
    low = lax.broadcasted_iota(jnp.int32, qp.shape, 1) < HEAD_DIM
    zero = jnp.zeros(qp.shape, qp.dtype)
    return jnp.concatenate([jnp.where(low, qp, zero), jnp.where(low, zero, qp)], axis=0)


def _finish_pair(acc_lo, acc_hi):
    halves = [a[:HEAD_DIM] / a[HEAD_DIM:HEAD_DIM + 1] for a in (acc_lo, acc_hi)]
    return jnp.concatenate(halves, axis=0).astype(BF16).T


def _mix_a_kernel(q_ref, k_ref, vt_ref, o_ref, qs_ref, m_ref, acc_ref, *, n_chunks):
    tq = A_TQ
    n_pairs = A_Q // LANES
    for c in range(n_pairs):
        qs_ref[:, 2 * c * tq:(2 * c + 2) * tq] = (
            _split_pair(q_ref[:, c * LANES:(c + 1) * LANES]).T)
    m_ref[...] = jnp.full(m_ref.shape, NEG, F32)
    acc_ref[...] = jnp.zeros(acc_ref.shape, F32)

    def scores(ci, h):
        k = k_ref[pl.ds(pl.multiple_of(ci * A_TK, A_TK), A_TK), :]
        return jnp.dot(k, qs_ref[:, h * tq:(h + 1) * tq], preferred_element_type=F32)

    def chunks(it, carry):
        items = [(it * A_UNROLL + u, h) for u in range(A_UNROLL) for h in range(2 * n_pairs)]
        ahead = [scores(*item) for item in items[:A_AHEAD]]
        for t, (ci, h) in enumerate(items):
            s = ahead.pop(0)
            if t + A_AHEAD < len(items):
                ahead.append(scores(*items[t + A_AHEAD]))
            cols = slice(h * tq, (h + 1) * tq)
            g = h % 2
            m_old = m_ref[:, cols]
            m_new = jnp.maximum(m_old, jnp.max(s, axis=0, keepdims=True))
            alpha = jnp.exp2(m_old - m_new)
            p = jnp.exp2(s - m_new).astype(BF16)
            m_ref[:, cols] = m_new
            vt = vt_ref[ci, g * V_ROWS:(g + 1) * V_ROWS, :]
            acc_ref[h] = alpha * acc_ref[h] + jnp.dot(vt, p, preferred_element_type=F32)
        return carry

    lax.fori_loop(0, n_chunks // A_UNROLL, chunks, 0)

    for c in range(n_pairs):
        o_ref[:, c * LANES:(c + 1) * LANES] = _finish_pair(acc_ref[2 * c], acc_ref[2 * c + 1])


def _mix_a(qa, ka, vat, bn, seq):
    n = qa.shape[0]
    nq = seq // A_TQ
    nc = seq // A_TK
    assert seq % A_TQ == 0 and nc % A_UNROLL == 0, seq
    return pl.pallas_call(
        functools.partial(_mix_a_kernel, n_chunks=nc),
        grid=(bn, nq),
        in_specs=[pl.BlockSpec((A_TQ, A_Q), lambda b, i: (b * nq + i, 0)),
                  pl.BlockSpec((seq, A_KV), lambda b, i: (b, 0)),
                  pl.BlockSpec((nc, vat.shape[1], A_TK), lambda b, i: (b, 0, 0))],
        out_specs=pl.BlockSpec((A_TQ, A_Q), lambda b, i: (b * nq + i, 0)),
        out_shape=jax.ShapeDtypeStruct((n, A_Q), BF16),
        scratch_shapes=[pltpu.VMEM((LANES, A_HEADS * A_TQ), BF16), pltpu.VMEM((1, A_HEADS * A_TQ), F32),
                        pltpu.VMEM((A_HEADS, V_ROWS, A_TQ), F32)],
        compiler_params=pltpu.CompilerParams(dimension_semantics=("parallel", "parallel"),
                                             vmem_limit_bytes=VMEM_LIMIT),
        name="mix_a",
    )(qa, ka, vat)


def _mix_b_kernel(q_ref, k0_ref, k1_ref, k2_ref, vt0_ref, vt1_ref, vt2_ref, bias_ref, o_ref,
                  qs_ref, m_ref, acc_ref):
    tq = B_TQ
    n_pairs = B_W // LANES
    k_refs = (k0_ref, k1_ref, k2_ref)
    vt_refs = (vt0_ref, vt1_ref, vt2_ref)
    for c in range(n_pairs):
        qs_ref[:, 2 * c * tq:(2 * c + 2) * tq] = (
            _split_pair(q_ref[:, c * LANES:(c + 1) * LANES]).T)

    def scores(h, j):
        c = h // 2
        return jnp.dot(k_refs[j][:, c * LANES:(c + 1) * LANES], qs_ref[:, h * tq:(h + 1) * tq],
                       preferred_element_type=F32)

    items = [(h, j) for h in range(2 * n_pairs) for j in range(len(k_refs))]
    ahead = [scores(*item) for item in items[:B_AHEAD]]
    for t, (h, j) in enumerate(items):
        c, hh = divmod(h, 2)
        s = bias_ref[0, c, j * tq:(j + 1) * tq, hh * tq:(hh + 1) * tq] + ahead.pop(0)
        if t + B_AHEAD < len(items):
            ahead.append(scores(*items[t + B_AHEAD]))
        rows = slice(h * V_ROWS, (h + 1) * V_ROWS)
        m_new = jnp.max(s, axis=0, keepdims=True)
        if j > 0:
            m_old = m_ref[h]
            m_new = jnp.maximum(m_old, m_new)
        pv = jnp.dot(vt_refs[j][rows, :], jnp.exp2(s - m_new).astype(BF16), preferred_element_type=F32)
        acc_ref[h] = pv if j == 0 else jnp.exp2(m_old - m_new) * acc_ref[h] + pv
        m_ref[h] = m_new
        if j == len(k_refs) - 1 and hh == 1:
            o_ref[:, c * LANES:(c + 1) * LANES] = _finish_pair(acc_ref[h - 1], acc_ref[h])


def _mix_b(qb, kb, vbt, bias, bn, seq):
    n = qb.shape[0]
    nb = seq // B_TQ
    q_map = lambda b, i: (b * nb + i, 0)

    def kv_map(j):
        return lambda b, i: (b * nb + jnp.clip(i - 1, 0, nb - 3) + j, 0)

    def vt_map(j):
        return lambda b, i: (b * nb + jnp.clip(i - 1, 0, nb - 3) + j, 0, 0)

    def bias_map(b, i):
        return (jnp.where(i == 0, 0, jnp.where(i == nb - 1, 2, 1)), 0, 0, 0)

    blk = lambda m: pl.BlockSpec((B_TQ, B_W), m)
    vblk = lambda m: pl.BlockSpec((None, vbt.shape[1], B_TQ), m)
    return pl.pallas_call(
        _mix_b_kernel,
        grid=(bn, nb),
        in_specs=[blk(q_map), blk(kv_map(0)), blk(kv_map(1)), blk(kv_map(2)),
                  vblk(vt_map(0)), vblk(vt_map(1)), vblk(vt_map(2)),
                  pl.BlockSpec((1, B_W // LANES, B_TK, 2 * B_TQ), bias_map)],
        out_specs=blk(q_map),
        out_shape=jax.ShapeDtypeStruct((n, B_W), BF16),
        scratch_shapes=[pltpu.VMEM((LANES, B_HEADS * B_TQ), BF16), pltpu.VMEM((B_HEADS, 1, B_TQ), F32),
                        pltpu.VMEM((B_HEADS, V_ROWS, B_TQ), F32)],
        compiler_params=pltpu.CompilerParams(dimension_semantics=("parallel", "parallel"),
                                             vmem_limit_bytes=VMEM_LIMIT),
        name="mix_b",
    )(qb, kb, kb, kb, vbt, vbt, vbt, bias)


def _neighbourhood_bias(rpb, rows):
    n_dc = 2 * NA_COLS - 1
    c = np.arange(GRID_W)
    cs = np.clip(c - NA_COLS // 2, 0, GRID_W - NA_COLS)
    col_ok = (c[:, None] >= cs[None, :]) & (c[:, None] < cs[None, :] + NA_COLS)
    dc = np.clip(c[:, None] - c[None, :] + NA_COLS - 1, 0, n_dc - 1)
    pick = (dc.reshape(-1)[None, :] == np.arange(n_dc)[:, None]).astype(np.float32)
    by_col = jnp.einsum("hrd,dx->hrx", rpb.astype(F32) * LOG2E, jnp.asarray(pick),
                        precision=lax.Precision.HIGHEST)
    by_col = by_col.reshape(B_HEADS, 2 * NA_ROWS - 1, GRID_W, GRID_W)
    by_col = jnp.where(col_ok[None, None], by_col, NEG)
    masked = jnp.full((B_HEADS, GRID_W, GRID_W), NEG, F32)
    tables = []
    for r0, ws in ((0, 0), (4, 0), (rows - 4, rows - 12)):
        key_rows = []
        for kr in range(B_TK // GRID_W):
            blocks = []
            for rr in range(B_TQ // GRID_W):
                r = r0 + rr
                rs = min(max(r - NA_ROWS // 2, 0), rows - NA_ROWS)
                ok = rs <= ws + kr < rs + NA_ROWS
                blocks.append(by_col[:, ws + kr - r + NA_ROWS - 1] if ok else masked)
            key_rows.append(jnp.concatenate(blocks, axis=-1))
        t = jnp.concatenate(key_rows, axis=-2)
        t = t.reshape(B_HEADS // 2, 2, B_TK, B_TQ).transpose(0, 2, 1, 3).reshape(B_HEADS // 2, B_TK, 2 * B_TQ)
        tables.append(t)
    return jnp.stack(tables, axis=0)


def _ffn(x1, gain, wg_ref, wu_ref, wd_ref):
    hn = _rms(x1, gain).astype(BF16)
    gate = jnp.dot(hn, wg_ref[...], preferred_element_type=F32)
    up = jnp.dot(hn, wu_ref[...], preferred_element_type=F32)
    act = (gate * (1.0 / (1.0 + jnp.exp(-gate))) * up).astype(BF16)
    return x1 + jnp.dot(act, wd_ref[...], preferred_element_type=F32)


def _post_even_kernel(x_ref, oa_ref, ob_ref, woa_ref, wob_ref, g_ref, wg_ref, wu_ref, wd_ref, y_ref):
    mix = (jnp.dot(oa_ref[...], woa_ref[...], preferred_element_type=F32)
           + jnp.dot(ob_ref[...], wob_ref[...], preferred_element_type=F32))
    y_ref[...] = _ffn(x_ref[...] + mix, g_ref[...], wg_ref, wu_ref, wd_ref)


def _post_even(x2, oa, ob, woa, wob, g, wg, wu, wd):
    n = x2.shape[0]
    tm = ROW_TILE
    row = lambda i: (i, 0)
    return pl.pallas_call(
        _post_even_kernel,
        grid=(n // tm,),
        in_specs=[pl.BlockSpec((tm, D_MODEL), row), pl.BlockSpec((tm, A_Q), row), pl.BlockSpec((tm, B_W), row),
                  _resident(woa.shape), _resident(wob.shape), _resident((1, D_MODEL)),
                  _resident(wg.shape), _resident(wu.shape), _resident(wd.shape)],
        out_specs=pl.BlockSpec((tm, D_MODEL), row),
        out_shape=jax.ShapeDtypeStruct((n, D_MODEL), F32),
        compiler_params=pltpu.CompilerParams(dimension_semantics=("parallel",), vmem_limit_bytes=VMEM_LIMIT),
        name="post_even",
    )(x2, oa, ob, woa, wob, g, wg, wu, wd)


def _post_odd_kernel(x_ref, o1_ref, of2_ref, of3_ref, l1_ref, l2_ref, l3_ref, ex_ref, wo_ref, g_ref,
                     wg_ref, wu_ref, wd_ref, gf_ref, y_ref, o2_ref, o3_ref):
    tm = x_ref.shape[0]
    n_cb = C_W // LANES
    for (_, dil), src, dst in zip(C_GROUPS[1:], (of2_ref, of3_ref), (o2_ref, o3_ref)):
        for rho in range(dil):
            for cb in range(n_cb):
                dst[cb, pl.ds(rho, tm // dil, stride=dil), :] = src[0, rho, :, cb * LANES:(cb + 1) * LANES].astype(F32)
    o2 = jnp.concatenate([o2_ref[cb] for cb in range(n_cb)], axis=1)
    o3 = jnp.concatenate([o3_ref[cb] for cb in range(n_cb)], axis=1)
    l1, l2, l3 = l1_ref[...], l2_ref[...], l3_ref[...]
    mx = jnp.maximum(jnp.maximum(l1, l2), l3)
    e1, e2, e3 = jnp.exp2(l1 - mx), jnp.exp2(l2 - mx), jnp.exp2(l3 - mx)
    den = e1 + e2 + e3
    ex = ex_ref[...]

    def widen(w):
        hi = w.astype(BF16).astype(F32)
        return jnp.dot(jnp.concatenate([hi, w - hi], axis=1).astype(BF16), ex, preferred_element_type=F32)

    o = o3 + widen(e1 / den) * (o1_ref[...].astype(F32) - o3) + widen(e2 / den) * (o2 - o3)
    mix = jnp.dot(o.astype(BF16), wo_ref[...], preferred_element_type=F32)
    y = _ffn(x_ref[...] + mix, g_ref[...], wg_ref, wu_ref, wd_ref)
    y_ref[...] = _rms(y, gf_ref[...])


def _post_odd(x2, seq, os_, ls_, ex, wo, g, wg, wu, wd, gf):
    n = x2.shape[0]
    tm = ROW_TILE
    nt = seq // tm
    row = lambda i: (i, 0)
    wide = pl.BlockSpec((tm, D_MODEL), row)
    narrow = pl.BlockSpec((tm, C_HEADS), row)
    folded = [pl.BlockSpec((1, dil, tm // dil, C_W), lambda i: (i // nt, 0, i % nt, 0)) for _, dil in C_GROUPS[1:]]
    os_ = [os_[0].reshape(n, C_W)] + [o.reshape(n // seq, dil, seq // dil, C_W)
                                      for o, (_, dil) in zip(os_[1:], C_GROUPS[1:])]
    return pl.pallas_call(
        _post_odd_kernel,
        grid=(n // tm,),
        in_specs=[wide, wide, *folded, narrow, narrow, narrow, _resident(ex.shape), _resident(wo.shape),
                  _resident((1, D_MODEL)), _resident(wg.shape), _resident(wu.shape), _resident(wd.shape),
                  _resident((1, D_MODEL))],
        out_specs=wide,
        out_shape=jax.ShapeDtypeStruct((n, D_MODEL), F32),
        scratch_shapes=[pltpu.VMEM((C_W // LANES, tm, LANES), F32), pltpu.VMEM((C_W // LANES, tm, LANES), F32)],
        compiler_params=pltpu.CompilerParams(dimension_semantics=("parallel",), vmem_limit_bytes=VMEM_LIMIT),
        name="post_odd",
    )(x2, *os_, *ls_, ex, wo, g, wg, wu, wd, gf)


def _in_odd_kernel(x_ref, g_ref, w_ref, *refs):
    n_slabs = 3 * C_W // IN_SLAB
    out_refs, p_refs, f_refs = refs[:-4], refs[-4:-2], refs[-2:]
    tm = x_ref.shape[0]
    assert [d for _, d in C_GROUPS] == [1, 4, 16]
    hn = _rms(x_ref[...], g_ref[...]).astype(BF16)

    def project(slab):
        cols = slice(slab * IN_SLAB, (slab + 1) * IN_SLAB)
        return jnp.dot(hn, w_ref[:, cols], preferred_element_type=F32)

    ahead = [project(s) for s in range(IN_AHEAD)]
    for slab in range(n_slabs):
        p = ahead.pop(0)
        if slab + IN_AHEAD < n_slabs:
            ahead.append(project(slab + IN_AHEAD))
        part, col0 = divmod(slab * IN_SLAB, C_W)
        if part == 0:
            p = p * (SCALE * LOG2E)
        p_ref, f_ref = p_refs[slab % 2], f_refs[slab % 2]
        dst1, dst4, dst16 = out_refs[part * 3:part * 3 + 3]
        dst1[0, 0, :, col0:col0 + IN_SLAB] = p.astype(BF16)
        for cb in range(IN_SLAB // LANES):
            cols = slice(col0 + cb * LANES, col0 + (cb + 1) * LANES)
            p_ref[cb] = p[:, cb * LANES:(cb + 1) * LANES]
            for r4 in range(4):
                by4 = p_ref[cb, pl.ds(r4, tm // 4, stride=4), :]
                dst4[0, r4, :, cols] = by4.astype(BF16)
                f_ref[cb, r4] = by4
                for j in range(4):
                    by16 = f_ref[cb, r4, pl.ds(j, tm // 16, stride=4), :]
                    dst16[0, r4 + 4 * j, :, cols] = by16.astype(BF16)


def _in_odd(x2, bn, seq, g, w):
    n = x2.shape[0]
    tm = ROW_TILE
    nt = seq // tm
    specs, shapes = [], []
    for _ in range(3):
        for _, dil in C_GROUPS:
            specs.append(pl.BlockSpec((1, dil, tm // dil, C_W), lambda i: (i // nt, 0, i % nt, 0)))
            shapes.append(jax.ShapeDtypeStruct((bn, dil, seq // dil, C_W), BF16))
    outs = pl.pallas_call(
        _in_odd_kernel,
        grid=(n // tm,),
        in_specs=[pl.BlockSpec((tm, D_MODEL), lambda i: (i, 0)), _resident((1, D_MODEL)), _resident(w.shape)],
        out_specs=tuple(specs),
        out_shape=tuple(shapes),
        scratch_shapes=[pltpu.VMEM((IN_SLAB // LANES, tm, LANES), F32)] * 2
        + [pltpu.VMEM((IN_SLAB // LANES, 4, tm // 4, LANES), F32)] * 2,
        compiler_params=pltpu.CompilerParams(dimension_semantics=("parallel",), vmem_limit_bytes=VMEM_LIMIT),
        name="in_odd",
    )(x2, g, w)
    outs = [a.reshape(a.shape[0] * a.shape[1], a.shape[2], C_W) for a in outs]
    n_g = len(C_GROUPS)
    return outs[:n_g], outs[n_g:2 * n_g], outs[2 * n_g:]


def _band_kernel(sl_ref, q_ref, kp_ref, kc_ref, kn_ref, vp_ref, vc_ref, vn_ref, o_ref, lse_ref,
                 kwin_ref, vt_ref, pen_ref, *, length, tile, n_ahead):
    t = pl.program_id(1)
    n_pairs = C_W // LANES
    win = C_TQ + 2 * C_HALF
    span = tile + 2 * C_HALF
    kwin_ref[0:C_HALF, :] = kp_ref[C_TQ - C_HALF:C_TQ, :]
    kwin_ref[C_HALF:C_HALF + tile, :] = kc_ref[...]
    kwin_ref[C_HALF + tile:span, :] = kn_ref[0:C_HALF, :]

    @pl.when((pl.program_id(0) == 0) & (t == 0))
    def _():
        key = lax.broadcasted_iota(jnp.int32, (win, 2 * C_TQ), 0)
        lane = lax.broadcasted_iota(jnp.int32, (win, 2 * C_TQ), 1)
        dist = jnp.abs(key - C_HALF - (lane & (C_TQ - 1)))
        distf = dist.astype(F32)
        for c in range(n_pairs):
            slope = jnp.where(lane < C_TQ, sl_ref[2 * c], sl_ref[2 * c + 1])
            pen = jnp.where(dist <= C_HALF, slope * distf, -NEG)
            pen_ref[0, c] = pen
            pen_ref[1, c] = jnp.where(key >= C_HALF, pen, -NEG)
            pen_ref[2, c] = jnp.where(key < win - C_HALF, pen, -NEG)

    ones = jnp.ones((V_ROWS - HEAD_DIM, span), BF16)
    for c in range(n_pairs):
        cs = slice(c * LANES, (c + 1) * LANES)
        vwin = jnp.concatenate([vp_ref[C_TQ - C_HALF:C_TQ, cs], vc_ref[:, cs], vn_ref[0:C_HALF, cs]], axis=0)
        vt = vwin.T
        for hh in range(2):
            vt_ref[c, hh * V_ROWS:hh * V_ROWS + HEAD_DIM, :] = vt[hh * HEAD_DIM:(hh + 1) * HEAD_DIM, :]
            vt_ref[c, hh * V_ROWS + HEAD_DIM:(hh + 1) * V_ROWS, :] = ones

    def scores(j, c):
        cs = slice(c * LANES, (c + 1) * LANES)
        qs = _split_pair(q_ref[j * C_TQ:(j + 1) * C_TQ, cs])
        k = kwin_ref[j * C_TQ:j * C_TQ + win, cs]
        return lax.dot_general(k, qs, _NT, preferred_element_type=F32)

    n_blocks = tile // C_TQ
    assert n_blocks >= 2
    table = {0: jnp.where(t == 0, 1, 0), n_blocks - 1: jnp.where(t == length // tile - 1, 2, 0)}
    items = [(j, c) for j in range(n_blocks) for c in range(n_pairs)]
    ahead = [scores(*item) for item in items[:n_ahead]]
    for n_item, (j, c) in enumerate(items):
        rows = slice(j * C_TQ, (j + 1) * C_TQ)
        cs = slice(c * LANES, (c + 1) * LANES)
        s = ahead.pop(0) - pen_ref[table.get(j, 0), c]
        if n_item + n_ahead < len(items):
            ahead.append(scores(*items[n_item + n_ahead]))
        m = jnp.max(s, axis=0, keepdims=True)
        p = jnp.exp2(s - m).astype(BF16)
        acc = jnp.dot(vt_ref[c, :, j * C_TQ:j * C_TQ + win], p, preferred_element_type=F32)
        a0, a1 = acc[:V_ROWS, :C_TQ], acc[V_ROWS:, C_TQ:]
        o_ref[rows, cs] = _finish_pair(a0, a1)
        lse_ref[2 * c:2 * c + 1, rows] = m[:, :C_TQ] + jnp.log2(a0[HEAD_DIM:HEAD_DIM + 1])
        lse_ref[2 * c + 1:2 * c + 2, rows] = m[:, C_TQ:] + jnp.log2(a1[HEAD_DIM:HEAD_DIM + 1])


def _band(slopes, q, k, v):
    nb_, length, _ = q.shape
    tile = next(t for t in C_TILES if length % t == 0)
    per = tile // C_TQ
    last = length // C_TQ - 1
    cur = lambda n, t: (n, t, 0)
    prev = lambda n, t: (n, jnp.maximum(t * per - 1, 0), 0)
    nxt = lambda n, t: (n, jnp.minimum((t + 1) * per, last), 0)
    big = lambda m: pl.BlockSpec((None, tile, C_W), m)
    halo = lambda m: pl.BlockSpec((None, C_TQ, C_W), m)
    span = tile + 2 * C_HALF
    return pl.pallas_call(
        functools.partial(_band_kernel, length=length, tile=tile,
                          n_ahead=C_AHEAD_LONG if tile == C_TILES[0] else C_AHEAD),
        grid=(nb_, length // tile),
        in_specs=[pl.BlockSpec(memory_space=pltpu.SMEM), big(cur),
                  halo(prev), big(cur), halo(nxt), halo(prev), big(cur), halo(nxt)],
        out_specs=(big(cur), pl.BlockSpec((None, C_HEADS, tile), lambda n, t: (n, 0, t))),
        out_shape=(jax.ShapeDtypeStruct((nb_, length, C_W), BF16),
                   jax.ShapeDtypeStruct((nb_, C_HEADS, length), F32)),
        scratch_shapes=[pltpu.VMEM((span, C_W), BF16), pltpu.VMEM((C_W // LANES, 2 * V_ROWS, span), BF16),
                        pltpu.VMEM((3, C_W // LANES, C_TQ + 2 * C_HALF, 2 * C_TQ), F32)],
        compiler_params=pltpu.CompilerParams(dimension_semantics=("arbitrary", "arbitrary"),
                                             vmem_limit_bytes=VMEM_LIMIT),
        name="band",
    )(slopes, q, k, k, k, v, v, v)


def _rope_tables(seq):
    t = jnp.arange(seq)[:, None]
    d = jnp.arange(LANES)[None, :] % HEAD_DIM
    freq = jnp.power(ROPE_THETA, -(d % ROPE_HALF).astype(F32) / ROPE_HALF)
    pos = jnp.where(d >= HEAD_DIM // 2, t % GRID_W, t // GRID_W).astype(F32)
    ang = pos * freq
    sin = jnp.sin(ang)
    second = (d % (2 * ROPE_HALF)) >= ROPE_HALF
    sa = jnp.where(second, 0.0, -sin)
    sb = jnp.where(second, sin, 0.0)
    return jnp.cos(ang), sa, sb


def _trunk(x, prm):
    bn, seq, _ = x.shape
    n = bn * seq
    assert x.shape[2] == D_MODEL and seq % (16 * 256) == 0, x.shape
    x2 = x.reshape(n, D_MODEL)
    cos, sa, sb = prm["rope"]
    qa, ka, vat, qb, kb, vbt = _in_even(x2, seq, prm["g_mix0"], prm["w_in_even"], prm["gq"], prm["gk"],
                                        cos, sa, sb, prm["bd"])
    oa = _mix_a(qa, ka, vat, bn, seq)
    ob = _mix_b(qb, kb, vbt, _neighbourhood_bias(prm["rpb"], seq // GRID_W), bn, seq)
    x2 = _post_even(x2, oa, ob, prm["wo_a"], prm["wo_b"], prm["g_ffn0"], prm["wg0"], prm["wu0"], prm["wd0"])

    qs, ks, vs = _in_odd(x2, bn, seq, prm["g_mix1"], prm["w_in_odd"])
    outs, lses = [], []
    for (_, dil), q, k, v in zip(C_GROUPS, qs, ks, vs):
        o, lse = _band(prm["slopes"] * dil * LOG2E, q, k, v)
        outs.append(o)
        lses.append(lse.reshape(bn, dil, C_HEADS, seq // dil).transpose(0, 3, 1, 2).reshape(n, C_HEADS))
    y = _post_odd(x2, seq, outs, lses, prm["expand"], prm["wo_odd"], prm["g_ffn1"], prm["wg1"], prm["wu1"],
                  prm["wd1"], prm["g_final"])
    return y.reshape(bn, seq, D_MODEL)


def kernel(x_prompt, x_sample, norm_mix, norm_ffn, norm_final, w_in_even, a_q_norm, a_k_norm, na_rpb,
           w_out_even, w_in_odd, w_out_odd, w_gate_up, w_down):
    head = np.arange(LANES) // HEAD_DIM
    order = np.array([0, 4, 1, 5, 2, 6, 3, 7])
    w_in0 = w_in_even[0]
    w_qa = w_in0[:, :A_Q].reshape(D_MODEL, A_Q // HEAD_DIM, HEAD_DIM)[:, order].reshape(D_MODEL, A_Q)
    w_in0 = jnp.concatenate([w_qa, w_in0[:, A_Q:]], axis=1)
    wo_a = w_out_even[0, :A_Q].reshape(A_Q // HEAD_DIM, HEAD_DIM, D_MODEL)[order].reshape(A_Q, D_MODEL)
    prm = {
        "g_mix0": norm_mix[0][None].astype(F32), "g_mix1": norm_mix[1][None].astype(F32),
        "g_ffn0": norm_ffn[0][None].astype(F32), "g_ffn1": norm_ffn[1][None].astype(F32),
        "g_final": norm_final[None].astype(F32),
        "w_in_even": w_in0.astype(BF16), "w_in_odd": w_in_odd[0].astype(BF16),
        "gq": jnp.tile(a_q_norm[0].astype(F32) * (SCALE * LOG2E), 2)[None], "gk": jnp.tile(a_k_norm[0].astype(F32), 2)[None],
        "bd": jnp.asarray((head[:, None] == head[None, :]) / HEAD_DIM, BF16),
        "rpb": na_rpb[0],
        "rope": _rope_tables(max(x_prompt.shape[1], x_sample.shape[1])),
        "wo_a": wo_a.astype(BF16), "wo_b": w_out_even[0, A_Q:].astype(BF16),
        "wo_odd": w_out_odd[0].astype(BF16),
        "wg0": w_gate_up[0, :, :D_FF].astype(BF16), "wu0": w_gate_up[0, :, D_FF:].astype(BF16),
        "wg1": w_gate_up[1, :, :D_FF].astype(BF16), "wu1": w_gate_up[1, :, D_FF:].astype(BF16),
        "wd0": w_down[0].astype(BF16), "wd1": w_down[1].astype(BF16),
        "slopes": jnp.exp2(-8.0 * (jnp.arange(C_HEADS, dtype=F32) + 1.0) / C_HEADS),
        "expand": jnp.asarray(np.tile(np.arange(C_HEADS), 2)[:, None] == (np.arange(C_W) // HEAD_DIM)[None, :], BF16),
    }
    return (_trunk(x_prompt, prm), _trunk(x_sample, prm))
```
